```python
import math
import jax, jax.numpy as jnp
from jax import lax
import numpy as np


D_MODEL = 1024
BATCH = 8
SEQ = 8192
DEPTH = 1

SSM_GROUP = 16
D_SSM = D_MODEL // 2
SSM_GROUPS = D_SSM // SSM_GROUP
SSM_STATE = 64
STEP_MIN = 1e-3
STEP_MAX = 1e-1
HEAD_DIM = 64
D_ATTN = D_MODEL // 2
N_HEADS = D_ATTN // HEAD_DIM
MOBA_BLOCK = 256
MOBA_TOP_K = 3
Q_CHUNK = 32
NUM_BUCKETS = 32
MAX_DISTANCE = 128
N_GROUPS = 4
EXPERTS_PER_GROUP = 8
N_EXPERTS = N_GROUPS * EXPERTS_PER_GROUP
EXPERT_TOP_K = 2
D_EXPERT = D_MODEL // 2
RMS_EPS = 1e-6
NEG_INF = -1e30
D_IN = D_SSM + 3 * D_ATTN + 2 * D_MODEL

kernel_name = 'hybrid_s5_moba_hmoe_block'


def rms_norm(x, g):
    xf = x.astype(jnp.float32)
    y = xf * lax.rsqrt(jnp.mean(xf * xf, axis=-1, keepdims=True) + RMS_EPS)
    return (y * g.astype(jnp.float32)).astype(x.dtype)


def t5_bucket(dist):
    n = jnp.maximum(dist, 0)
    max_exact = NUM_BUCKETS // 2
    large = max_exact + (jnp.log(jnp.maximum(n, max_exact).astype(jnp.float32) / max_exact)
                         / math.log(MAX_DISTANCE / max_exact)
                         * (NUM_BUCKETS - max_exact)).astype(jnp.int32)
    return jnp.where(n < max_exact, n, jnp.minimum(large, NUM_BUCKETS - 1))


def _complex_affine_combine(e1, e2):
    a1r, a1i, b1r, b1i = e1
    a2r, a2i, b2r, b2i = e2
    return (a2r * a1r - a2i * a1i,
            a2r * a1i + a2i * a1r,
            a2r * b1r - a2i * b1i + b2r,
            a2r * b1i + a2i * b1r + b2i)


def s5_mixer(u, lam_re, lam_im, log_step, b_re, b_im, c_re, c_im, d):
    step = jnp.exp(log_step)[:, None]
    decay = jnp.exp(lam_re * step)
    a_re = decay * jnp.cos(lam_im * step)
    a_im = decay * jnp.sin(lam_im * step)
    denom = lam_re * lam_re + lam_im * lam_im
    nr, ni = a_re - 1.0, a_im
    coef_re = (nr * lam_re + ni * lam_im) / denom
    coef_im = (ni * lam_re - nr * lam_im) / denom
    bb_re = coef_re[..., None] * b_re - coef_im[..., None] * b_im
    bb_im = coef_re[..., None] * b_im + coef_im[..., None] * b_re
    bu_re = jnp.einsum('bsgh,gph->bsgp', u, bb_re)
    bu_im = jnp.einsum('bsgh,gph->bsgp', u, bb_im)
    a_re_full = jnp.broadcast_to(a_re, bu_re.shape)
    a_im_full = jnp.broadcast_to(a_im, bu_re.shape)
    _, _, h_re, h_im = lax.associative_scan(
        _complex_affine_combine, (a_re_full, a_im_full, bu_re, bu_im), axis=1)
    y = (jnp.einsum('bsgp,ghp->bsgh', h_re, c_re)
         - jnp.einsum('bsgp,ghp->bsgh', h_im, c_im)
         + d * u)
    return y


def moba_attention(q, k, v, rel_bias):
    B, H, S, Dh = q.shape
    nb = -(-S // MOBA_BLOCK)
    s_pad = nb * MOBA_BLOCK
    pad = ((0, 0), (0, 0), (0, s_pad - S), (0, 0))
    q, k, v = jnp.pad(q, pad), jnp.pad(k, pad), jnp.pad(v, pad)
    scale = HEAD_DIM ** -0.5
    rel_bias = rel_bias.astype(jnp.float32)
    k_blocks = k.reshape(B, H, nb, MOBA_BLOCK, Dh)
    v_blocks = v.reshape(B, H, nb, MOBA_BLOCK, Dh)
    k_mean = jnp.mean(k_blocks.astype(jnp.float32), axis=3)
    gate = jnp.einsum('bhsd,bhnd->bhsn', q.astype(jnp.float32), k_mean)
    q_blk = jnp.arange(s_pad) // MOBA_BLOCK
    gate = jnp.where(jnp.arange(nb)[None, :] < q_blk[:, None], gate, NEG_INF)
    k_top = min(MOBA_TOP_K, nb)
    _, idx = lax.top_k(gate, k_top)
    b_ix = jnp.arange(B)[:, None, None, None]
    h_ix = jnp.arange(H)[None, :, None, None]
    n_sel = k_top * MOBA_BLOCK

    def attend_chunk(c):
        start = c * Q_CHUNK
        own = start // MOBA_BLOCK
        q_c = lax.dynamic_slice_in_dim(q, start, Q_CHUNK, axis=2)
        idx_c = lax.dynamic_slice_in_dim(idx, start, Q_CHUNK, axis=2)
        q_pos = start + jnp.arange(Q_CHUNK)
        k_sel = k_blocks[b_ix, h_ix, idx_c].reshape(B, H, Q_CHUNK, n_sel, Dh)
        v_sel = v_blocks[b_ix, h_ix, idx_c].reshape(B, H, Q_CHUNK, n_sel, Dh)
        sel_pos = (idx_c[..., None] * MOBA_BLOCK + jnp.arange(MOBA_BLOCK)).reshape(B, H, Q_CHUNK, n_sel)
        sel_bias = rel_bias[h_ix, t5_bucket(q_pos[:, None] - sel_pos)]
        valid = jnp.repeat(idx_c < own, MOBA_BLOCK, axis=-1)
        logit_sel = jnp.einsum('bhqd,bhqnd->bhqn', q_c, k_sel).astype(jnp.float32) * scale + sel_bias
        logit_sel = jnp.where(valid, logit_sel, NEG_INF)
        k_own = lax.dynamic_slice_in_dim(k, own * MOBA_BLOCK, MOBA_BLOCK, axis=2)
        v_own = lax.dynamic_slice_in_dim(v, own * MOBA_BLOCK, MOBA_BLOCK, axis=2)
        d_own = q_pos[:, None] - (own * MOBA_BLOCK + jnp.arange(MOBA_BLOCK))[None, :]
        own_bias = rel_bias[:, t5_bucket(d_own)]
        logit_own = jnp.einsum('bhqd,bhkd->bhqk', q_c, k_own).astype(jnp.float32) * scale + own_bias[None]
        logit_own = jnp.where((d_own >= 0)[None, None], logit_own, NEG_INF)
        probs = jax.nn.softmax(jnp.concatenate([logit_sel, logit_own], axis=-1), axis=-1).astype(v.dtype)
        return (jnp.einsum('bhqn,bhqnd->bhqd', probs[..., :n_sel], v_sel)
                + jnp.einsum('bhqk,bhkd->bhqd', probs[..., n_sel:], v_own))

    out = lax.map(attend_chunk, jnp.arange(s_pad // Q_CHUNK))
    out = jnp.moveaxis(out, 0, 2).reshape(B, H, s_pad, Dh)
    return out[:, :, :S]


def hier_moe(h, w_rg, b_rg, w_re, b_re, w1, w3, w2):
    B, S, D = h.shape
    xf = h.reshape(B * S, D)
    g_prob = jax.nn.softmax((xf @ w_rg).astype(jnp.float32) + b_rg.astype(jnp.float32), axis=-1)
    g_val, g_idx = lax.top_k(g_prob, 1)
    e_logits = ((xf @ w_re).astype(jnp.float32) + b_re.astype(jnp.float32)).reshape(-1, N_GROUPS, EXPERTS_PER_GROUP)
    e_sel = jnp.take_along_axis(e_logits, g_idx[:, :, None], axis=1)[:, 0]
    e_val, e_idx = lax.top_k(jax.nn.softmax(e_sel, axis=-1), EXPERT_TOP_K)
    e_val = e_val / jnp.sum(e_val, axis=-1, keepdims=True)
    w_in_group = jnp.einsum('tk,tke->te', e_val, jax.nn.one_hot(e_idx, EXPERTS_PER_GROUP, dtype=jnp.float32))
    combine = (jax.nn.one_hot(g_idx[:, 0], N_GROUPS, dtype=jnp.float32) * g_val)[:, :, None] * w_in_group[:, None, :]
    combine = combine.reshape(-1, N_EXPERTS).astype(xf.dtype)
    y = jnp.zeros_like(xf)
    for e in range(N_EXPERTS):
        hid = jax.nn.silu(xf @ w1[e]) * (xf @ w3[e])
        y = y + combine[:, e:e + 1] * (hid @ w2[e])
    return y.reshape(B, S, D)


def setup_inputs(seed: int = 0) -> dict:
    key = jax.random.key(seed)
    ks = jax.random.split(key, 32)
    f32 = jnp.float32

    def nrm(k, shape, scale):
        return jax.random.normal(k, shape, f32) * scale

    L, G, P, Hc = DEPTH, SSM_GROUPS, SSM_STATE, SSM_GROUP
    return {
        'x': nrm(ks[0], (BATCH, SEQ, D_MODEL), 1.0),
        'ln1_g': 1.0 + nrm(ks[1], (L, D_MODEL), 0.02),
        'w_in': nrm(ks[2], (L, D_MODEL, D_IN), D_MODEL ** -0.5),
        'b_gate': nrm(ks[3], (L, 2 * D_MODEL), 0.1),
        'ssm_lambda_re': -0.5 + nrm(ks[4], (L, G, P), 0.01),
        'ssm_lambda_im': math.pi * jnp.arange(P, dtype=f32) + nrm(ks[5], (L, G, P), 0.01),
        'ssm_log_step': jax.random.uniform(ks[6], (L, G), f32, math.log(STEP_MIN), math.log(STEP_MAX)),
        'ssm_b_re': nrm(ks[7], (L, G, P, Hc), (2 * Hc) ** -0.5),
        'ssm_b_im': nrm(ks[8], (L, G, P, Hc), (2 * Hc) ** -0.5),
        'ssm_c_re': nrm(ks[9], (L, G, Hc, P), P ** -0.5),
        'ssm_c_im': nrm(ks[10], (L, G, Hc, P), P ** -0.5),
        'ssm_d': nrm(ks[11], (L, G, Hc), 1.0),
        'w_glu': nrm(ks[12], (L, D_SSM, D_SSM), D_SSM ** -0.5),
        'b_glu': nrm(ks[13], (L, D_SSM), 0.02),
        'w_up_ssm': nrm(ks[14], (L, D_SSM, D_MODEL), D_SSM ** -0.5),
        'w_up_attn': nrm(ks[15], (L, D_ATTN, D_MODEL), D_ATTN ** -0.5),
        'rel_bias': nrm(ks[16], (N_HEADS, NUM_BUCKETS), 0.5),
        'w_out': nrm(ks[17], (L, D_MODEL, D_MODEL), D_MODEL ** -0.5),
        'ln2_g': 1.0 + nrm(ks[18], (L, D_MODEL), 0.02),
        'w_router_group': nrm(ks[19], (L, D_MODEL, N_GROUPS), D_MODEL ** -0.5),
        'b_router_group': nrm(ks[20], (L, N_GROUPS), 0.01),
        'w_router_expert': nrm(ks[21], (L, D_MODEL, N_EXPERTS), D_MODEL ** -0.5),
        'b_router_expert': nrm(ks[22], (L, N_EXPERTS), 0.01),
        'w1': nrm(ks[23], (L, N_EXPERTS, D_MODEL, D_EXPERT), D_MODEL ** -0.5),
        'w3': nrm(ks[24], (L, N_EXPERTS, D_MODEL, D_EXPERT), D_MODEL ** -0.5),
        'w2': nrm(ks[25], (L, N_EXPERTS, D_EXPERT, D_MODEL), D_EXPERT ** -0.5),
        'ln_f_g': 1.0 + nrm(ks[26], (D_MODEL,), 0.02),
    }


def reference(x, ln1_g, w_in, b_gate, ssm_lambda_re, ssm_lambda_im, ssm_log_step,
              ssm_b_re, ssm_b_im, ssm_c_re, ssm_c_im, ssm_d, w_glu, b_glu,
              w_up_ssm, w_up_attn, rel_bias, w_out, ln2_g, w_router_group,
              b_router_group, w_router_expert, b_router_expert, w1, w3, w2, ln_f_g):
    B, S, D = x.shape
    for l in range(DEPTH):
        h = rms_norm(x, ln1_g[l])
        proj = h @ w_in[l]
        o1 = D_SSM
        o2 = o1 + D_ATTN
        o3 = o2 + D_ATTN
        o4 = o3 + D_ATTN
        u = proj[..., :o1].reshape(B, S, SSM_GROUPS, SSM_GROUP)
        q = proj[..., o1:o2].reshape(B, S, N_HEADS, HEAD_DIM).transpose(0, 2, 1, 3)
        k = proj[..., o2:o3].reshape(B, S, N_HEADS, HEAD_DIM).transpose(0, 2, 1, 3)
        v = proj[..., o3:o4].reshape(B, S, N_HEADS, HEAD_DIM).transpose(0, 2, 1, 3)
        gates = proj[..., o4:] + b_gate[l]
        gate_ssm, gate_attn = gates[..., :D_MODEL], gates[..., D_MODEL:]
        y_ssm = s5_mixer(u, ssm_lambda_re[l], ssm_lambda_im[l], ssm_log_step[l],
                         ssm_b_re[l], ssm_b_im[l], ssm_c_re[l], ssm_c_im[l], ssm_d[l]).reshape(B, S, D_SSM)
        g = jax.nn.gelu(y_ssm)
        y_ssm = g * jax.nn.sigmoid(g @ w_glu[l] + b_glu[l])
        y_attn = moba_attention(q, k, v, rel_bias).transpose(0, 2, 1, 3).reshape(B, S, D_ATTN)
        merged = (jax.nn.sigmoid(gate_ssm) * (y_ssm @ w_up_ssm[l])
                  + jax.nn.sigmoid(gate_attn) * (y_attn @ w_up_attn[l]))
        x = x + merged @ w_out[l]
        x = x + hier_moe(rms_norm(x, ln2_g[l]), w_router_group[l], b_router_group[l],
                         w_router_expert[l], b_router_expert[l], w1[l], w3[l], w2[l])
    return rms_norm(x, ln_f_g)
```

```python
import functools
import math

import jax
import jax.numpy as jnp
from jax import lax
from jax.experimental import pallas as pl
from jax.experimental.pallas import tpu as pltpu

F32 = jnp.float32
BF16 = jnp.bfloat16

SSM_GROUP = 16
SSM_STATE = 64
HEAD_DIM = 64
MOBA_BLOCK = 256
MOBA_TOP_K = 3
NUM_BUCKETS = 32
MAX_DISTANCE = 128
N_GROUPS = 4
EXPERTS_PER_GROUP = 8
N_EXPERTS = N_GROUPS * EXPERTS_PER_GROUP
RMS_EPS = 1e-6
NEG_INF = -1e30

LANES = 128
ROUTER_LANES = LANES
EXPERT_LANE0 = N_GROUPS
VMEM_LIMIT = 56 * 1024 * 1024


def _sigmoid(x):
    return 1.0 / (1.0 + jnp.exp(-x))


def _rms(x, g):
    ms = jnp.mean(x * x, axis=-1, keepdims=True)
    return x * lax.rsqrt(ms + RMS_EPS) * g


def _const_spec(shape):
    n = len(shape)
    return pl.BlockSpec(shape, lambda *_: (0,) * n)


def _inproj_kernel(x_ref, g_ref, w_ref, wqT_ref, wvT_ref, bg_ref,
                   u_ref, k_ref, kmean_ref, qT_ref, vT_ref, gates_ref, *, nblk, d_ssm, d_attn):
    hn = _rms(x_ref[...], g_ref[...]).astype(BF16)
    acc = jnp.dot(hn, w_ref[...], preferred_element_type=F32)
    u_ref[...] = acc[:, :d_ssm]
    kf = acc[:, d_ssm:d_ssm + d_attn].reshape(nblk, MOBA_BLOCK, d_attn)
    k_ref[...] = kf.astype(BF16)
    kmean_ref[...] = jnp.mean(kf, axis=1, keepdims=True)
    gates_ref[...] = (acc[:, d_ssm + d_attn:] + bg_ref[...]).astype(BF16)
    nt = (((1,), (1,)), ((), ()))
    qT = lax.dot_general(wqT_ref[...], hn, nt, preferred_element_type=F32)
    qT_ref[0] = qT.astype(BF16)
    vT = lax.dot_general(wvT_ref[...], hn, nt, preferred_element_type=F32)
    for j in range(nblk):
        vT_ref[j] = vT[:, j * MOBA_BLOCK:(j + 1) * MOBA_BLOCK].astype(BF16)


def _inproj(x2, g1, w_main, wqT, wvT, bg, *, batch, seq, tm):
    t, d = x2.shape
    d_attn = wqT.shape[0]
    d_ssm = w_main.shape[1] - d_attn - bg.shape[1]
    nblk = tm // MOBA_BLOCK
    tiles_per_seq = seq // tm
    nb_total = t // MOBA_BLOCK
    kern = functools.partial(_inproj_kernel, nblk=nblk, d_ssm=d_ssm, d_attn=d_attn)
    return pl.pallas_call(
        kern,
        grid=(t // tm,),
        in_specs=[
            pl.BlockSpec((tm, d), lambda i: (i, 0)),
            _const_spec(g1.shape),
            _const_spec(w_main.shape),
            _const_spec(wqT.shape),
            _const_spec(wvT.shape),
            _const_spec(bg.shape),
        ],
        out_specs=[
            pl.BlockSpec((tm, d_ssm), lambda i: (i, 0)),
            pl.BlockSpec((nblk, MOBA_BLOCK, d_attn), lambda i: (i, 0, 0)),
            pl.BlockSpec((nblk, 1, d_attn), lambda i: (i, 0, 0)),
            pl.BlockSpec((1, d_attn, tm), lambda i: (i // tiles_per_seq, 0, i % tiles_per_seq)),
            pl.BlockSpec((nblk, d_attn, MOBA_BLOCK), lambda i: (i, 0, 0)),
            pl.BlockSpec((tm, bg.shape[1]), lambda i: (i, 0)),
        ],
        out_shape=[
            jax.ShapeDtypeStruct((t, d_ssm), F32),
            jax.ShapeDtypeStruct((nb_total, MOBA_BLOCK, d_attn), BF16),
            jax.ShapeDtypeStruct((nb_total, 1, d_attn), F32),
            jax.ShapeDtypeStruct((batch, d_attn, seq), BF16),
            jax.ShapeDtypeStruct((nb_total, d_attn, MOBA_BLOCK), BF16),
            jax.ShapeDtypeStruct((t, bg.shape[1]), BF16),
        ],
        compiler_params=pltpu.CompilerParams(
            dimension_semantics=("arbitrary",), vmem_limit_bytes=VMEM_LIMIT),
    )(x2, g1, w_main, wqT, wvT, bg)


def _s5_kernel(u_ref, bb_ref, cc_ref, ar_ref, ai_ref, d_ref, wglu_ref, bglu_ref,
               y_ref, utb_ref, sbuf_ref, state_ref, ytb_ref, *, batch, ts, n_state, chunk):
    d_ssm = u_ref.shape[2]
    n_slab = d_ssm // LANES

    @pl.when(pl.program_id(0) == 0)
    def _():
        state_ref[...] = jnp.zeros_like(state_ref)

    for b in range(batch):
        ub = u_ref[b]
        for j in range(n_slab):
            utb_ref[j, pl.ds(b, ts, stride=batch), :] = ub[:, j * LANES:(j + 1) * LANES]
    u_tb = jnp.concatenate([utb_ref[j] for j in range(n_slab)], axis=1)
    sbuf_ref[...] = jnp.dot(u_tb.astype(BF16), bb_ref[...], preferred_element_type=F32)

    for c in range(n_state // chunk):
        re = slice(c * chunk, (c + 1) * chunk)
        im = slice(n_state + c * chunk, n_state + (c + 1) * chunk)
        ar = jnp.broadcast_to(ar_ref[:, re], (batch, chunk))
        ai = jnp.broadcast_to(ai_ref[:, re], (batch, chunk))

        def step(t, carry, re=re, im=im, ar=ar, ai=ai):
            hr, hi = carry
            r0 = pl.multiple_of(t * batch, batch)
            nhr = ar * hr - ai * hi + sbuf_ref[pl.ds(r0, batch), re]
            nhi = ar * hi + ai * hr + sbuf_ref[pl.ds(r0, batch), im]
            sbuf_ref[pl.ds(r0, batch), re] = nhr
            sbuf_ref[pl.ds(r0, batch), im] = nhi
            return nhr, nhi

        hr, hi = lax.fori_loop(0, ts, step, (state_ref[:, re], state_ref[:, im]), unroll=8)
        state_ref[:, re] = hr
        state_ref[:, im] = hi

    y = jnp.dot(sbuf_ref[...].astype(BF16), cc_ref[...], preferred_element_type=F32)
    y = y + d_ref[...] * u_tb
    g = jax.nn.gelu(y)
    z = jnp.dot(g.astype(BF16), wglu_ref[...], preferred_element_type=F32) + bglu_ref[...]
    out = g * _sigmoid(z)
    for j in range(n_slab):
        ytb_ref[j] = out[:, j * LANES:(j + 1) * LANES]
    for b in range(batch):
        y_ref[b] = jnp.concatenate(
            [ytb_ref[j, pl.ds(b, ts, stride=batch), :] for j in range(n_slab)], axis=1).astype(BF16)


def _s5(u3, bb, cc, ar, ai, dvec, wglu, bglu, *, ts, chunk):
    batch, seq, d_ssm = u3.shape
    n_state = ar.shape[1]
    kern = functools.partial(_s5_kernel, batch=batch, ts=ts, n_state=n_state, chunk=chunk)
    return pl.pallas_call(
        kern,
        grid=(seq // ts,),
        in_specs=[
            pl.BlockSpec((batch, ts, d_ssm), lambda i: (0, i, 0)),
            _const_spec(bb.shape), _const_spec(cc.shape), _const_spec(ar.shape), _const_spec(ai.shape),
            _const_spec(dvec.shape), _const_spec(wglu.shape), _const_spec(bglu.shape),
        ],
        out_specs=pl.BlockSpec((batch, ts, d_ssm), lambda i: (0, i, 0)),
        out_shape=jax.ShapeDtypeStruct((batch, seq, d_ssm), BF16),
        scratch_shapes=[
            pltpu.VMEM((d_ssm // LANES, ts * batch, LANES), F32),
            pltpu.VMEM((ts * batch, 2 * n_state), F32),
            pltpu.VMEM((batch, 2 * n_state), F32),
            pltpu.VMEM((d_ssm // LANES, ts * batch, LANES), F32),
        ],
        compiler_params=pltpu.CompilerParams(
            dimension_semantics=("arbitrary",), vmem_limit_bytes=VMEM_LIMIT),
    )(u3, bb, cc, ar, ai, dvec, wglu, bglu)


def _moba_kernel(farb_ref, qT_ref, k_ref, vT_ref, kmean_ref, bias_ref, o_ref, mask_ref, *, nb):
    hp = pl.program_id(1)
    i = pl.program_id(2)
    tq = qT_ref.shape[2]
    qT = qT_ref[0]
    row = lax.broadcasted_iota(jnp.int32, qT.shape, 0)
    km = kmean_ref[0]
    km_hi = km.astype(BF16)
    km_lo = (km - km_hi.astype(F32)).astype(BF16)
    blk = lax.broadcasted_iota(jnp.int32, (nb, tq), 0)
    prev = jnp.maximum(i - 1, 0)
    outs = []
    for j in range(2):
        qTh = jnp.where((row >= j * HEAD_DIM) & (row < (j + 1) * HEAD_DIM), qT, jnp.zeros_like(qT))
        gate = (jnp.dot(km_hi, qTh, preferred_element_type=F32)
                + jnp.dot(km_lo, qTh, preferred_element_type=F32))
        gate = jnp.where(blk < i, gate, NEG_INF)
        rank = jnp.zeros((nb, tq), jnp.int32)
        for m in range(nb):
            gm = gate[m:m + 1, :]
            beats = (gm > gate) | ((gm == gate) & (blk > m))
            rank = rank + beats.astype(jnp.int32)
        sel = (rank < MOBA_TOP_K) & (blk < i)
        far = jnp.where(blk == i - 1, 0.0, farb_ref[2 * hp + j])
        mask_ref[j] = jnp.where(sel, far, NEG_INF)

        vrows = slice(j * HEAD_DIM, (j + 1) * HEAD_DIM)
        s = jnp.dot(k_ref[i], qTh, preferred_element_type=F32) + bias_ref[0, j, 0]
        m_run = jnp.max(s, axis=0, keepdims=True)
        p = jnp.exp(s - m_run)
        l_run = jnp.sum(p, axis=0, keepdims=True)
        acc = jnp.dot(vT_ref[i][vrows, :], p.astype(BF16), preferred_element_type=F32)

        def update(carry, s, n, vrows=vrows):
            m_run, l_run, acc = carry
            m_new = jnp.maximum(m_run, jnp.max(s, axis=0, keepdims=True))
            alpha = jnp.exp(m_run - m_new)
            p = jnp.exp(s - m_new)
            l_new = alpha * l_run + jnp.sum(p, axis=0, keepdims=True)
            acc_new = alpha * acc + jnp.dot(vT_ref[n][vrows, :], p.astype(BF16),
                                            preferred_element_type=F32)
            return m_new, l_new, acc_new

        s = (jnp.dot(k_ref[prev], qTh, preferred_element_type=F32) + bias_ref[0, j, 1]
             + mask_ref[j, pl.ds(prev, 1), :])
        carry = update((m_run, l_run, acc), s, prev)

        def far_step(n, carry, j=j, qTh=qTh, update=update):
            s = jnp.dot(k_ref[n], qTh, preferred_element_type=F32) + mask_ref[j, pl.ds(n, 1), :]
            return update(carry, s, n)

        m_run, l_run, acc = lax.fori_loop(0, i - 1, far_step, carry)
        outs.append(acc / l_run)
    o_ref[0] = jnp.concatenate(outs, axis=0).T.astype(BF16)


def _moba(farb, qT, k3, vT3, kmean, bias, *, batch, seq):
    nb = seq // MOBA_BLOCK
    d_attn = qT.shape[1]
    hw = 2 * HEAD_DIM
    n_hp = d_attn // hw
    kern = functools.partial(_moba_kernel, nb=nb)
    return pl.pallas_call(
        kern,
        grid=(batch, n_hp, nb),
        in_specs=[
            pl.BlockSpec(memory_space=pltpu.SMEM),
            pl.BlockSpec((1, hw, MOBA_BLOCK), lambda b, h, i: (b, h, i)),
            pl.BlockSpec((nb, MOBA_BLOCK, hw), lambda b, h, i: (b, 0, h)),
            pl.BlockSpec((nb, hw, MOBA_BLOCK), lambda b, h, i: (b, h, 0)),
            pl.BlockSpec((1, nb, hw), lambda b, h, i: (b, 0, h)),
            pl.BlockSpec((1, 2, 2, MOBA_BLOCK, MOBA_BLOCK), lambda b, h, i: (h, 0, 0, 0, 0)),
        ],
        out_specs=pl.BlockSpec((1, MOBA_BLOCK, hw), lambda b, h, i: (b, i, h)),
        out_shape=jax.ShapeDtypeStruct((batch, seq, d_attn), BF16),
        scratch_shapes=[pltpu.VMEM((2, nb, MOBA_BLOCK), F32)],
        compiler_params=pltpu.CompilerParams(
            dimension_semantics=("arbitrary", "arbitrary", "arbitrary"),
            vmem_limit_bytes=VMEM_LIMIT),
    )(farb, qT, k3, vT3, kmean, bias)


def _route(logits):
    lane = lax.broadcasted_iota(jnp.int32, logits.shape, 1)
    lane_f = lane.astype(F32)
    big = float(ROUTER_LANES)
    ninf = -jnp.inf
    gmask = lane < N_GROUPS
    gmax = jnp.max(jnp.where(gmask, logits, ninf), axis=1, keepdims=True)
    gsum = jnp.sum(jnp.where(gmask, jnp.exp(logits - gmax), 0.0), axis=1, keepdims=True)
    g_val = 1.0 / gsum
    g_idx = jnp.min(jnp.where(gmask & (logits == gmax), lane_f, big), axis=1, keepdims=True)
    e_grp = ((lane - EXPERT_LANE0) >> 3).astype(F32)
    emask = (lane >= EXPERT_LANE0) & (lane < EXPERT_LANE0 + N_EXPERTS) & (e_grp == g_idx)
    el = jnp.where(emask, logits, ninf)
    e1 = jnp.max(el, axis=1, keepdims=True)
    i1 = jnp.min(jnp.where(emask & (el == e1), lane_f, big), axis=1, keepdims=True)
    el2 = jnp.where(lane_f == i1, ninf, el)
    e2 = jnp.max(el2, axis=1, keepdims=True)
    i2 = jnp.min(jnp.where(emask & (lane_f != i1) & (el2 == e2), lane_f, big), axis=1, keepdims=True)
    tt = jnp.exp(e2 - e1)
    w1 = 1.0 / (1.0 + tt)
    w2 = tt / (1.0 + tt)
    return g_val * (jnp.where(lane_f == i1, w1, 0.0) + jnp.where(lane_f == i2, w2, 0.0))


def _merge_kernel(x_ref, ys_ref, ya_ref, gates_ref, wus_ref, wua_ref, wout_ref, g2_ref, wr_ref, br_ref,
                  x1_ref, xn_ref, comb_ref):
    d = x_ref.shape[1]
    a = jnp.dot(ys_ref[...], wus_ref[...], preferred_element_type=F32)
    b = jnp.dot(ya_ref[...], wua_ref[...], preferred_element_type=F32)
    gts = gates_ref[...].astype(F32)
    merged = _sigmoid(gts[:, :d]) * a + _sigmoid(gts[:, d:]) * b
    x1 = x_ref[...] + jnp.dot(merged.astype(BF16), wout_ref[...], preferred_element_type=F32)
    x1_ref[...] = x1
    xn = _rms(x1, g2_ref[...]).astype(BF16)
    xn_ref[...] = xn
    logits = jnp.dot(xn, wr_ref[...], preferred_element_type=F32) + br_ref[...]
    comb_ref[...] = _route(logits)


def _merge(x2, ys, ya, gates, wus, wua, wout, g2, wr, br, *, tm):
    t, d = x2.shape
    return pl.pallas_call(
        _merge_kernel,
        grid=(t // tm,),
        in_specs=[
            pl.BlockSpec((tm, d), lambda i: (i, 0)),
            pl.BlockSpec((tm, ys.shape[1]), lambda i: (i, 0)),
            pl.BlockSpec((tm, ya.shape[1]), lambda i: (i, 0)),
            pl.BlockSpec((tm, gates.shape[1]), lambda i: (i, 0)),
            _const_spec(wus.shape), _const_spec(wua.shape), _const_spec(wout.shape),
            _const_spec(g2.shape), _const_spec(wr.shape), _const_spec(br.shape),
        ],
        out_specs=[
            pl.BlockSpec((tm, d), lambda i: (i, 0)),
            pl.BlockSpec((tm, d), lambda i: (i, 0)),
            pl.BlockSpec((tm, ROUTER_LANES), lambda i: (i, 0)),
        ],
        out_shape=[
            jax.ShapeDtypeStruct((t, d), F32),
            jax.ShapeDtypeStruct((t, d), BF16),
            jax.ShapeDtypeStruct((t, ROUTER_LANES), F32),
        ],
        compiler_params=pltpu.CompilerParams(
            dimension_semantics=("arbitrary",), vmem_limit_bytes=VMEM_LIMIT),
    )(x2, ys, ya, gates, wus, wua, wout, g2, wr, br)


def _moe_kernel(xn_ref, comb_ref, x1_ref, w1_ref, w3_ref, w2_ref, gf_ref, y_ref, acc_ref):
    e = pl.program_id(1)

    @pl.when(e == 0)
    def _():
        acc_ref[...] = jnp.zeros_like(acc_ref)

    xt = xn_ref[...]
    h1 = jnp.dot(xt, w1_ref[0], preferred_element_type=F32)
    h3 = jnp.dot(xt, w3_ref[0], preferred_element_type=F32)
    hid = (h1 * _sigmoid(h1) * h3).astype(BF16)
    o = jnp.dot(hid, w2_ref[0], preferred_element_type=F32)
    comb = comb_ref[...]
    lane = lax.broadcasted_iota(jnp.int32, comb.shape, 1)
    c = jnp.sum(jnp.where(lane == e + EXPERT_LANE0, comb, 0.0), axis=1, keepdims=True)
    acc_ref[...] += c * o

    @pl.when(e == pl.num_programs(1) - 1)
    def _():
        y_ref[...] = _rms(x1_ref[...] + acc_ref[...], gf_ref[...])


def _moe(xn, comb, x1, w1, w3, w2, gf, *, tm):
    t, d = x1.shape
    ne, _, de = w1.shape
    return pl.pallas_call(
        _moe_kernel,
        grid=(t // tm, ne),
        in_specs=[
            pl.BlockSpec((tm, d), lambda i, e: (i, 0)),
            pl.BlockSpec((tm, ROUTER_LANES), lambda i, e: (i, 0)),
            pl.BlockSpec((tm, d), lambda i, e: (i, 0)),
            pl.BlockSpec((1, d, de), lambda i, e: (e, 0, 0)),
            pl.BlockSpec((1, d, de), lambda i, e: (e, 0, 0)),
            pl.BlockSpec((1, de, d), lambda i, e: (e, 0, 0)),
            _const_spec(gf.shape),
        ],
        out_specs=pl.BlockSpec((tm, d), lambda i, e: (i, 0)),
        out_shape=jax.ShapeDtypeStruct((t, d), F32),
        scratch_shapes=[pltpu.VMEM((tm, d), F32)],
        compiler_params=pltpu.CompilerParams(
            dimension_semantics=("arbitrary", "arbitrary"), vmem_limit_bytes=VMEM_LIMIT),
    )(xn, comb, x1, w1, w3, w2, gf)


def _t5_bucket(dist):
    n = jnp.maximum(dist, 0)
    max_exact = NUM_BUCKETS // 2
    large = max_exact + (jnp.log(jnp.maximum(n, max_exact).astype(F32) / max_exact)
                         / math.log(MAX_DISTANCE / max_exact)
                         * (NUM_BUCKETS - max_exact)).astype(jnp.int32)
    return jnp.where(n < max_exact, n, jnp.minimum(large, NUM_BUCKETS - 1))


def _bias_tables(rel_bias):
    kk = jnp.arange(MOBA_BLOCK)[:, None]
    qq = jnp.arange(MOBA_BLOCK)[None, :]
    rb = rel_bias.astype(F32)
    d_own = qq - kk
    own = jnp.where((d_own >= 0)[None], rb[:, _t5_bucket(d_own)], NEG_INF)
    prv = rb[:, _t5_bucket(d_own + MOBA_BLOCK)]
    tab = jnp.stack([own, prv], axis=1)
    far = rb[:, _t5_bucket(jnp.asarray(2 * MOBA_BLOCK))]
    return tab.reshape(rb.shape[0] // 2, 2, 2, MOBA_BLOCK, MOBA_BLOCK), far


def _ssm_params(lam_re, lam_im, log_step, b_re, b_im, c_re, c_im):
    g, p = lam_re.shape
    step = jnp.exp(log_step)[:, None]
    decay = jnp.exp(lam_re * step)
    a_re = decay * jnp.cos(lam_im * step)
    a_im = decay * jnp.sin(lam_im * step)
    denom = lam_re * lam_re + lam_im * lam_im
    nr, ni = a_re - 1.0, a_im
    coef_re = (nr * lam_re + ni * lam_im) / denom
    coef_im = (ni * lam_re - nr * lam_im) / denom
    bb_re = coef_re[..., None] * b_re - coef_im[..., None] * b_im
    bb_im = coef_re[..., None] * b_im + coef_im[..., None] * b_re
    eye = jnp.eye(g, dtype=F32)
    hc = b_re.shape[2]

    def in_mat(m):
        return jnp.einsum('gk,gph->ghkp', eye, m).reshape(g * hc, g * p)

    def out_mat(m):
        return jnp.einsum('gk,ghp->gpkh', eye, m).reshape(g * p, g * hc)

    bb = jnp.concatenate([in_mat(bb_re), in_mat(bb_im)], axis=1).astype(BF16)
    cc = jnp.concatenate([out_mat(c_re), -out_mat(c_im)], axis=0).astype(BF16)
    return bb, cc, a_re.reshape(1, g * p), a_im.reshape(1, g * p)


def kernel(x, ln1_g, w_in, b_gate, ssm_lambda_re, ssm_lambda_im, ssm_log_step, ssm_b_re, ssm_b_im,
           ssm_c_re, ssm_c_im, ssm_d, w_glu, b_glu, w_up_ssm, w_up_attn, rel_bias, w_out, ln2_g,
           w_router_group, b_router_group, w_router_expert, b_router_expert, w1, w3, w2, ln_f_g):
    batch, seq, d = x.shape
    t = batch * seq
    d_ssm = w_glu.shape[1]
    d_attn = w_up_attn.shape[1]
    o1, o2, o3, o4 = d_ssm, d_ssm + d_attn, d_ssm + 2 * d_attn, d_ssm + 3 * d_attn
    x2 = x.reshape(t, d)
    xcur = x2
    for l in range(w_in.shape[0]):
        wl = w_in[l]
        w_main = jnp.concatenate([wl[:, :o1], wl[:, o2:o3], wl[:, o4:]], axis=1).astype(BF16)
        wqT = (wl[:, o1:o2] * (HEAD_DIM ** -0.5)).T.astype(BF16)
        wvT = wl[:, o3:o4].T.astype(BF16)
        u, k3, kmean, qT, vT3, gates = _inproj(
            xcur, ln1_g[l][None], w_main, wqT, wvT, b_gate[l][None], batch=batch, seq=seq, tm=512)

        bb, cc, ar, ai = _ssm_params(ssm_lambda_re[l], ssm_lambda_im[l], ssm_log_step[l],
                                     ssm_b_re[l], ssm_b_im[l], ssm_c_re[l], ssm_c_im[l])
        y_ssm = _s5(u.reshape(batch, seq, d_ssm), bb, cc, ar, ai, ssm_d[l].reshape(1, d_ssm),
                    w_glu[l].astype(BF16), b_glu[l][None], ts=128, chunk=512)

        bias, far = _bias_tables(rel_bias)
        y_attn = _moba(far, qT, k3, vT3, kmean.reshape(batch, seq // MOBA_BLOCK, d_attn), bias,
                       batch=batch, seq=seq)

        wr = jnp.zeros((d, ROUTER_LANES), F32)
        wr = wr.at[:, :N_GROUPS].set(w_router_group[l])
        wr = wr.at[:, EXPERT_LANE0:EXPERT_LANE0 + N_EXPERTS].set(w_router_expert[l]).astype(BF16)
        br = jnp.zeros((1, ROUTER_LANES), F32)
        br = br.at[0, :N_GROUPS].set(b_router_group[l])
        br = br.at[0, EXPERT_LANE0:EXPERT_LANE0 + N_EXPERTS].set(b_router_expert[l])
        x1, xn, comb = _merge(
            xcur, y_ssm.reshape(t, d_ssm), y_attn.reshape(t, d_attn), gates,
            w_up_ssm[l].astype(BF16), w_up_attn[l].astype(BF16), w_out[l].astype(BF16),
            ln2_g[l][None], wr, br, tm=512)

        last = l == w_in.shape[0] - 1
        gf = ln_f_g[None] if last else jnp.ones((1, d), F32)
        xcur = _moe(xn, comb, x1, w1[l].astype(BF16), w3[l].astype(BF16), w2[l].astype(BF16), gf, tm=1024)
    return xcur.reshape(batch, seq, d)
```

```python
import functools
import math

import jax
import jax.numpy as jnp
from jax import lax
from jax.experimental import pallas as pl
from jax.experimental.pallas import tpu as pltpu

F32 = jnp.float32
BF16 = jnp.bfloat16

SSM_GROUP = 16
SSM_STATE = 64
HEAD_DIM = 64
MOBA_BLOCK = 256
MOBA_TOP_K = 3
NUM_BUCKETS = 32
MAX_DISTANCE = 128
N_GROUPS = 4
EXPERTS_PER_GROUP = 8
N_EXPERTS = N_GROUPS * EXPERTS_PER_GROUP
PAIRS_PER_GROUP = EXPERTS_PER_GROUP * (EXPERTS_PER_GROUP - 1) // 2
N_CLASSES = N_GROUPS * PAIRS_PER_GROUP
MOE_ROWS = 256
RMS_EPS = 1e-6
NEG_INF = -1e30
LOG2E = 1.4426950408889634

LANES = 128
BF16_ROWS = 16
V_ROWS = HEAD_DIM + BF16_ROWS
ROUTER_LANES = LANES
EXPERT_LANE0 = N_GROUPS
VMEM_LIMIT = 56 * 1024 * 1024


def _sigmoid(x):
    return 1.0 / (1.0 + jnp.exp(-x))


def _rms(x, g):
    ms = jnp.mean(x * x, axis=-1, keepdims=True)
    return x * lax.rsqrt(ms + RMS_EPS) * g


def _const_spec(shape):
    n = len(shape)
    return pl.BlockSpec(shape, lambda *_: (0,) * n)


def _inproj_kernel(x_ref, g_ref, w_ref, wqT_ref, wvT_ref, vb_ref, bg_ref,
                   u_ref, k_ref, kmean_ref, qT_ref, vT_ref, gates_ref, *, nblk, d_ssm, d_attn):
    hn = _rms(x_ref[...], g_ref[...]).astype(BF16)
    acc = jnp.dot(hn, w_ref[...], preferred_element_type=F32)
    u_ref[...] = acc[:, :d_ssm]
    kf = acc[:, d_ssm:d_ssm + d_attn].reshape(nblk, MOBA_BLOCK, d_attn)
    k_ref[...] = kf.astype(BF16)
    kmean_ref[...] = jnp.mean(kf, axis=1, keepdims=True)
    gates_ref[...] = (acc[:, d_ssm + d_attn:] + bg_ref[...]).astype(BF16)
    nt = (((1,), (1,)), ((), ()))
    qT = lax.dot_general(wqT_ref[...], hn, nt, preferred_element_type=F32)
    qT_ref[0] = qT.astype(BF16)
    vT = lax.dot_general(wvT_ref[...], hn, nt, preferred_element_type=F32) + vb_ref[...]
    for j in range(nblk):
        vT_ref[j] = vT[:, j * MOBA_BLOCK:(j + 1) * MOBA_BLOCK].astype(BF16)


def _inproj(x2, g1, w_main, wqT, wvT, vb, bg, *, batch, seq, tm):
    t, d = x2.shape
    d_attn = wqT.shape[0]
    v_rows = wvT.shape[0]
    d_ssm = w_main.shape[1] - d_attn - bg.shape[1]
    nblk = tm // MOBA_BLOCK
    tiles_per_seq = seq // tm
    nb_total = t // MOBA_BLOCK
    kern = functools.partial(_inproj_kernel, nblk=nblk, d_ssm=d_ssm, d_attn=d_attn)
    return pl.pallas_call(
        kern,
        grid=(t // tm,),
        in_specs=[
            pl.BlockSpec((tm, d), lambda i: (i, 0)),
            _const_spec(g1.shape),
            _const_spec(w_main.shape),
            _const_spec(wqT.shape),
            _const_spec(wvT.shape),
            _const_spec(vb.shape),
            _const_spec(bg.shape),
        ],
        out_specs=[
            pl.BlockSpec((tm, d_ssm), lambda i: (i, 0)),
            pl.BlockSpec((nblk, MOBA_BLOCK, d_attn), lambda i: (i, 0, 0)),
            pl.BlockSpec((nblk, 1, d_attn), lambda i: (i, 0, 0)),
            pl.BlockSpec((1, d_attn, tm), lambda i: (i // tiles_per_seq, 0, i % tiles_per_seq)),
            pl.BlockSpec((nblk, v_rows, MOBA_BLOCK), lambda i: (i, 0, 0)),
            pl.BlockSpec((tm, bg.shape[1]), lambda i: (i, 0)),
        ],
        out_shape=[
            jax.ShapeDtypeStruct((t, d_ssm), F32),
            jax.ShapeDtypeStruct((nb_total, MOBA_BLOCK, d_attn), BF16),
            jax.ShapeDtypeStruct((nb_total, 1, d_attn), F32),
            jax.ShapeDtypeStruct((batch, d_attn, seq), BF16),
            jax.ShapeDtypeStruct((nb_total, v_rows, MOBA_BLOCK), BF16),
            jax.ShapeDtypeStruct((t, bg.shape[1]), BF16),
        ],
        compiler_params=pltpu.CompilerParams(
            dimension_semantics=("arbitrary",), vmem_limit_bytes=VMEM_LIMIT),
    )(x2, g1, w_main, wqT, wvT, vb, bg)


def _s5_kernel(u_ref, bb_ref, cc_ref, ar_ref, ai_ref, d_ref, wglu_ref, bglu_ref,
               y_ref, utb_ref, sbuf_ref, state_ref, ytb_ref, *, batch, ts, n_state, chunk):
    d_ssm = u_ref.shape[2]
    n_slab = d_ssm // LANES

    @pl.when(pl.program_id(0) == 0)
    def _():
        state_ref[...] = jnp.zeros_like(state_ref)

    for b in range(batch):
        ub = u_ref[b]
        for j in range(n_slab):
            utb_ref[j, pl.ds(b, ts, stride=batch), :] = ub[:, j * LANES:(j + 1) * LANES]
    u_tb = jnp.concatenate([utb_ref[j] for j in range(n_slab)], axis=1)
    sbuf_ref[...] = jnp.dot(u_tb.astype(BF16), bb_ref[...], preferred_element_type=F32)

    for c in range(n_state // chunk):
        re = slice(c * chunk, (c + 1) * chunk)
        im = slice(n_state + c * chunk, n_state + (c + 1) * chunk)
        ar = jnp.broadcast_to(ar_ref[:, re], (batch, chunk))
        ai = jnp.broadcast_to(ai_ref[:, re], (batch, chunk))

        def step(t, carry, re=re, im=im, ar=ar, ai=ai):
            hr, hi = carry
            r0 = pl.multiple_of(t * batch, batch)
            nhr = ar * hr - ai * hi + sbuf_ref[pl.ds(r0, batch), re]
            nhi = ar * hi + ai * hr + sbuf_ref[pl.ds(r0, batch), im]
            sbuf_ref[pl.ds(r0, batch), re] = nhr
            sbuf_ref[pl.ds(r0, batch), im] = nhi
            return nhr, nhi

        hr, hi = lax.fori_loop(0, ts, step, (state_ref[:, re], state_ref[:, im]), unroll=8)
        state_ref[:, re] = hr
        state_ref[:, im] = hi

    y = jnp.dot(sbuf_ref[...].astype(BF16), cc_ref[...], preferred_element_type=F32)
    y = y + d_ref[...] * u_tb
    g = jax.nn.gelu(y)
    z = jnp.dot(g.astype(BF16), wglu_ref[...], preferred_element_type=F32) + bglu_ref[...]
    out = g * _sigmoid(z)
    for j in range(n_slab):
        ytb_ref[j] = out[:, j * LANES:(j + 1) * LANES]
    for b in range(batch):
        y_ref[b] = jnp.concatenate(
            [ytb_ref[j, pl.ds(b, ts, stride=batch), :] for j in range(n_slab)], axis=1).astype(BF16)


def _s5(u3, bb, cc, ar, ai, dvec, wglu, bglu, *, ts, chunk):
    batch, seq, d_ssm = u3.shape
    n_state = ar.shape[1]
    kern = functools.partial(_s5_kernel, batch=batch, ts=ts, n_state=n_state, chunk=chunk)
    return pl.pallas_call(
        kern,
        grid=(seq // ts,),
        in_specs=[
            pl.BlockSpec((batch, ts, d_ssm), lambda i: (0, i, 0)),
            _const_spec(bb.shape), _const_spec(cc.shape), _const_spec(ar.shape), _const_spec(ai.shape),
            _const_spec(dvec.shape), _const_spec(wglu.shape), _const_spec(bglu.shape),
        ],
        out_specs=pl.BlockSpec((batch, ts, d_ssm), lambda i: (0, i, 0)),
        out_shape=jax.ShapeDtypeStruct((batch, seq, d_ssm), BF16),
        scratch_shapes=[
            pltpu.VMEM((d_ssm // LANES, ts * batch, LANES), F32),
            pltpu.VMEM((ts * batch, 2 * n_state), F32),
            pltpu.VMEM((batch, 2 * n_state), F32),
            pltpu.VMEM((d_ssm // LANES, ts * batch, LANES), F32),
        ],
        compiler_params=pltpu.CompilerParams(
            dimension_semantics=("arbitrary",), vmem_limit_bytes=VMEM_LIMIT),
    )(u3, bb, cc, ar, ai, dvec, wglu, bglu)


def _select_blocks(gate, blk_f, n_pick):
    sel = jnp.zeros(gate.shape, jnp.bool_)
    for _ in range(n_pick):
        mx = jnp.max(gate, axis=0, keepdims=True)
        idx = jnp.min(jnp.where(gate == mx, blk_f, float(gate.shape[0])), axis=0, keepdims=True)
        hit = blk_f == idx
        sel = sel | hit
        gate = jnp.where(hit, -jnp.inf, gate)
    return sel


def _moba_kernel(farb_ref, qT_ref, k_ref, vT_ref, kmean_ref, bias_ref, o_ref, mask_ref, *, nb):
    hp = pl.program_id(1)
    i = pl.program_id(2)
    tq = qT_ref.shape[2]
    hw = qT_ref.shape[1]
    qT = qT_ref[0]
    row = lax.broadcasted_iota(jnp.int32, qT.shape, 0)
    km = kmean_ref[0]
    km_hi = km.astype(BF16)
    km_lo = (km - km_hi.astype(F32)).astype(BF16)
    blk = lax.broadcasted_iota(jnp.int32, (nb, tq), 0)
    blk_f = blk.astype(F32)
    prev = jnp.maximum(i - 1, 0)
    n_far = jnp.maximum(i - 1, 0)

    qTs, mprev, state = [], [], []
    for j in range(2):
        qTh = jnp.where((row >= j * HEAD_DIM) & (row < (j + 1) * HEAD_DIM), qT, jnp.zeros_like(qT))
        qTs.append(qTh)
        gate = (jnp.dot(km_hi, qTh, preferred_element_type=F32)
                + jnp.dot(km_lo, qTh, preferred_element_type=F32))
        gate = jnp.where(blk < i, gate, NEG_INF)
        sel = _select_blocks(gate, blk_f, MOBA_TOP_K) & (blk < i)
        mask_ref[j] = jnp.where(sel & (blk < i - 1), farb_ref[2 * hp + j], NEG_INF)
        mprev.append(jnp.max(jnp.where(sel & (blk == i - 1), 0.0, NEG_INF), axis=0, keepdims=True))

        s = jnp.dot(k_ref[i], qTh, preferred_element_type=F32) + bias_ref[0, j, 0]
        m0 = jnp.max(s, axis=0, keepdims=True)
        p = jnp.exp2(s - m0).astype(BF16)
        acc = jnp.dot(vT_ref[i][j * V_ROWS:(j + 1) * V_ROWS, :], p, preferred_element_type=F32)
        state.append((m0, acc))

    def fold(carry, s_parts, masks, v_parts):
        m_run, acc = carry
        mb = [jnp.max(s, axis=0, keepdims=True) + mk for s, mk in zip(s_parts, masks)]
        m_new = functools.reduce(jnp.maximum, mb, m_run)
        p = jnp.concatenate([jnp.exp2(s - (m_new - mk)) for s, mk in zip(s_parts, masks)],
                            axis=0).astype(BF16)
        vv = v_parts[0] if len(v_parts) == 1 else jnp.concatenate(v_parts, axis=1)
        return m_new, jnp.exp2(m_run - m_new) * acc + jnp.dot(vv, p, preferred_element_type=F32)

    kprev = k_ref[prev]
    vprev = vT_ref[prev]
    for j in range(2):
        s = jnp.dot(kprev, qTs[j], preferred_element_type=F32) + bias_ref[0, j, 1]
        state[j] = fold(state[j], [s], [mprev[j]], [vprev[j * V_ROWS:(j + 1) * V_ROWS, :]])

    def far_step(it, carry):
        n0 = pl.multiple_of(2 * it, 2)
        kp = k_ref[pl.ds(n0, 2)].reshape(2 * MOBA_BLOCK, hw)
        v0 = vT_ref[n0]
        v1 = vT_ref[n0 + 1]
        out = []
        for j in range(2):
            s = jnp.dot(kp, qTs[j], preferred_element_type=F32)
            masks = [mask_ref[j, pl.ds(n0, 1), :], mask_ref[j, pl.ds(n0 + 1, 1), :]]
            vs = [v0[j * V_ROWS:(j + 1) * V_ROWS, :], v1[j * V_ROWS:(j + 1) * V_ROWS, :]]
            out.append(fold(carry[j], [s[:MOBA_BLOCK], s[MOBA_BLOCK:]], masks, vs))
        return tuple(out)

    state = lax.fori_loop(0, (n_far + 1) // 2, far_step, tuple(state))
    outs = [acc[:HEAD_DIM] / acc[HEAD_DIM:HEAD_DIM + 1] for _, acc in state]
    o_ref[0] = jnp.concatenate(outs, axis=0).T.astype(BF16)


def _moba(farb, qT, k3, vT3, kmean, bias, *, batch, seq):
    nb = seq // MOBA_BLOCK
    d_attn = qT.shape[1]
    hw = 2 * HEAD_DIM
    n_hp = d_attn // hw
    kern = functools.partial(_moba_kernel, nb=nb)
    return pl.pallas_call(
        kern,
        grid=(batch, n_hp, nb),
        in_specs=[
            pl.BlockSpec(memory_space=pltpu.SMEM),
            pl.BlockSpec((1, hw, MOBA_BLOCK), lambda b, h, i: (b, h, i)),
            pl.BlockSpec((nb, MOBA_BLOCK, hw), lambda b, h, i: (b, 0, h)),
            pl.BlockSpec((nb, 2 * V_ROWS, MOBA_BLOCK), lambda b, h, i: (b, h, 0)),
            pl.BlockSpec((1, nb, hw), lambda b, h, i: (b, 0, h)),
            pl.BlockSpec((1, 2, 2, MOBA_BLOCK, MOBA_BLOCK), lambda b, h, i: (h, 0, 0, 0, 0)),
        ],
        out_specs=pl.BlockSpec((1, MOBA_BLOCK, hw), lambda b, h, i: (b, i, h)),
        out_shape=jax.ShapeDtypeStruct((batch, seq, d_attn), BF16),
        scratch_shapes=[pltpu.VMEM((2, nb, MOBA_BLOCK), F32)],
        compiler_params=pltpu.CompilerParams(
            dimension_semantics=("arbitrary", "arbitrary", "arbitrary"),
            vmem_limit_bytes=VMEM_LIMIT),
    )(farb, qT, k3, vT3, kmean, bias)


def _route(logits):
    lane = lax.broadcasted_iota(jnp.int32, logits.shape, 1)
    lane_f = lane.astype(F32)
    big = float(ROUTER_LANES)
    ninf = -jnp.inf
    gmask = lane < N_GROUPS
    gmax = jnp.max(jnp.where(gmask, logits, ninf), axis=1, keepdims=True)
    gsum = jnp.sum(jnp.where(gmask, jnp.exp(logits - gmax), 0.0), axis=1, keepdims=True)
    g_val = 1.0 / gsum
    g_idx = jnp.min(jnp.where(gmask & (logits == gmax), lane_f, big), axis=1, keepdims=True)
    e_grp = ((lane - EXPERT_LANE0) >> 3).astype(F32)
    emask = (lane >= EXPERT_LANE0) & (lane < EXPERT_LANE0 + N_EXPERTS) & (e_grp == g_idx)
    el = jnp.where(emask, logits, ninf)
    e1 = jnp.max(el, axis=1, keepdims=True)
    i1 = jnp.min(jnp.where(emask & (el == e1), lane_f, big), axis=1, keepdims=True)
    el2 = jnp.where(lane_f == i1, ninf, el)
    e2 = jnp.max(el2, axis=1, keepdims=True)
    i2 = jnp.min(jnp.where(emask & (lane_f != i1) & (el2 == e2), lane_f, big), axis=1, keepdims=True)
    tt = jnp.exp(e2 - e1)
    w1 = g_val / (1.0 + tt)
    w2 = g_val * tt / (1.0 + tt)
    first_low = i1 < i2
    lo = jnp.minimum(i1, i2) - (EXPERT_LANE0 + EXPERTS_PER_GROUP * g_idx)
    hi = jnp.maximum(i1, i2) - (EXPERT_LANE0 + EXPERTS_PER_GROUP * g_idx)
    pair = lo * (2 * EXPERTS_PER_GROUP - 1 - lo) * 0.5 + (hi - lo - 1.0)
    cls = g_idx * PAIRS_PER_GROUP + pair
    return cls, jnp.where(first_low, w1, w2), jnp.where(first_low, w2, w1)


def _col_to_row(col):
    n = col.shape[0]
    r = lax.broadcasted_iota(jnp.int32, (n, n), 0)
    c = lax.broadcasted_iota(jnp.int32, (n, n), 1)
    return jnp.sum(jnp.where(r == c, col, 0.0), axis=0, keepdims=True)


def _merge_kernel(x_ref, ys_ref, ya_ref, gates_ref, wus_ref, wua_ref, wout_ref, g2_ref, wr_ref, br_ref,
                  tri_ref, x1e_ref, cls_ref, rank_ref, cnt_ref):
    d = x_ref.shape[1]
    a = jnp.dot(ys_ref[...], wus_ref[...], preferred_element_type=F32)
    b = jnp.dot(ya_ref[...], wua_ref[...], preferred_element_type=F32)
    gts = gates_ref[...].astype(F32)
    merged = _sigmoid(gts[:, :d]) * a + _sigmoid(gts[:, d:]) * b
    x1 = x_ref[...] + jnp.dot(merged.astype(BF16), wout_ref[...], preferred_element_type=F32)
    xn = _rms(x1, g2_ref[...]).astype(BF16)
    logits = jnp.dot(xn, wr_ref[...], preferred_element_type=F32) + br_ref[...]
    cls, wa, wb = _route(logits)
    lane = lax.broadcasted_iota(jnp.int32, logits.shape, 1)
    x1e_ref[:, :d] = x1
    x1e_ref[:, d:] = jnp.where(lane == 0, wa, jnp.where(lane == 1, wb, 0.0))
    onehot = lane.astype(F32) == cls
    before = jnp.dot(tri_ref[...], onehot.astype(BF16), preferred_element_type=F32)
    rank = jnp.sum(jnp.where(onehot, before, 0.0), axis=1, keepdims=True)
    cls_ref[0] = _col_to_row(cls).astype(jnp.int32)
    rank_ref[0] = _col_to_row(rank).astype(jnp.int32)
    cnt_ref[0] = jnp.sum(onehot.astype(F32), axis=0, keepdims=True).astype(jnp.int32)


def _merge(x2, ys, ya, gates, wus, wua, wout, g2, wr, br, *, tm):
    t, d = x2.shape
    nt = t // tm
    tri = jnp.tril(jnp.ones((tm, tm), F32), -1).astype(BF16)
    return pl.pallas_call(
        _merge_kernel,
        grid=(nt,),
        in_specs=[
            pl.BlockSpec((tm, d), lambda i: (i, 0)),
            pl.BlockSpec((tm, ys.shape[1]), lambda i: (i, 0)),
            pl.BlockSpec((tm, ya.shape[1]), lambda i: (i, 0)),
            pl.BlockSpec((tm, gates.shape[1]), lambda i: (i, 0)),
            _const_spec(wus.shape), _const_spec(wua.shape), _const_spec(wout.shape),
            _const_spec(g2.shape), _const_spec(wr.shape), _const_spec(br.shape), _const_spec(tri.shape),
        ],
        out_specs=[
            pl.BlockSpec((tm, d + ROUTER_LANES), lambda i: (i, 0)),
            pl.BlockSpec((1, 1, tm), lambda i: (i, 0, 0)),
            pl.BlockSpec((1, 1, tm), lambda i: (i, 0, 0)),
            pl.BlockSpec((1, 1, ROUTER_LANES), lambda i: (i, 0, 0)),
        ],
        out_shape=[
            jax.ShapeDtypeStruct((t, d + ROUTER_LANES), F32),
            jax.ShapeDtypeStruct((nt, 1, tm), jnp.int32),
            jax.ShapeDtypeStruct((nt, 1, tm), jnp.int32),
            jax.ShapeDtypeStruct((nt, 1, ROUTER_LANES), jnp.int32),
        ],
        compiler_params=pltpu.CompilerParams(
            dimension_semantics=("arbitrary",), vmem_limit_bytes=VMEM_LIMIT),
    )(x2, ys, ya, gates, wus, wua, wout, g2, wr, br, tri)


def _permute_kernel(pos_ref, src_ref, dst_ref, sem, *, rows, scatter):
    i = pl.program_id(0)
    base = i * rows

    def copy(r):
        p = pos_ref[0, 0, r]
        if scatter:
            return pltpu.make_async_copy(src_ref.at[pl.ds(base + r, 1)], dst_ref.at[pl.ds(p, 1)], sem)
        return pltpu.make_async_copy(src_ref.at[pl.ds(p, 1)], dst_ref.at[pl.ds(base + r, 1)], sem)

    def issue(r, carry):
        copy(r).start()
        return carry

    lax.fori_loop(0, rows, issue, 0, unroll=8)

    def drain(r, carry):
        copy(r).wait()
        return carry

    lax.fori_loop(0, rows, drain, 0, unroll=8)


def _permute_rows(pos, src, *, scatter):
    nt, _, rows = pos.shape
    kern = functools.partial(_permute_kernel, rows=rows, scatter=scatter)
    return pl.pallas_call(
        kern,
        grid=(nt,),
        in_specs=[
            pl.BlockSpec((1, 1, rows), lambda i: (i, 0, 0), memory_space=pltpu.SMEM),
            pl.BlockSpec(memory_space=pl.ANY),
        ],
        out_specs=pl.BlockSpec(memory_space=pl.ANY),
        out_shape=jax.ShapeDtypeStruct(src.shape, src.dtype),
        scratch_shapes=[pltpu.SemaphoreType.DMA(())],
        compiler_params=pltpu.CompilerParams(dimension_semantics=("arbitrary",)),
    )(pos, src)


def _expert(xn, w1_ref, w3_ref, w2_ref):
    h1 = jnp.dot(xn, w1_ref[0], preferred_element_type=F32)
    h3 = jnp.dot(xn, w3_ref[0], preferred_element_type=F32)
    hid = (h1 * _sigmoid(h1) * h3).astype(BF16)
    return jnp.dot(hid, w2_ref[0], preferred_element_type=F32)


def _moe_kernel(tile_ref, ea_ref, eb_ref, lo_ref, hi_ref, first_ref, last_ref,
                x_ref, w1a_ref, w3a_ref, w2a_ref, w1b_ref, w3b_ref, w2b_ref, g2_ref, gf_ref,
                y_ref, acc_ref):
    w = pl.program_id(0)
    d = y_ref.shape[1]
    lo = lo_ref[w]
    hi = hi_ref[w]

    @pl.when(hi > lo)
    def _():
        x1 = x_ref[:, :d]
        row = lax.broadcasted_iota(jnp.int32, (x1.shape[0], 1), 0)
        inseg = (row >= lo) & (row < hi)
        ca = jnp.where(inseg, x_ref[:, d:d + 1], 0.0)
        cb = jnp.where(inseg, x_ref[:, d + 1:d + 2], 0.0)
        xn = _rms(x1, g2_ref[...]).astype(BF16)
        contrib = (ca * _expert(xn, w1a_ref, w3a_ref, w2a_ref)
                   + cb * _expert(xn, w1b_ref, w3b_ref, w2b_ref))

        @pl.when(first_ref[w] == 1)
        def _():
            acc_ref[...] = contrib

        @pl.when(first_ref[w] == 0)
        def _():
            acc_ref[...] += contrib

        @pl.when(last_ref[w] == 1)
        def _():
            y_ref[...] = _rms(x1 + acc_ref[...], gf_ref[...])


def _moe(items, x1s, w1, w3, w2, g2, gf, *, rows):
    t, de_ = x1s.shape
    d = g2.shape[1]
    ne, _, de = w1.shape
    n_items = items[0].shape[0]

    def wa_map(w, tile, ea, eb, lo, hi, first, last):
        return (ea[w], 0, 0)

    def wb_map(w, tile, ea, eb, lo, hi, first, last):
        return (eb[w], 0, 0)

    def row_map(w, tile, ea, eb, lo, hi, first, last):
        return (tile[w], 0)

    def const_map(w, *_):
        return (0, 0)

    grid_spec = pltpu.PrefetchScalarGridSpec(
        num_scalar_prefetch=len(items),
        grid=(n_items,),
        in_specs=[
            pl.BlockSpec((rows, de_), row_map),
            pl.BlockSpec((1, d, de), wa_map), pl.BlockSpec((1, d, de), wa_map), pl.BlockSpec((1, de, d), wa_map),
            pl.BlockSpec((1, d, de), wb_map), pl.BlockSpec((1, d, de), wb_map), pl.BlockSpec((1, de, d), wb_map),
            pl.BlockSpec(g2.shape, const_map), pl.BlockSpec(gf.shape, const_map),
        ],
        out_specs=pl.BlockSpec((rows, d), row_map),
        scratch_shapes=[pltpu.VMEM((rows, d), F32)],
    )
    return pl.pallas_call(
        _moe_kernel,
        grid_spec=grid_spec,
        out_shape=jax.ShapeDtypeStruct((t, d), F32),
        compiler_params=pltpu.CompilerParams(
            dimension_semantics=("arbitrary",), vmem_limit_bytes=VMEM_LIMIT),
    )(*items, x1s, w1, w3, w2, w1, w3, w2, g2, gf)


def _pair_table():
    lo, hi = [], []
    for a in range(EXPERTS_PER_GROUP):
        for b in range(a + 1, EXPERTS_PER_GROUP):
            lo.append(a)
            hi.append(b)
    return jnp.asarray(lo, jnp.int32), jnp.asarray(hi, jnp.int32)


def _routing_tables(cls, rank, cnt, *, rows):
    nt, _, tm = cls.shape
    t = nt * tm
    c = cnt[:, 0, :]
    tot = jnp.sum(c, axis=0)
    class_off = jnp.cumsum(tot) - tot
    base = class_off[None, :] + jnp.cumsum(c, axis=0) - c
    pos = jnp.take_along_axis(base, cls[:, 0, :], axis=1) + rank[:, 0, :]

    n_tiles = t // rows
    starts = jnp.concatenate([jnp.arange(n_tiles, dtype=jnp.int32) * rows, class_off[1:N_CLASSES]])
    starts = jnp.sort(starts)
    ends = jnp.concatenate([starts[1:], jnp.full((1,), t, jnp.int32)])
    order = jnp.argsort(ends <= starts, stable=True)
    starts, ends = starts[order], ends[order]
    n_items = starts.shape[0]
    idx = jnp.arange(n_items)
    n_real = jnp.sum(ends > starts)
    valid = idx < n_real
    last_real = jnp.maximum(n_real - 1, 0)
    starts = jnp.where(valid, starts, starts[last_real])
    tile = jnp.minimum(starts // rows, n_tiles - 1)
    klass = jnp.searchsorted(class_off[:N_CLASSES], starts, side='right') - 1
    pair_lo, pair_hi = _pair_table()
    grp = klass // PAIRS_PER_GROUP
    ea = grp * EXPERTS_PER_GROUP + pair_lo[klass % PAIRS_PER_GROUP]
    eb = grp * EXPERTS_PER_GROUP + pair_hi[klass % PAIRS_PER_GROUP]
    lo = jnp.where(valid, starts - tile * rows, 0)
    hi = jnp.where(valid, ends - tile * rows, 0)
    first = valid & ((idx == 0) | (tile != jnp.roll(tile, 1)))
    last = valid & ((idx == last_real) | (tile != jnp.roll(tile, -1)))
    items = tuple(a.astype(jnp.int32) for a in (tile, ea, eb, lo, hi, first, last))
    return pos.reshape(nt, 1, tm).astype(jnp.int32), items


def _t5_bucket(dist):
    n = jnp.maximum(dist, 0)
    max_exact = NUM_BUCKETS // 2
    large = max_exact + (jnp.log(jnp.maximum(n, max_exact).astype(F32) / max_exact)
                         / math.log(MAX_DISTANCE / max_exact)
                         * (NUM_BUCKETS - max_exact)).astype(jnp.int32)
    return jnp.where(n < max_exact, n, jnp.minimum(large, NUM_BUCKETS - 1))


def _bias_tables(rel_bias):
    kk = jnp.arange(MOBA_BLOCK)[:, None]
    qq = jnp.arange(MOBA_BLOCK)[None, :]
    rb = rel_bias.astype(F32) * LOG2E
    d_own = qq - kk

    def lookup(dist):
        oh = jax.nn.one_hot(_t5_bucket(dist), NUM_BUCKETS, dtype=F32)
        return jnp.einsum('hn,kqn->hkq', rb, oh, precision=lax.Precision.HIGHEST)

    own = jnp.where((d_own >= 0)[None], lookup(d_own), NEG_INF)
    prv = lookup(d_own + MOBA_BLOCK)
    tab = jnp.stack([own, prv], axis=1)
    far = rb[:, NUM_BUCKETS - 1]
    return tab.reshape(rb.shape[0] // 2, 2, 2, MOBA_BLOCK, MOBA_BLOCK), far


def _ssm_params(lam_re, lam_im, log_step, b_re, b_im, c_re, c_im):
    g, p = lam_re.shape
    step = jnp.exp(log_step)[:, None]
    decay = jnp.exp(lam_re * step)
    a_re = decay * jnp.cos(lam_im * step)
    a_im = decay * jnp.sin(lam_im * step)
    denom = lam_re * lam_re + lam_im * lam_im
    nr, ni = a_re - 1.0, a_im
    coef_re = (nr * lam_re + ni * lam_im) / denom
    coef_im = (ni * lam_re - nr * lam_im) / denom
    bb_re = coef_re[..., None] * b_re - coef_im[..., None] * b_im
    bb_im = coef_re[..., None] * b_im + coef_im[..., None] * b_re
    eye = jnp.eye(g, dtype=F32)
    hc = b_re.shape[2]

    def in_mat(m):
        return (eye[:, None, :, None] * m.transpose(0, 2, 1)[:, :, None, :]).reshape(g * hc, g * p)

    def out_mat(m):
        return (eye[:, None, :, None] * m.transpose(0, 2, 1)[:, :, None, :]).reshape(g * p, g * hc)

    bb = jnp.concatenate([in_mat(bb_re), in_mat(bb_im)], axis=1).astype(BF16)
    cc = jnp.concatenate([out_mat(c_re), -out_mat(c_im)], axis=0).astype(BF16)
    return bb, cc, a_re.reshape(1, g * p), a_im.reshape(1, g * p)


def kernel(x, ln1_g, w_in, b_gate, ssm_lambda_re, ssm_lambda_im, ssm_log_step, ssm_b_re, ssm_b_im,
           ssm_c_re, ssm_c_im, ssm_d, w_glu, b_glu, w_up_ssm, w_up_attn, rel_bias, w_out, ln2_g,
           w_router_group, b_router_group, w_router_expert, b_router_expert, w1, w3, w2, ln_f_g):
    assert w_in.shape[0] == 1, "single-layer block"
    batch, seq, d = x.shape
    t = batch * seq
    d_ssm = w_glu.shape[1]
    d_attn = w_up_attn.shape[1]
    n_heads = d_attn // HEAD_DIM
    o1, o2, o3, o4 = d_ssm, d_ssm + d_attn, d_ssm + 2 * d_attn, d_ssm + 3 * d_attn
    x2 = x.reshape(t, d)

    wl = w_in[0]
    w_main = jnp.concatenate([wl[:, :o1], wl[:, o2:o3], wl[:, o4:]], axis=1).astype(BF16)
    wqT = (wl[:, o1:o2] * (HEAD_DIM ** -0.5 * LOG2E)).T.astype(BF16)
    wv = wl[:, o3:o4].T.reshape(n_heads, HEAD_DIM, d)
    wvT = jnp.concatenate([wv, jnp.zeros((n_heads, BF16_ROWS, d), F32)], axis=1)
    wvT = wvT.reshape(n_heads * V_ROWS, d).astype(BF16)
    vb = jnp.concatenate([jnp.zeros((n_heads, HEAD_DIM, 1), F32), jnp.ones((n_heads, BF16_ROWS, 1), F32)],
                         axis=1).reshape(n_heads * V_ROWS, 1)
    u, k3, kmean, qT, vT3, gates = _inproj(
        x2, ln1_g[0][None], w_main, wqT, wvT, vb, b_gate[0][None], batch=batch, seq=seq, tm=512)

    bb, cc, ar, ai = _ssm_params(ssm_lambda_re[0], ssm_lambda_im[0], ssm_log_step[0],
                                 ssm_b_re[0], ssm_b_im[0], ssm_c_re[0], ssm_c_im[0])
    y_ssm = _s5(u.reshape(batch, seq, d_ssm), bb, cc, ar, ai, ssm_d[0].reshape(1, d_ssm),
                w_glu[0].astype(BF16), b_glu[0][None], ts=128, chunk=512)

    bias, far = _bias_tables(rel_bias)
    y_attn = _moba(far, qT, k3, vT3, kmean.reshape(batch, seq // MOBA_BLOCK, d_attn), bias,
                   batch=batch, seq=seq)

    wr = jnp.zeros((d, ROUTER_LANES), F32)
    wr = wr.at[:, :N_GROUPS].set(w_router_group[0])
    wr = wr.at[:, EXPERT_LANE0:EXPERT_LANE0 + N_EXPERTS].set(w_router_expert[0]).astype(BF16)
    br = jnp.zeros((1, ROUTER_LANES), F32)
    br = br.at[0, :N_GROUPS].set(b_router_group[0])
    br = br.at[0, EXPERT_LANE0:EXPERT_LANE0 + N_EXPERTS].set(b_router_expert[0])
    x1e, cls, rank, cnt = _merge(
        x2, y_ssm.reshape(t, d_ssm), y_attn.reshape(t, d_attn), gates,
        w_up_ssm[0].astype(BF16), w_up_attn[0].astype(BF16), w_out[0].astype(BF16),
        ln2_g[0][None], wr, br, tm=512)

    pos, items = _routing_tables(cls, rank, cnt, rows=MOE_ROWS)
    x1s = _permute_rows(pos, x1e, scatter=True)
    ys = _moe(items, x1s, w1[0].astype(BF16), w3[0].astype(BF16), w2[0].astype(BF16),
              ln2_g[0][None], ln_f_g[None], rows=MOE_ROWS)
    y = _permute_rows(pos, ys, scatter=False)
    return y.reshape(batch, seq, d)
```

```python
import functools
import math

import jax
import jax.numpy as jnp
from jax import lax
from jax.experimental import pallas as pl
from jax.experimental.pallas import tpu as pltpu

F32 = jnp.float32
BF16 = jnp.bfloat16

SSM_GROUP = 16
SSM_STATE = 64
HEAD_DIM = 64
MOBA_BLOCK = 256
MOBA_TOP_K = 3
NUM_BUCKETS = 32
MAX_DISTANCE = 128
N_GROUPS = 4
EXPERTS_PER_GROUP = 8
N_EXPERTS = N_GROUPS * EXPERTS_PER_GROUP
PAIRS_PER_GROUP = EXPERTS_PER_GROUP * (EXPERTS_PER_GROUP - 1) // 2
N_CLASSES = N_GROUPS * PAIRS_PER_GROUP
MOE_ROWS = 256
RMS_EPS = 1e-6
NEG_INF = -1e30
LOG2E = 1.4426950408889634

LANES = 128
BF16_ROWS = 16
V_ROWS = HEAD_DIM + BF16_ROWS
ROUTER_LANES = LANES
EXPERT_LANE0 = N_GROUPS
VMEM_LIMIT = 56 * 1024 * 1024


def _sigmoid(x):
    return 1.0 / (1.0 + jnp.exp(-x))


def _rms(x, g):
    ms = jnp.mean(x * x, axis=-1, keepdims=True)
    return x * lax.rsqrt(ms + RMS_EPS) * g


def _const_spec(shape):
    n = len(shape)
    return pl.BlockSpec(shape, lambda *_: (0,) * n)


def _inproj_kernel(x_ref, g_ref, w_ref, wqT_ref, wvT_ref, vb_ref, bg_ref,
                   u_ref, k_ref, kmean_ref, qT_ref, vT_ref, gates_ref, *, nblk, d_ssm, d_attn):
    hn = _rms(x_ref[...], g_ref[...]).astype(BF16)
    acc = jnp.dot(hn, w_ref[...], preferred_element_type=F32)
    u_ref[...] = acc[:, :d_ssm]
    kf = acc[:, d_ssm:d_ssm + d_attn].reshape(nblk, MOBA_BLOCK, d_attn)
    k_ref[...] = kf.astype(BF16)
    kmean_ref[...] = jnp.mean(kf, axis=1, keepdims=True)
    gates_ref[...] = (acc[:, d_ssm + d_attn:] + bg_ref[...]).astype(BF16)
    nt = (((1,), (1,)), ((), ()))
    qT = lax.dot_general(wqT_ref[...], hn, nt, preferred_element_type=F32)
    qT_ref[0] = qT.astype(BF16)
    vT = lax.dot_general(wvT_ref[...], hn, nt, preferred_element_type=F32) + vb_ref[...]
    for j in range(nblk):
        vT_ref[j] = vT[:, j * MOBA_BLOCK:(j + 1) * MOBA_BLOCK].astype(BF16)


def _inproj(x2, g1, w_main, wqT, wvT, vb, bg, *, batch, seq, tm):
    t, d = x2.shape
    d_attn = wqT.shape[0]
    v_rows = wvT.shape[0]
    d_ssm = w_main.shape[1] - d_attn - bg.shape[1]
    nblk = tm // MOBA_BLOCK
    tiles_per_seq = seq // tm
    nb_total = t // MOBA_BLOCK
    kern = functools.partial(_inproj_kernel, nblk=nblk, d_ssm=d_ssm, d_attn=d_attn)
    return pl.pallas_call(
        kern,
        grid=(t // tm,),
        in_specs=[
            pl.BlockSpec((tm, d), lambda i: (i, 0)),
            _const_spec(g1.shape),
            _const_spec(w_main.shape),
            _const_spec(wqT.shape),
            _const_spec(wvT.shape),
            _const_spec(vb.shape),
            _const_spec(bg.shape),
        ],
        out_specs=[
            pl.BlockSpec((tm, d_ssm), lambda i: (i, 0)),
            pl.BlockSpec((nblk, MOBA_BLOCK, d_attn), lambda i: (i, 0, 0)),
            pl.BlockSpec((nblk, 1, d_attn), lambda i: (i, 0, 0)),
            pl.BlockSpec((1, d_attn, tm), lambda i: (i // tiles_per_seq, 0, i % tiles_per_seq)),
            pl.BlockSpec((nblk, v_rows, MOBA_BLOCK), lambda i: (i, 0, 0)),
            pl.BlockSpec((tm, bg.shape[1]), lambda i: (i, 0)),
        ],
        out_shape=[
            jax.ShapeDtypeStruct((t, d_ssm), F32),
            jax.ShapeDtypeStruct((nb_total, MOBA_BLOCK, d_attn), BF16),
            jax.ShapeDtypeStruct((nb_total, 1, d_attn), F32),
            jax.ShapeDtypeStruct((batch, d_attn, seq), BF16),
            jax.ShapeDtypeStruct((nb_total, v_rows, MOBA_BLOCK), BF16),
            jax.ShapeDtypeStruct((t, bg.shape[1]), BF16),
        ],
        compiler_params=pltpu.CompilerParams(
            dimension_semantics=("arbitrary",), vmem_limit_bytes=VMEM_LIMIT),
    )(x2, g1, w_main, wqT, wvT, vb, bg)


def _s5_kernel(u_ref, bb_ref, cc_ref, ar_ref, ai_ref, d_ref, wglu_ref, bglu_ref,
               y_ref, utb_ref, sbuf_ref, state_ref, ytb_ref, *, batch, ts, n_state, chunk):
    d_ssm = u_ref.shape[2]
    n_slab = d_ssm // LANES

    @pl.when(pl.program_id(0) == 0)
    def _():
        state_ref[...] = jnp.zeros_like(state_ref)

    for b in range(batch):
        ub = u_ref[b]
        for j in range(n_slab):
            utb_ref[j, pl.ds(b, ts, stride=batch), :] = ub[:, j * LANES:(j + 1) * LANES]
    u_tb = jnp.concatenate([utb_ref[j] for j in range(n_slab)], axis=1)
    sbuf_ref[...] = jnp.dot(u_tb.astype(BF16), bb_ref[...], preferred_element_type=F32)

    for c in range(n_state // chunk):
        re = slice(c * chunk, (c + 1) * chunk)
        im = slice(n_state + c * chunk, n_state + (c + 1) * chunk)
        ar = jnp.broadcast_to(ar_ref[:, re], (batch, chunk))
        ai = jnp.broadcast_to(ai_ref[:, re], (batch, chunk))

        def step(t, carry, re=re, im=im, ar=ar, ai=ai):
            hr, hi = carry
            r0 = pl.multiple_of(t * batch, batch)
            nhr = ar * hr - ai * hi + sbuf_ref[pl.ds(r0, batch), re]
            nhi = ar * hi + ai * hr + sbuf_ref[pl.ds(r0, batch), im]
            sbuf_ref[pl.ds(r0, batch), re] = nhr
            sbuf_ref[pl.ds(r0, batch), im] = nhi
            return nhr, nhi

        hr, hi = lax.fori_loop(0, ts, step, (state_ref[:, re], state_ref[:, im]), unroll=8)
        state_ref[:, re] = hr
        state_ref[:, im] = hi

    y = jnp.dot(sbuf_ref[...].astype(BF16), cc_ref[...], preferred_element_type=F32)
    y = y + d_ref[...] * u_tb
    g = jax.nn.gelu(y)
    z = jnp.dot(g.astype(BF16), wglu_ref[...], preferred_element_type=F32) + bglu_ref[...]
    out = g * _sigmoid(z)
    for j in range(n_slab):
        ytb_ref[j] = out[:, j * LANES:(j + 1) * LANES]
    for b in range(batch):
        y_ref[b] = jnp.concatenate(
            [ytb_ref[j, pl.ds(b, ts, stride=batch), :] for j in range(n_slab)], axis=1).astype(BF16)


def _s5(u3, bb, cc, ar, ai, dvec, wglu, bglu, *, ts, chunk):
    batch, seq, d_ssm = u3.shape
    n_state = ar.shape[1]
    kern = functools.partial(_s5_kernel, batch=batch, ts=ts, n_state=n_state, chunk=chunk)
    return pl.pallas_call(
        kern,
        grid=(seq // ts,),
        in_specs=[
            pl.BlockSpec((batch, ts, d_ssm), lambda i: (0, i, 0)),
            _const_spec(bb.shape), _const_spec(cc.shape), _const_spec(ar.shape), _const_spec(ai.shape),
            _const_spec(dvec.shape), _const_spec(wglu.shape), _const_spec(bglu.shape),
        ],
        out_specs=pl.BlockSpec((batch, ts, d_ssm), lambda i: (0, i, 0)),
        out_shape=jax.ShapeDtypeStruct((batch, seq, d_ssm), BF16),
        scratch_shapes=[
            pltpu.VMEM((d_ssm // LANES, ts * batch, LANES), F32),
            pltpu.VMEM((ts * batch, 2 * n_state), F32),
            pltpu.VMEM((batch, 2 * n_state), F32),
            pltpu.VMEM((d_ssm // LANES, ts * batch, LANES), F32),
        ],
        compiler_params=pltpu.CompilerParams(
            dimension_semantics=("arbitrary",), vmem_limit_bytes=VMEM_LIMIT),
    )(u3, bb, cc, ar, ai, dvec, wglu, bglu)


def _select_blocks(gate, blk_f, n_pick):
    sel = jnp.zeros(gate.shape, jnp.bool_)
    for _ in range(n_pick):
        mx = jnp.max(gate, axis=0, keepdims=True)
        idx = jnp.min(jnp.where(gate == mx, blk_f, float(gate.shape[0])), axis=0, keepdims=True)
        hit = blk_f == idx
        sel = sel | hit
        gate = jnp.where(hit, -jnp.inf, gate)
    return sel


def _moba_kernel(farb_ref, qT_ref, k_ref, vT_ref, kmean_ref, bias_ref, o_ref,
                 mask_ref, s0_ref, s1_ref, p0_ref, p1_ref, acc_ref, *, nb):
    hp = pl.program_id(1)
    i = pl.program_id(2)
    tq = qT_ref.shape[2]
    hw = qT_ref.shape[1]
    qT = qT_ref[0]
    row = lax.broadcasted_iota(jnp.int32, qT.shape, 0)
    km = kmean_ref[0]
    km_hi = km.astype(BF16)
    km_lo = (km - km_hi.astype(F32)).astype(BF16)
    blk = lax.broadcasted_iota(jnp.int32, (nb, tq), 0)
    blk_f = blk.astype(F32)
    prev = jnp.maximum(i - 1, 0)
    n_far = jnp.maximum(i - 1, 0)
    n_iter = (n_far + 3) // 4

    qTs, near_masks = [], []
    for j in range(2):
        qTh = jnp.where((row >= j * HEAD_DIM) & (row < (j + 1) * HEAD_DIM), qT, jnp.zeros_like(qT))
        qTs.append(qTh)
        gate = (jnp.dot(km_hi, qTh, preferred_element_type=F32)
                + jnp.dot(km_lo, qTh, preferred_element_type=F32))
        gate = jnp.where(blk < i, gate, NEG_INF)
        sel = _select_blocks(gate, blk_f, MOBA_TOP_K) & (blk < i)
        mask_ref[j] = jnp.where(sel & (blk < i - 1), farb_ref[2 * hp + j], NEG_INF)
        mprev = jnp.max(jnp.where(sel & (blk == i - 1), 0.0, NEG_INF), axis=0, keepdims=True)
        near_masks.append((mprev, jnp.zeros_like(mprev)))

    def scores(dst_ref, kp):
        for j in range(2):
            dst_ref[j] = jnp.dot(kp, qTs[j], preferred_element_type=F32)

    def far_keys(n0):
        n0 = pl.multiple_of(jnp.minimum(n0, nb - 2), 2)
        return k_ref[pl.ds(n0, 2)].reshape(2 * MOBA_BLOCK, hw)

    def far_masks(n0):
        return [(mask_ref[j, pl.ds(n0, 1), :], mask_ref[j, pl.ds(n0 + 1, 1), :]) for j in range(2)]

    def softmax(src_ref, dst_ref, m_run, masks, bias=None):
        m_out, alpha = [], []
        for j in range(2):
            s = src_ref[j]
            if bias is not None:
                s = s + bias[j]
            mk0, mk1 = masks[j]
            s0, s1 = s[:MOBA_BLOCK], s[MOBA_BLOCK:]
            m_new = jnp.maximum(m_run[j], jnp.maximum(jnp.max(s0, axis=0, keepdims=True) + mk0,
                                                      jnp.max(s1, axis=0, keepdims=True) + mk1))
            dst_ref[j, :MOBA_BLOCK] = jnp.exp2(s0 - (m_new - mk0)).astype(BF16)
            dst_ref[j, MOBA_BLOCK:] = jnp.exp2(s1 - (m_new - mk1)).astype(BF16)
            m_out.append(m_new)
            alpha.append(jnp.exp2(m_run[j] - m_new))
        return tuple(m_out), tuple(alpha)

    def pv(p_ref, alpha, na, nb_):
        va = vT_ref[na]
        vb = vT_ref[nb_]
        for j in range(2):
            vrows = slice(j * V_ROWS, (j + 1) * V_ROWS)
            vv = jnp.concatenate([va[vrows, :], vb[vrows, :]], axis=1)
            acc_ref[j] = alpha[j] * acc_ref[j] + jnp.dot(vv, p_ref[j], preferred_element_type=F32)

    scores(s0_ref, jnp.concatenate([k_ref[prev], k_ref[i]], axis=0))
    acc_ref[...] = jnp.zeros_like(acc_ref)
    m_init = tuple(jnp.full((1, tq), -jnp.inf, F32) for _ in range(2))
    m_run, a0 = softmax(s0_ref, p0_ref, m_init, near_masks, bias=[bias_ref[0, 0], bias_ref[0, 1]])
    scores(s1_ref, far_keys(0))

    def body(g, carry):
        m_run, a0 = carry
        f0 = 4 * g
        m_run, a1 = softmax(s1_ref, p1_ref, m_run, far_masks(f0))
        pv(p0_ref, a0, jnp.where(g == 0, prev, f0 - 2), jnp.where(g == 0, i, f0 - 1))
        scores(s0_ref, far_keys(f0 + 2))
        m_run, a0 = softmax(s0_ref, p0_ref, m_run, far_masks(f0 + 2))
        pv(p1_ref, a1, f0, f0 + 1)
        scores(s1_ref, far_keys(f0 + 4))
        return m_run, a0

    m_run, a0 = lax.fori_loop(0, n_iter, body, (m_run, a0))
    last = 4 * n_iter
    pv(p0_ref, a0, jnp.where(n_iter == 0, prev, last - 2), jnp.where(n_iter == 0, i, last - 1))
    outs = [acc_ref[j][:HEAD_DIM] / acc_ref[j][HEAD_DIM:HEAD_DIM + 1] for j in range(2)]
    o_ref[0] = jnp.concatenate(outs, axis=0).T.astype(BF16)


def _moba(farb, qT, k3, vT3, kmean, bias, *, batch, seq):
    nb = seq // MOBA_BLOCK
    assert nb % 4 == 0, "far blocks are consumed four per loop trip"
    d_attn = qT.shape[1]
    hw = 2 * HEAD_DIM
    n_hp = d_attn // hw
    kern = functools.partial(_moba_kernel, nb=nb)
    return pl.pallas_call(
        kern,
        grid=(batch, n_hp, nb),
        in_specs=[
            pl.BlockSpec(memory_space=pltpu.SMEM),
            pl.BlockSpec((1, hw, MOBA_BLOCK), lambda b, h, i: (b, h, i)),
            pl.BlockSpec((nb, MOBA_BLOCK, hw), lambda b, h, i: (b, 0, h)),
            pl.BlockSpec((nb, 2 * V_ROWS, MOBA_BLOCK), lambda b, h, i: (b, h, 0)),
            pl.BlockSpec((1, nb, hw), lambda b, h, i: (b, 0, h)),
            pl.BlockSpec((1, 2, 2 * MOBA_BLOCK, MOBA_BLOCK), lambda b, h, i: (h, 0, 0, 0)),
        ],
        out_specs=pl.BlockSpec((1, MOBA_BLOCK, hw), lambda b, h, i: (b, i, h)),
        out_shape=jax.ShapeDtypeStruct((batch, seq, d_attn), BF16),
        scratch_shapes=[
            pltpu.VMEM((2, nb, MOBA_BLOCK), F32),
            pltpu.VMEM((2, 2 * MOBA_BLOCK, MOBA_BLOCK), F32),
            pltpu.VMEM((2, 2 * MOBA_BLOCK, MOBA_BLOCK), F32),
            pltpu.VMEM((2, 2 * MOBA_BLOCK, MOBA_BLOCK), BF16),
            pltpu.VMEM((2, 2 * MOBA_BLOCK, MOBA_BLOCK), BF16),
            pltpu.VMEM((2, V_ROWS, MOBA_BLOCK), F32),
        ],
        compiler_params=pltpu.CompilerParams(
            dimension_semantics=("arbitrary", "arbitrary", "arbitrary"),
            vmem_limit_bytes=VMEM_LIMIT),
    )(farb, qT, k3, vT3, kmean, bias)


def _route(logits):
    lane = lax.broadcasted_iota(jnp.int32, logits.shape, 1)
    lane_f = lane.astype(F32)
    big = float(ROUTER_LANES)
    ninf = -jnp.inf
    gmask = lane < N_GROUPS
    gmax = jnp.max(jnp.where(gmask, logits, ninf), axis=1, keepdims=True)
    gsum = jnp.sum(jnp.where(gmask, jnp.exp(logits - gmax), 0.0), axis=1, keepdims=True)
    g_val = 1.0 / gsum
    g_idx = jnp.min(jnp.where(gmask & (logits == gmax), lane_f, big), axis=1, keepdims=True)
    e_grp = ((lane - EXPERT_LANE0) >> 3).astype(F32)
    emask = (lane >= EXPERT_LANE0) & (lane < EXPERT_LANE0 + N_EXPERTS) & (e_grp == g_idx)
    el = jnp.where(emask, logits, ninf)
    e1 = jnp.max(el, axis=1, keepdims=True)
    i1 = jnp.min(jnp.where(emask & (el == e1), lane_f, big), axis=1, keepdims=True)
    el2 = jnp.where(lane_f == i1, ninf, el)
    e2 = jnp.max(el2, axis=1, keepdims=True)
    i2 = jnp.min(jnp.where(emask & (lane_f != i1) & (el2 == e2), lane_f, big), axis=1, keepdims=True)
    tt = jnp.exp(e2 - e1)
    w1 = g_val / (1.0 + tt)
    w2 = g_val * tt / (1.0 + tt)
    first_low = i1 < i2
    lo = jnp.minimum(i1, i2) - (EXPERT_LANE0 + EXPERTS_PER_GROUP * g_idx)
    hi = jnp.maximum(i1, i2) - (EXPERT_LANE0 + EXPERTS_PER_GROUP * g_idx)
    pair = lo * (2 * EXPERTS_PER_GROUP - 1 - lo) * 0.5 + (hi - lo - 1.0)
    cls = g_idx * PAIRS_PER_GROUP + pair
    return cls, jnp.where(first_low, w1, w2), jnp.where(first_low, w2, w1)


def _col_to_row(col):
    n = col.shape[0]
    r = lax.broadcasted_iota(jnp.int32, (n, n), 0)
    c = lax.broadcasted_iota(jnp.int32, (n, n), 1)
    return jnp.sum(jnp.where(r == c, col, 0.0), axis=0, keepdims=True)


def _merge_kernel(x_ref, ys_ref, ya_ref, gates_ref, wus_ref, wua_ref, wout_ref, g2_ref, wr_ref, br_ref,
                  tri_ref, x1e_ref, cls_ref, rank_ref, cnt_ref):
    d = x_ref.shape[1]
    a = jnp.dot(ys_ref[...], wus_ref[...], preferred_element_type=F32)
    b = jnp.dot(ya_ref[...], wua_ref[...], preferred_element_type=F32)
    gts = gates_ref[...].astype(F32)
    merged = _sigmoid(gts[:, :d]) * a + _sigmoid(gts[:, d:]) * b
    x1 = x_ref[...] + jnp.dot(merged.astype(BF16), wout_ref[...], preferred_element_type=F32)
    xn = _rms(x1, g2_ref[...]).astype(BF16)
    logits = jnp.dot(xn, wr_ref[...], preferred_element_type=F32) + br_ref[...]
    cls, wa, wb = _route(logits)
    lane = lax.broadcasted_iota(jnp.int32, logits.shape, 1)
    x1e_ref[:, :d] = x1
    x1e_ref[:, d:] = jnp.where(lane == 0, wa, jnp.where(lane == 1, wb, 0.0))
    onehot = lane.astype(F32) == cls
    before = jnp.dot(tri_ref[...], onehot.astype(BF16), preferred_element_type=F32)
    rank = jnp.sum(jnp.where(onehot, before, 0.0), axis=1, keepdims=True)
    cls_ref[0] = _col_to_row(cls).astype(jnp.int32)
    rank_ref[0] = _col_to_row(rank).astype(jnp.int32)
    cnt_ref[0] = jnp.sum(onehot.astype(F32), axis=0, keepdims=True).astype(jnp.int32)


def _merge(x2, ys, ya, gates, wus, wua, wout, g2, wr, br, *, tm):
    t, d = x2.shape
    nt = t // tm
    tri = jnp.tril(jnp.ones((tm, tm), F32), -1).astype(BF16)
    return pl.pallas_call(
        _merge_kernel,
        grid=(nt,),
        in_specs=[
            pl.BlockSpec((tm, d), lambda i: (i, 0)),
            pl.BlockSpec((tm, ys.shape[1]), lambda i: (i, 0)),
            pl.BlockSpec((tm, ya.shape[1]), lambda i: (i, 0)),
            pl.BlockSpec((tm, gates.shape[1]), lambda i: (i, 0)),
            _const_spec(wus.shape), _const_spec(wua.shape), _const_spec(wout.shape),
            _const_spec(g2.shape), _const_spec(wr.shape), _const_spec(br.shape), _const_spec(tri.shape),
        ],
        out_specs=[
            pl.BlockSpec((tm, d + ROUTER_LANES), lambda i: (i, 0)),
            pl.BlockSpec((1, 1, tm), lambda i: (i, 0, 0)),
            pl.BlockSpec((1, 1, tm), lambda i: (i, 0, 0)),
            pl.BlockSpec((1, 1, ROUTER_LANES), lambda i: (i, 0, 0)),
        ],
        out_shape=[
            jax.ShapeDtypeStruct((t, d + ROUTER_LANES), F32),
            jax.ShapeDtypeStruct((nt, 1, tm), jnp.int32),
            jax.ShapeDtypeStruct((nt, 1, tm), jnp.int32),
            jax.ShapeDtypeStruct((nt, 1, ROUTER_LANES), jnp.int32),
        ],
        compiler_params=pltpu.CompilerParams(
            dimension_semantics=("arbitrary",), vmem_limit_bytes=VMEM_LIMIT),
    )(x2, ys, ya, gates, wus, wua, wout, g2, wr, br, tri)


def _permute_kernel(pos_ref, src_ref, dst_ref, sem, *, rows, scatter):
    def copy(r):
        p = pos_ref[0, 0, r]
        if scatter:
            return pltpu.make_async_copy(src_ref.at[pl.ds(r, 1)], dst_ref.at[pl.ds(p, 1)], sem)
        return pltpu.make_async_copy(src_ref.at[pl.ds(p, 1)], dst_ref.at[pl.ds(r, 1)], sem)

    def issue(r, carry):
        copy(r).start()
        return carry

    lax.fori_loop(0, rows, issue, 0, unroll=8)

    def drain(r, carry):
        copy(r).wait()
        return carry

    lax.fori_loop(0, rows, drain, 0, unroll=8)


def _permute_rows(pos, src, *, scatter):
    nt, _, rows = pos.shape
    width = src.shape[1]
    kern = functools.partial(_permute_kernel, rows=rows, scatter=scatter)
    tile = pl.BlockSpec((rows, width), lambda i: (i, 0))
    hbm = pl.BlockSpec(memory_space=pl.ANY)
    return pl.pallas_call(
        kern,
        grid=(nt,),
        in_specs=[
            pl.BlockSpec((1, 1, rows), lambda i: (i, 0, 0), memory_space=pltpu.SMEM),
            tile if scatter else hbm,
        ],
        out_specs=hbm if scatter else tile,
        out_shape=jax.ShapeDtypeStruct(src.shape, src.dtype),
        scratch_shapes=[pltpu.SemaphoreType.DMA(())],
        compiler_params=pltpu.CompilerParams(dimension_semantics=("arbitrary",)),
    )(pos, src)


def _expert(xn, w1_ref, w3_ref, w2_ref):
    h1 = jnp.dot(xn, w1_ref[0], preferred_element_type=F32)
    h3 = jnp.dot(xn, w3_ref[0], preferred_element_type=F32)
    hid = (h1 * _sigmoid(h1) * h3).astype(BF16)
    return jnp.dot(hid, w2_ref[0], preferred_element_type=F32)


def _moe_kernel(tile_ref, ea_ref, eb_ref, lo_ref, hi_ref, first_ref, last_ref,
                x_ref, w1a_ref, w3a_ref, w2a_ref, w1b_ref, w3b_ref, w2b_ref, g2_ref, gf_ref,
                y_ref, acc_ref):
    w = pl.program_id(0)
    d = y_ref.shape[1]
    lo = lo_ref[w]
    hi = hi_ref[w]

    @pl.when(hi > lo)
    def _():
        x1 = x_ref[:, :d]
        row = lax.broadcasted_iota(jnp.int32, (x1.shape[0], 1), 0)
        inseg = (row >= lo) & (row < hi)
        ca = jnp.where(inseg, x_ref[:, d:d + 1], 0.0)
        cb = jnp.where(inseg, x_ref[:, d + 1:d + 2], 0.0)
        xn = _rms(x1, g2_ref[...]).astype(BF16)
        contrib = (ca * _expert(xn, w1a_ref, w3a_ref, w2a_ref)
                   + cb * _expert(xn, w1b_ref, w3b_ref, w2b_ref))

        @pl.when(first_ref[w] == 1)
        def _():
            acc_ref[...] = contrib

        @pl.when(first_ref[w] == 0)
        def _():
            acc_ref[...] += contrib

        @pl.when(last_ref[w] == 1)
        def _():
            y_ref[...] = _rms(x1 + acc_ref[...], gf_ref[...])


def _moe(items, x1s, w1, w3, w2, g2, gf, *, rows):
    t, de_ = x1s.shape
    d = g2.shape[1]
    ne, _, de = w1.shape
    n_items = items[0].shape[0]

    def wa_map(w, tile, ea, eb, lo, hi, first, last):
        return (ea[w], 0, 0)

    def wb_map(w, tile, ea, eb, lo, hi, first, last):
        return (eb[w], 0, 0)

    def row_map(w, tile, ea, eb, lo, hi, first, last):
        return (tile[w], 0)

    def const_map(w, *_):
        return (0, 0)

    grid_spec = pltpu.PrefetchScalarGridSpec(
        num_scalar_prefetch=len(items),
        grid=(n_items,),
        in_specs=[
            pl.BlockSpec((rows, de_), row_map),
            pl.BlockSpec((1, d, de), wa_map), pl.BlockSpec((1, d, de), wa_map), pl.BlockSpec((1, de, d), wa_map),
            pl.BlockSpec((1, d, de), wb_map), pl.BlockSpec((1, d, de), wb_map), pl.BlockSpec((1, de, d), wb_map),
            pl.BlockSpec(g2.shape, const_map), pl.BlockSpec(gf.shape, const_map),
        ],
        out_specs=pl.BlockSpec((rows, d), row_map),
        scratch_shapes=[pltpu.VMEM((rows, d), F32)],
    )
    return pl.pallas_call(
        _moe_kernel,
        grid_spec=grid_spec,
        out_shape=jax.ShapeDtypeStruct((t, d), F32),
        compiler_params=pltpu.CompilerParams(
            dimension_semantics=("arbitrary",), vmem_limit_bytes=VMEM_LIMIT),
    )(*items, x1s, w1, w3, w2, w1, w3, w2, g2, gf)


def _pair_table():
    lo, hi = [], []
    for a in range(EXPERTS_PER_GROUP):
        for b in range(a + 1, EXPERTS_PER_GROUP):
            lo.append(a)
            hi.append(b)
    return jnp.asarray(lo, jnp.int32), jnp.asarray(hi, jnp.int32)


def _before_sum(v):
    ix = jnp.arange(v.shape[0])
    return jnp.sum(jnp.where(ix[None, :] < ix[:, None], v[None, :], 0), axis=1)


def _place(dest, vals, n):
    return jnp.sum(jnp.where(dest[None, :] == jnp.arange(n)[:, None], vals[None, :], 0), axis=1)


def _routing_tables(cls, rank, cnt, *, rows):
    nt, _, tm = cls.shape
    t = nt * tm
    lanes = cnt.shape[2]
    c = cnt[:, 0, :]
    tot = jnp.sum(c, axis=0)
    class_off = _before_sum(tot)
    tile_ix = jnp.arange(nt)
    tiles_before = jnp.sum(jnp.where((tile_ix[None, :] < tile_ix[:, None])[:, :, None], c[None], 0), axis=1)
    base = class_off[None, :] + tiles_before
    onehot = cls[:, 0, :, None] == jnp.arange(lanes)[None, None, :]
    pos = jnp.sum(jnp.where(onehot, base[:, None, :], 0), axis=2) + rank[:, 0, :]

    n_tiles = t // rows
    ta = jnp.arange(n_tiles, dtype=jnp.int32) * rows
    cb = class_off[1:N_CLASSES]
    n_items = n_tiles + N_CLASSES - 1
    idx = jnp.arange(n_items)
    at_a = jnp.arange(n_tiles) + jnp.sum(cb[None, :] <= ta[:, None], axis=1)
    at_b = jnp.arange(N_CLASSES - 1) + jnp.sum(ta[None, :] < cb[:, None], axis=1)
    starts = _place(at_a, ta, n_items) + _place(at_b, cb, n_items)
    ends = jnp.concatenate([starts[1:], jnp.full((1,), t, jnp.int32)])
    real = ends > starts
    n_real = jnp.sum(real)
    dest = jnp.where(real, _before_sum(real.astype(jnp.int32)), n_real + _before_sum(1 - real.astype(jnp.int32)))
    starts, ends = _place(dest, starts, n_items), _place(dest, ends, n_items)
    valid = idx < n_real
    last_real = jnp.maximum(n_real - 1, 0)
    starts = jnp.where(valid, starts, jnp.sum(jnp.where(idx == last_real, starts, 0)))
    tile = jnp.minimum(starts // rows, n_tiles - 1)
    klass = jnp.sum(class_off[None, :N_CLASSES] <= starts[:, None], axis=1) - 1
    pair_lo, pair_hi = _pair_table()
    grp = klass // PAIRS_PER_GROUP
    pair_hot = (klass % PAIRS_PER_GROUP)[:, None] == jnp.arange(PAIRS_PER_GROUP)[None, :]
    ea = grp * EXPERTS_PER_GROUP + jnp.sum(jnp.where(pair_hot, pair_lo[None, :], 0), axis=1)
    eb = grp * EXPERTS_PER_GROUP + jnp.sum(jnp.where(pair_hot, pair_hi[None, :], 0), axis=1)
    lo = jnp.where(valid, starts - tile * rows, 0)
    hi = jnp.where(valid, ends - tile * rows, 0)
    tile_prev = jnp.concatenate([tile[:1], tile[:-1]])
    tile_next = jnp.concatenate([tile[1:], tile[-1:]])
    first = valid & ((idx == 0) | (tile != tile_prev))
    last = valid & ((idx == last_real) | (tile != tile_next))
    items = tuple(a.astype(jnp.int32) for a in (tile, ea, eb, lo, hi, first, last))
    return pos.reshape(nt, 1, tm).astype(jnp.int32), items


def _t5_bucket(dist):
    n = jnp.maximum(dist, 0)
    max_exact = NUM_BUCKETS // 2
    large = max_exact + (jnp.log(jnp.maximum(n, max_exact).astype(F32) / max_exact)
                         / math.log(MAX_DISTANCE / max_exact)
                         * (NUM_BUCKETS - max_exact)).astype(jnp.int32)
    return jnp.where(n < max_exact, n, jnp.minimum(large, NUM_BUCKETS - 1))


def _bias_tables(rel_bias):
    kk = jnp.arange(MOBA_BLOCK)[:, None]
    qq = jnp.arange(MOBA_BLOCK)[None, :]
    rb = rel_bias.astype(F32) * LOG2E
    d_own = qq - kk

    def lookup(dist):
        oh = jax.nn.one_hot(_t5_bucket(dist), NUM_BUCKETS, dtype=F32)
        return jnp.einsum('hn,kqn->hkq', rb, oh, precision=lax.Precision.HIGHEST)

    own = jnp.where((d_own >= 0)[None], lookup(d_own), NEG_INF)
    prv = lookup(d_own + MOBA_BLOCK)
    tab = jnp.concatenate([prv, own], axis=1)
    far = rb[:, NUM_BUCKETS - 1]
    return tab.reshape(rb.shape[0] // 2, 2, 2 * MOBA_BLOCK, MOBA_BLOCK), far


def _ssm_params(lam_re, lam_im, log_step, b_re, b_im, c_re, c_im):
    g, p = lam_re.shape
    step = jnp.exp(log_step)[:, None]
    decay = jnp.exp(lam_re * step)
    a_re = decay * jnp.cos(lam_im * step)
    a_im = decay * jnp.sin(lam_im * step)
    denom = lam_re * lam_re + lam_im * lam_im
    nr, ni = a_re - 1.0, a_im
    coef_re = (nr * lam_re + ni * lam_im) / denom
    coef_im = (ni * lam_re - nr * lam_im) / denom
    bb_re = coef_re[..., None] * b_re - coef_im[..., None] * b_im
    bb_im = coef_re[..., None] * b_im + coef_im[..., None] * b_re
    eye = jnp.eye(g, dtype=F32)
    hc = b_re.shape[2]

    def in_mat(m):
        return (eye[:, None, :, None] * m.transpose(0, 2, 1)[:, :, None, :]).reshape(g * hc, g * p)

    def out_mat(m):
        return (eye[:, None, :, None] * m.transpose(0, 2, 1)[:, :, None, :]).reshape(g * p, g * hc)

    bb = jnp.concatenate([in_mat(bb_re), in_mat(bb_im)], axis=1).astype(BF16)
    cc = jnp.concatenate([out_mat(c_re), -out_mat(c_im)], axis=0).astype(BF16)
    return bb, cc, a_re.reshape(1, g * p), a_im.reshape(1, g * p)


def kernel(x, ln1_g, w_in, b_gate, ssm_lambda_re, ssm_lambda_im, ssm_log_step, ssm_b_re, ssm_b_im,
           ssm_c_re, ssm_c_im, ssm_d, w_glu, b_glu, w_up_ssm, w_up_attn, rel_bias, w_out, ln2_g,
           w_router_group, b_router_group, w_router_expert, b_router_expert, w1, w3, w2, ln_f_g):
    assert w_in.shape[0] == 1, "single-layer block"
    batch, seq, d = x.shape
    t = batch * seq
    d_ssm = w_glu.shape[1]
    d_attn = w_up_attn.shape[1]
    n_heads = d_attn // HEAD_DIM
    o1, o2, o3, o4 = d_ssm, d_ssm + d_attn, d_ssm + 2 * d_attn, d_ssm + 3 * d_attn
    x2 = x.reshape(t, d)

    wl = w_in[0]
    w_main = jnp.concatenate([wl[:, :o1], wl[:, o2:o3], wl[:, o4:]], axis=1).astype(BF16)
    wqT = (wl[:, o1:o2] * (HEAD_DIM ** -0.5 * LOG2E)).T.astype(BF16)
    wv = wl[:, o3:o4].T.reshape(n_heads, HEAD_DIM, d)
    wvT = jnp.concatenate([wv, jnp.zeros((n_heads, BF16_ROWS, d), F32)], axis=1)
    wvT = wvT.reshape(n_heads * V_ROWS, d).astype(BF16)
    vb = jnp.concatenate([jnp.zeros((n_heads, HEAD_DIM, 1), F32), jnp.ones((n_heads, BF16_ROWS, 1), F32)],
                         axis=1).reshape(n_heads * V_ROWS, 1)
    u, k3, kmean, qT, vT3, gates = _inproj(
        x2, ln1_g[0][None], w_main, wqT, wvT, vb, b_gate[0][None], batch=batch, seq=seq, tm=512)

    bb, cc, ar, ai = _ssm_params(ssm_lambda_re[0], ssm_lambda_im[0], ssm_log_step[0],
                                 ssm_b_re[0], ssm_b_im[0], ssm_c_re[0], ssm_c_im[0])
    y_ssm = _s5(u.reshape(batch, seq, d_ssm), bb, cc, ar, ai, ssm_d[0].reshape(1, d_ssm),
                w_glu[0].astype(BF16), b_glu[0][None], ts=128, chunk=512)

    bias, far = _bias_tables(rel_bias)
    y_attn = _moba(far, qT, k3, vT3, kmean.reshape(batch, seq // MOBA_BLOCK, d_attn), bias,
                   batch=batch, seq=seq)

    wr = jnp.zeros((d, ROUTER_LANES), F32)
    wr = wr.at[:, :N_GROUPS].set(w_router_group[0])
    wr = wr.at[:, EXPERT_LANE0:EXPERT_LANE0 + N_EXPERTS].set(w_router_expert[0]).astype(BF16)
    br = jnp.zeros((1, ROUTER_LANES), F32)
    br = br.at[0, :N_GROUPS].set(b_router_group[0])
    br = br.at[0, EXPERT_LANE0:EXPERT_LANE0 + N_EXPERTS].set(b_router_expert[0])
    x1e, cls, rank, cnt = _merge(
        x2, y_ssm.reshape(t, d_ssm), y_attn.reshape(t, d_attn), gates,
        w_up_ssm[0].astype(BF16), w_up_attn[0].astype(BF16), w_out[0].astype(BF16),
        ln2_g[0][None], wr, br, tm=512)

    pos, items = _routing_tables(cls, rank, cnt, rows=MOE_ROWS)
    x1s = _permute_rows(pos, x1e, scatter=True)
    ys = _moe(items, x1s, w1[0].astype(BF16), w3[0].astype(BF16), w2[0].astype(BF16),
              ln2_g[0][None], ln_f_g[None], rows=MOE_ROWS)
    y = _permute_rows(pos, ys, scatter=False)
    return y.reshape(batch, seq, d)
```

```python
import functools
import math

import jax
import jax.numpy as jnp
from jax import lax
from jax.experimental import pallas as pl
from jax.experimental.pallas import tpu as pltpu

F32 = jnp.float32
BF16 = jnp.bfloat16

SSM_GROUP = 16
SSM_STATE = 64
HEAD_DIM = 64
MOBA_BLOCK = 256
MOBA_TOP_K = 3
NUM_BUCKETS = 32
MAX_DISTANCE = 128
N_GROUPS = 4
EXPERTS_PER_GROUP = 8
N_EXPERTS = N_GROUPS * EXPERTS_PER_GROUP
PAIRS_PER_GROUP = EXPERTS_PER_GROUP * (EXPERTS_PER_GROUP - 1) // 2
N_CLASSES = N_GROUPS * PAIRS_PER_GROUP
MOE_ROWS = 256
SSM_PARTS = 2
MOBA_HEADS_PER_STEP = 4
RMS_EPS = 1e-6
NEG_INF = -1e30
LOG2E = 1.4426950408889634

LANES = 128
BF16_ROWS = 16
V_ROWS = HEAD_DIM + BF16_ROWS
ROUTER_LANES = LANES
EXPERT_LANE0 = N_GROUPS
VMEM_LIMIT = 56 * 1024 * 1024


def _sigmoid(x):
    return 1.0 / (1.0 + jnp.exp(-x))


def _rms(x, g):
    ms = jnp.mean(x * x, axis=-1, keepdims=True)
    return x * lax.rsqrt(ms + RMS_EPS) * g


def _const_spec(shape):
    n = len(shape)
    return pl.BlockSpec(shape, lambda *_: (0,) * n)


def _inproj_kernel(x_ref, g_ref, w_ref, wqT_ref, wvT_ref, vb_ref, bg_ref,
                   u_ref, k_ref, kmean_ref, qT_ref, vT_ref, gates_ref, *, nblk, d_ssm, d_attn):
    hn = _rms(x_ref[...], g_ref[...]).astype(BF16)
    acc = jnp.dot(hn, w_ref[...], preferred_element_type=F32)
    u_ref[...] = acc[:, :d_ssm]
    kf = acc[:, d_ssm:d_ssm + d_attn].reshape(nblk, MOBA_BLOCK, d_attn)
    k_ref[...] = kf.astype(BF16)
    kmean_ref[...] = jnp.mean(kf, axis=1, keepdims=True)
    gates_ref[...] = (acc[:, d_ssm + d_attn:] + bg_ref[...]).astype(BF16)
    nt = (((1,), (1,)), ((), ()))
    qT = lax.dot_general(wqT_ref[...], hn, nt, preferred_element_type=F32)
    qT_ref[0] = qT.astype(BF16)
    vT = lax.dot_general(wvT_ref[...], hn, nt, preferred_element_type=F32) + vb_ref[...]
    for j in range(nblk):
        vT_ref[j] = vT[:, j * MOBA_BLOCK:(j + 1) * MOBA_BLOCK].astype(BF16)


def _inproj(x2, g1, w_main, wqT, wvT, vb, bg, *, batch, seq, tm):
    t, d = x2.shape
    d_attn = wqT.shape[0]
    v_rows = wvT.shape[0]
    d_ssm = w_main.shape[1] - d_attn - bg.shape[1]
    nblk = tm // MOBA_BLOCK
    tiles_per_seq = seq // tm
    nb_total = t // MOBA_BLOCK
    kern = functools.partial(_inproj_kernel, nblk=nblk, d_ssm=d_ssm, d_attn=d_attn)
    return pl.pallas_call(
        kern,
        grid=(t // tm,),
        in_specs=[
            pl.BlockSpec((tm, d), lambda i: (i, 0)),
            _const_spec(g1.shape),
            _const_spec(w_main.shape),
            _const_spec(wqT.shape),
            _const_spec(wvT.shape),
            _const_spec(vb.shape),
            _const_spec(bg.shape),
        ],
        out_specs=[
            pl.BlockSpec((tm, d_ssm), lambda i: (i, 0)),
            pl.BlockSpec((nblk, MOBA_BLOCK, d_attn), lambda i: (i, 0, 0)),
            pl.BlockSpec((nblk, 1, d_attn), lambda i: (i, 0, 0)),
            pl.BlockSpec((1, d_attn, tm), lambda i: (i // tiles_per_seq, 0, i % tiles_per_seq)),
            pl.BlockSpec((nblk, v_rows, MOBA_BLOCK), lambda i: (i, 0, 0)),
            pl.BlockSpec((tm, bg.shape[1]), lambda i: (i, 0)),
        ],
        out_shape=[
            jax.ShapeDtypeStruct((t, d_ssm), F32),
            jax.ShapeDtypeStruct((nb_total, MOBA_BLOCK, d_attn), BF16),
            jax.ShapeDtypeStruct((nb_total, 1, d_attn), F32),
            jax.ShapeDtypeStruct((batch, d_attn, seq), BF16),
            jax.ShapeDtypeStruct((nb_total, v_rows, MOBA_BLOCK), BF16),
            jax.ShapeDtypeStruct((t, bg.shape[1]), BF16),
        ],
        compiler_params=pltpu.CompilerParams(
            dimension_semantics=("arbitrary",), vmem_limit_bytes=VMEM_LIMIT),
    )(x2, g1, w_main, wqT, wvT, vb, bg)


def _s5_kernel(u_ref, bb_ref, cc_ref, ar_ref, ai_ref, d_ref, wglu_ref, bglu_ref,
               y_ref, utb_ref, sbuf_ref, state_ref, ytb_ref, *, batch, ts, n_state, chunk):
    d_ssm = u_ref.shape[2]
    n_slab = d_ssm // LANES

    @pl.when(pl.program_id(0) == 0)
    def _():
        state_ref[...] = jnp.zeros_like(state_ref)

    for b in range(batch):
        ub = u_ref[b]
        for j in range(n_slab):
            utb_ref[j, pl.ds(b, ts, stride=batch), :] = ub[:, j * LANES:(j + 1) * LANES]
    u_tb = jnp.concatenate([utb_ref[j] for j in range(n_slab)], axis=1)
    n_part, pc, pw = bb_ref.shape
    ps = pw // 2
    u_bf = u_tb.astype(BF16)
    for h in range(n_part):
        sbuf_ref[:, h * pw:(h + 1) * pw] = jnp.dot(u_bf[:, h * pc:(h + 1) * pc], bb_ref[h],
                                                   preferred_element_type=F32)

    for c in range(n_state // chunk):
        h, cc_ = divmod(c * chunk, ps)
        re = slice(h * pw + cc_, h * pw + cc_ + chunk)
        im = slice(h * pw + ps + cc_, h * pw + ps + cc_ + chunk)
        ar = jnp.broadcast_to(ar_ref[:, c * chunk:(c + 1) * chunk], (batch, chunk))
        ai = jnp.broadcast_to(ai_ref[:, c * chunk:(c + 1) * chunk], (batch, chunk))

        def step(t, carry, re=re, im=im, ar=ar, ai=ai):
            hr, hi = carry
            r0 = pl.multiple_of(t * batch, batch)
            nhr = ar * hr - ai * hi + sbuf_ref[pl.ds(r0, batch), re]
            nhi = ar * hi + ai * hr + sbuf_ref[pl.ds(r0, batch), im]
            sbuf_ref[pl.ds(r0, batch), re] = nhr
            sbuf_ref[pl.ds(r0, batch), im] = nhi
            return nhr, nhi

        hr, hi = lax.fori_loop(0, ts, step, (state_ref[:, re], state_ref[:, im]), unroll=8)
        state_ref[:, re] = hr
        state_ref[:, im] = hi

    y = jnp.concatenate(
        [jnp.dot(sbuf_ref[:, h * pw:(h + 1) * pw].astype(BF16), cc_ref[h], preferred_element_type=F32)
         for h in range(n_part)], axis=1)
    y = y + d_ref[...] * u_tb
    g = jax.nn.gelu(y)
    z = jnp.dot(g.astype(BF16), wglu_ref[...], preferred_element_type=F32) + bglu_ref[...]
    out = g * _sigmoid(z)
    for j in range(n_slab):
        ytb_ref[j] = out[:, j * LANES:(j + 1) * LANES]
    for b in range(batch):
        y_ref[b] = jnp.concatenate(
            [ytb_ref[j, pl.ds(b, ts, stride=batch), :] for j in range(n_slab)], axis=1).astype(BF16)


def _s5(u3, bb, cc, ar, ai, dvec, wglu, bglu, *, ts, chunk):
    batch, seq, d_ssm = u3.shape
    n_state = ar.shape[1]
    kern = functools.partial(_s5_kernel, batch=batch, ts=ts, n_state=n_state, chunk=chunk)
    return pl.pallas_call(
        kern,
        grid=(seq // ts,),
        in_specs=[
            pl.BlockSpec((batch, ts, d_ssm), lambda i: (0, i, 0)),
            _const_spec(bb.shape), _const_spec(cc.shape), _const_spec(ar.shape), _const_spec(ai.shape),
            _const_spec(dvec.shape), _const_spec(wglu.shape), _const_spec(bglu.shape),
        ],
        out_specs=pl.BlockSpec((batch, ts, d_ssm), lambda i: (0, i, 0)),
        out_shape=jax.ShapeDtypeStruct((batch, seq, d_ssm), BF16),
        scratch_shapes=[
            pltpu.VMEM((d_ssm // LANES, ts * batch, LANES), F32),
            pltpu.VMEM((ts * batch, 2 * n_state), F32),
            pltpu.VMEM((batch, 2 * n_state), F32),
            pltpu.VMEM((d_ssm // LANES, ts * batch, LANES), F32),
        ],
        compiler_params=pltpu.CompilerParams(
            dimension_semantics=("arbitrary",), vmem_limit_bytes=VMEM_LIMIT),
    )(u3, bb, cc, ar, ai, dvec, wglu, bglu)


def _select_blocks(gate, blk_f, n_pick):
    sel = jnp.zeros(gate.shape, jnp.bool_)
    for _ in range(n_pick):
        mx = jnp.max(gate, axis=0, keepdims=True)
        idx = jnp.min(jnp.where(gate == mx, blk_f, float(gate.shape[0])), axis=0, keepdims=True)
        hit = blk_f == idx
        sel = sel | hit
        gate = jnp.where(hit, -jnp.inf, gate)
    return sel


def _moba_kernel(farb_ref, qT_ref, k_ref, vT_ref, kmean_ref, bias_ref, o_ref,
                 mask_ref, s_ref, p0_ref, p1_ref, acc_ref, *, nb):
    hg = pl.program_id(1)
    i = pl.program_id(2)
    tq = qT_ref.shape[2]
    hw = qT_ref.shape[1]
    nh = hw // HEAD_DIM
    qT = qT_ref[0]
    row = lax.broadcasted_iota(jnp.int32, qT.shape, 0)
    km = kmean_ref[0]
    km_hi = km.astype(BF16)
    km_lo = (km - km_hi.astype(F32)).astype(BF16)
    blk = lax.broadcasted_iota(jnp.int32, (nb, tq), 0)
    blk_f = blk.astype(F32)
    prev = jnp.maximum(i - 1, 0)
    n_far = jnp.maximum(i - 1, 0)
    n_iter = (n_far + 3) // 4

    qTs, near_masks = [], []
    for j in range(nh):
        qTh = jnp.where((row >= j * HEAD_DIM) & (row < (j + 1) * HEAD_DIM), qT, jnp.zeros_like(qT))
        qTs.append(qTh)
        gate = (jnp.dot(km_hi, qTh, preferred_element_type=F32)
                + jnp.dot(km_lo, qTh, preferred_element_type=F32))
        gate = jnp.where(blk < i, gate, NEG_INF)
        sel = _select_blocks(gate, blk_f, MOBA_TOP_K) & (blk < i)
        mask_ref[j] = jnp.where(sel & (blk < i - 1), farb_ref[nh * hg + j], NEG_INF)
        mprev = jnp.max(jnp.where(sel & (blk == i - 1), 0.0, NEG_INF), axis=0, keepdims=True)
        near_masks.append((mprev, jnp.zeros_like(mprev)))

    def far_keys(n0):
        n0 = pl.multiple_of(n0, 2)
        return k_ref[pl.ds(n0, 2)].reshape(2 * MOBA_BLOCK, hw)

    def far_masks(j, n0):
        return mask_ref[j, pl.ds(n0, 1), :], mask_ref[j, pl.ds(n0 + 1, 1), :]

    def block_max(s, masks):
        return jnp.maximum(jnp.max(s[:MOBA_BLOCK], axis=0, keepdims=True) + masks[0],
                           jnp.max(s[MOBA_BLOCK:], axis=0, keepdims=True) + masks[1])

    def probs(j, p_ref, s, shift, masks):
        p_ref[j, :MOBA_BLOCK] = jnp.exp2((s[:MOBA_BLOCK] - (shift - masks[0])).astype(BF16))
        p_ref[j, MOBA_BLOCK:] = jnp.exp2((s[MOBA_BLOCK:] - (shift - masks[1])).astype(BF16))

    def pv(j, p_ref, va, vb):
        vrows = slice(j * V_ROWS, (j + 1) * V_ROWS)
        vv = jnp.concatenate([va[vrows, :], vb[vrows, :]], axis=1)
        acc_ref[j] += jnp.dot(vv, p_ref[j], preferred_element_type=F32)

    near_keys = jnp.concatenate([k_ref[prev], k_ref[i]], axis=0)
    near_max = []
    for j in range(nh):
        s = jnp.dot(near_keys, qTs[j], preferred_element_type=F32) + bias_ref[j]
        s_ref[j] = s
        near_max.append(block_max(s, near_masks[j]))

    def attend(shift):
        acc_ref[...] = jnp.zeros_like(acc_ref)
        first = far_keys(0)
        for j in range(nh):
            probs(j, p0_ref, s_ref[j], shift[j], near_masks[j])
            probs(j, p1_ref, jnp.dot(first, qTs[j], preferred_element_type=F32), shift[j], far_masks(j, 0))

        def trip(g, carry):
            f0 = 4 * g
            n2 = f0 + 2
            n3 = jnp.minimum(f0 + 4, nb - 2)
            va0 = vT_ref[jnp.where(g == 0, prev, f0 - 2)]
            vb0 = vT_ref[jnp.where(g == 0, i, f0 - 1)]
            va1 = vT_ref[f0]
            vb1 = vT_ref[f0 + 1]
            k2 = far_keys(n2)
            k3_ = far_keys(n3)
            for j in range(nh):
                pv(j, p0_ref, va0, vb0)
                probs(j, p0_ref, jnp.dot(k2, qTs[j], preferred_element_type=F32), shift[j], far_masks(j, n2))
                pv(j, p1_ref, va1, vb1)
                probs(j, p1_ref, jnp.dot(k3_, qTs[j], preferred_element_type=F32), shift[j], far_masks(j, n3))
            return carry

        lax.fori_loop(0, n_iter, trip, 0)
        last = 4 * n_iter
        va = vT_ref[jnp.where(n_iter == 0, prev, last - 2)]
        vb = vT_ref[jnp.where(n_iter == 0, i, last - 1)]
        for j in range(nh):
            pv(j, p0_ref, va, vb)

    attend(near_max)
    acc = acc_ref[...]
    overflow = jnp.max(jnp.where(jnp.isfinite(acc), 0.0, 1.0)) > 0.0

    @pl.when(overflow)
    def _():
        def far_max(n, m_run):
            kp = k_ref[n]
            return tuple(jnp.maximum(m_run[j], jnp.max(jnp.dot(kp, qTs[j], preferred_element_type=F32),
                                                       axis=0, keepdims=True) + mask_ref[j, pl.ds(n, 1), :])
                         for j in range(nh))

        attend(lax.fori_loop(0, n_far, far_max, tuple(near_max)))

    outs = [acc_ref[j][:HEAD_DIM] / acc_ref[j][HEAD_DIM:HEAD_DIM + 1] for j in range(nh)]
    o_ref[0] = jnp.concatenate(outs, axis=0).T.astype(BF16)


def _moba(farb, qT, k3, vT3, kmean, bias, *, batch, seq):
    nb = seq // MOBA_BLOCK
    assert nb % 4 == 0, "far blocks are consumed four per loop trip"
    d_attn = qT.shape[1]
    nh = MOBA_HEADS_PER_STEP
    hw = nh * HEAD_DIM
    kern = functools.partial(_moba_kernel, nb=nb)
    return pl.pallas_call(
        kern,
        grid=(batch, d_attn // hw, nb),
        in_specs=[
            pl.BlockSpec(memory_space=pltpu.SMEM),
            pl.BlockSpec((1, hw, MOBA_BLOCK), lambda b, h, i: (b, h, i)),
            pl.BlockSpec((nb, MOBA_BLOCK, hw), lambda b, h, i: (b, 0, h)),
            pl.BlockSpec((nb, nh * V_ROWS, MOBA_BLOCK), lambda b, h, i: (b, h, 0)),
            pl.BlockSpec((1, nb, hw), lambda b, h, i: (b, 0, h)),
            pl.BlockSpec((nh, 2 * MOBA_BLOCK, MOBA_BLOCK), lambda b, h, i: (h, 0, 0)),
        ],
        out_specs=pl.BlockSpec((1, MOBA_BLOCK, hw), lambda b, h, i: (b, i, h)),
        out_shape=jax.ShapeDtypeStruct((batch, seq, d_attn), BF16),
        scratch_shapes=[
            pltpu.VMEM((nh, nb, MOBA_BLOCK), F32),
            pltpu.VMEM((nh, 2 * MOBA_BLOCK, MOBA_BLOCK), F32),
            pltpu.VMEM((nh, 2 * MOBA_BLOCK, MOBA_BLOCK), BF16),
            pltpu.VMEM((nh, 2 * MOBA_BLOCK, MOBA_BLOCK), BF16),
            pltpu.VMEM((nh, V_ROWS, MOBA_BLOCK), F32),
        ],
        compiler_params=pltpu.CompilerParams(
            dimension_semantics=("arbitrary", "arbitrary", "arbitrary"),
            vmem_limit_bytes=VMEM_LIMIT),
    )(farb, qT, k3, vT3, kmean, bias)


def _route(logits):
    lane = lax.broadcasted_iota(jnp.int32, logits.shape, 1)
    lane_f = lane.astype(F32)
    big = float(ROUTER_LANES)
    ninf = -jnp.inf
    gmask = lane < N_GROUPS
    gmax = jnp.max(jnp.where(gmask, logits, ninf), axis=1, keepdims=True)
    gsum = jnp.sum(jnp.where(gmask, jnp.exp(logits - gmax), 0.0), axis=1, keepdims=True)
    g_val = 1.0 / gsum
    g_idx = jnp.min(jnp.where(gmask & (logits == gmax), lane_f, big), axis=1, keepdims=True)
    e_grp = ((lane - EXPERT_LANE0) >> 3).astype(F32)
    emask = (lane >= EXPERT_LANE0) & (lane < EXPERT_LANE0 + N_EXPERTS) & (e_grp == g_idx)
    el = jnp.where(emask, logits, ninf)
    e1 = jnp.max(el, axis=1, keepdims=True)
    i1 = jnp.min(jnp.where(emask & (el == e1), lane_f, big), axis=1, keepdims=True)
    el2 = jnp.where(lane_f == i1, ninf, el)
    e2 = jnp.max(el2, axis=1, keepdims=True)
    i2 = jnp.min(jnp.where(emask & (lane_f != i1) & (el2 == e2), lane_f, big), axis=1, keepdims=True)
    tt = jnp.exp(e2 - e1)
    w1 = g_val / (1.0 + tt)
    w2 = g_val * tt / (1.0 + tt)
    first_low = i1 < i2
    lo = jnp.minimum(i1, i2) - (EXPERT_LANE0 + EXPERTS_PER_GROUP * g_idx)
    hi = jnp.maximum(i1, i2) - (EXPERT_LANE0 + EXPERTS_PER_GROUP * g_idx)
    pair = lo * (2 * EXPERTS_PER_GROUP - 1 - lo) * 0.5 + (hi - lo - 1.0)
    cls = g_idx * PAIRS_PER_GROUP + pair
    return cls, jnp.where(first_low, w1, w2), jnp.where(first_low, w2, w1)


def _col_to_row(col):
    n = col.shape[0]
    r = lax.broadcasted_iota(jnp.int32, (n, n), 0)
    c = lax.broadcasted_iota(jnp.int32, (n, n), 1)
    return jnp.sum(jnp.where(r == c, col, 0.0), axis=0, keepdims=True)


def _merge_kernel(x_ref, ys_ref, ya_ref, gates_ref, wus_ref, wua_ref, wout_ref, g2_ref, wr_ref, br_ref,
                  tri_ref, x1e_ref, cls_ref, rank_ref, cnt_ref):
    d = x_ref.shape[1]
    a = jnp.dot(ys_ref[...], wus_ref[...], preferred_element_type=F32)
    b = jnp.dot(ya_ref[...], wua_ref[...], preferred_element_type=F32)
    gts = gates_ref[...].astype(F32)
    merged = _sigmoid(gts[:, :d]) * a + _sigmoid(gts[:, d:]) * b
    x1 = x_ref[...] + jnp.dot(merged.astype(BF16), wout_ref[...], preferred_element_type=F32)
    xn = _rms(x1, g2_ref[...]).astype(BF16)
    logits = jnp.dot(xn, wr_ref[...], preferred_element_type=F32) + br_ref[...]
    cls, wa, wb = _route(logits)
    lane = lax.broadcasted_iota(jnp.int32, logits.shape, 1)
    x1e_ref[:, :d] = x1
    x1e_ref[:, d:] = jnp.where(lane == 0, wa, jnp.where(lane == 1, wb, 0.0))
    onehot = lane.astype(F32) == cls
    before = jnp.dot(tri_ref[...], onehot.astype(BF16), preferred_element_type=F32)
    rank = jnp.sum(jnp.where(onehot, before, 0.0), axis=1, keepdims=True)
    cls_ref[0] = _col_to_row(cls).astype(jnp.int32)
    rank_ref[0] = _col_to_row(rank).astype(jnp.int32)
    cnt_ref[0] = jnp.sum(onehot.astype(F32), axis=0, keepdims=True).astype(jnp.int32)


def _merge(x2, ys, ya, gates, wus, wua, wout, g2, wr, br, *, tm):
    t, d = x2.shape
    nt = t // tm
    tri = jnp.tril(jnp.ones((tm, tm), F32), -1).astype(BF16)
    return pl.pallas_call(
        _merge_kernel,
        grid=(nt,),
        in_specs=[
            pl.BlockSpec((tm, d), lambda i: (i, 0)),
            pl.BlockSpec((tm, ys.shape[1]), lambda i: (i, 0)),
            pl.BlockSpec((tm, ya.shape[1]), lambda i: (i, 0)),
            pl.BlockSpec((tm, gates.shape[1]), lambda i: (i, 0)),
            _const_spec(wus.shape), _const_spec(wua.shape), _const_spec(wout.shape),
            _const_spec(g2.shape), _const_spec(wr.shape), _const_spec(br.shape), _const_spec(tri.shape),
        ],
        out_specs=[
            pl.BlockSpec((tm, d + ROUTER_LANES), lambda i: (i, 0)),
            pl.BlockSpec((1, 1, tm), lambda i: (i, 0, 0)),
            pl.BlockSpec((1, 1, tm), lambda i: (i, 0, 0)),
            pl.BlockSpec((1, 1, ROUTER_LANES), lambda i: (i, 0, 0)),
        ],
        out_shape=[
            jax.ShapeDtypeStruct((t, d + ROUTER_LANES), F32),
            jax.ShapeDtypeStruct((nt, 1, tm), jnp.int32),
            jax.ShapeDtypeStruct((nt, 1, tm), jnp.int32),
            jax.ShapeDtypeStruct((nt, 1, ROUTER_LANES), jnp.int32),
        ],
        compiler_params=pltpu.CompilerParams(
            dimension_semantics=("arbitrary",), vmem_limit_bytes=VMEM_LIMIT),
    )(x2, ys, ya, gates, wus, wua, wout, g2, wr, br, tri)


def _permute_kernel(pos_ref, src_ref, dst_ref, sem, *, rows, scatter):
    def copy(r):
        p = pos_ref[0, 0, r]
        if scatter:
            return pltpu.make_async_copy(src_ref.at[pl.ds(r, 1)], dst_ref.at[pl.ds(p, 1)], sem)
        return pltpu.make_async_copy(src_ref.at[pl.ds(p, 1)], dst_ref.at[pl.ds(r, 1)], sem)

    def issue(r, carry):
        copy(r).start()
        return carry

    lax.fori_loop(0, rows, issue, 0, unroll=8)

    def drain(r, carry):
        copy(r).wait()
        return carry

    lax.fori_loop(0, rows, drain, 0, unroll=8)


def _permute_rows(pos, src, *, scatter):
    nt, _, rows = pos.shape
    width = src.shape[1]
    kern = functools.partial(_permute_kernel, rows=rows, scatter=scatter)
    tile = pl.BlockSpec((rows, width), lambda i: (i, 0))
    hbm = pl.BlockSpec(memory_space=pl.ANY)
    return pl.pallas_call(
        kern,
        grid=(nt,),
        in_specs=[
            pl.BlockSpec((1, 1, rows), lambda i: (i, 0, 0), memory_space=pltpu.SMEM),
            tile if scatter else hbm,
        ],
        out_specs=hbm if scatter else tile,
        out_shape=jax.ShapeDtypeStruct(src.shape, src.dtype),
        scratch_shapes=[pltpu.SemaphoreType.DMA(())],
        compiler_params=pltpu.CompilerParams(dimension_semantics=("arbitrary",)),
    )(pos, src)


def _expert(xn, w1_ref, w3_ref, w2_ref):
    h1 = jnp.dot(xn, w1_ref[0], preferred_element_type=F32)
    h3 = jnp.dot(xn, w3_ref[0], preferred_element_type=F32)
    hid = (h1 * _sigmoid(h1) * h3).astype(BF16)
    return jnp.dot(hid, w2_ref[0], preferred_element_type=F32)


def _moe_kernel(tile_ref, ea_ref, eb_ref, lo_ref, hi_ref, first_ref, last_ref,
                x_ref, w1a_ref, w3a_ref, w2a_ref, w1b_ref, w3b_ref, w2b_ref, g2_ref, gf_ref,
                y_ref, acc_ref):
    w = pl.program_id(0)
    d = y_ref.shape[1]
    lo = lo_ref[w]
    hi = hi_ref[w]

    @pl.when(hi > lo)
    def _():
        x1 = x_ref[:, :d]
        row = lax.broadcasted_iota(jnp.int32, (x1.shape[0], 1), 0)
        inseg = (row >= lo) & (row < hi)
        ca = jnp.where(inseg, x_ref[:, d:d + 1], 0.0)
        cb = jnp.where(inseg, x_ref[:, d + 1:d + 2], 0.0)
        xn = _rms(x1, g2_ref[...]).astype(BF16)
        contrib = (ca * _expert(xn, w1a_ref, w3a_ref, w2a_ref)
                   + cb * _expert(xn, w1b_ref, w3b_ref, w2b_ref))

        @pl.when(first_ref[w] == 1)
        def _():
            acc_ref[...] = contrib

        @pl.when(first_ref[w] == 0)
        def _():
            acc_ref[...] += contrib

        @pl.when(last_ref[w] == 1)
        def _():
            y_ref[...] = _rms(x1 + acc_ref[...], gf_ref[...])


def _moe(items, x1s, w1, w3, w2, g2, gf, *, rows):
    t, de_ = x1s.shape
    d = g2.shape[1]
    ne, _, de = w1.shape
    n_items = items[0].shape[0]

    def wa_map(w, tile, ea, eb, lo, hi, first, last):
        return (ea[w], 0, 0)

    def wb_map(w, tile, ea, eb, lo, hi, first, last):
        return (eb[w], 0, 0)

    def row_map(w, tile, ea, eb, lo, hi, first, last):
        return (tile[w], 0)

    def const_map(w, *_):
        return (0, 0)

    grid_spec = pltpu.PrefetchScalarGridSpec(
        num_scalar_prefetch=len(items),
        grid=(n_items,),
        in_specs=[
            pl.BlockSpec((rows, de_), row_map),
            pl.BlockSpec((1, d, de), wa_map), pl.BlockSpec((1, d, de), wa_map), pl.BlockSpec((1, de, d), wa_map),
            pl.BlockSpec((1, d, de), wb_map), pl.BlockSpec((1, d, de), wb_map), pl.BlockSpec((1, de, d), wb_map),
            pl.BlockSpec(g2.shape, const_map), pl.BlockSpec(gf.shape, const_map),
        ],
        out_specs=pl.BlockSpec((rows, d), row_map),
        scratch_shapes=[pltpu.VMEM((rows, d), F32)],
    )
    return pl.pallas_call(
        _moe_kernel,
        grid_spec=grid_spec,
        out_shape=jax.ShapeDtypeStruct((t, d), F32),
        compiler_params=pltpu.CompilerParams(
            dimension_semantics=("arbitrary",), vmem_limit_bytes=VMEM_LIMIT),
    )(*items, x1s, w1, w3, w2, w1, w3, w2, g2, gf)


def _pair_table():
    lo, hi = [], []
    for a in range(EXPERTS_PER_GROUP):
        for b in range(a + 1, EXPERTS_PER_GROUP):
            lo.append(a)
            hi.append(b)
    return jnp.asarray(lo, jnp.int32), jnp.asarray(hi, jnp.int32)


def _before_sum(v):
    ix = jnp.arange(v.shape[0])
    return jnp.sum(jnp.where(ix[None, :] < ix[:, None], v[None, :], 0), axis=1)


def _place(dest, vals, n):
    return jnp.sum(jnp.where(dest[None, :] == jnp.arange(n)[:, None], vals[None, :], 0), axis=1)


def _routing_tables(cls, rank, cnt, *, rows):
    nt, _, tm = cls.shape
    t = nt * tm
    lanes = cnt.shape[2]
    c = cnt[:, 0, :]
    tot = jnp.sum(c, axis=0)
    class_off = _before_sum(tot)
    tile_ix = jnp.arange(nt)
    tiles_before = jnp.sum(jnp.where((tile_ix[None, :] < tile_ix[:, None])[:, :, None], c[None], 0), axis=1)
    base = class_off[None, :] + tiles_before
    onehot = cls[:, 0, :, None] == jnp.arange(lanes)[None, None, :]
    pos = jnp.sum(jnp.where(onehot, base[:, None, :], 0), axis=2) + rank[:, 0, :]

    n_tiles = t // rows
    ta = jnp.arange(n_tiles, dtype=jnp.int32) * rows
    cb = class_off[1:N_CLASSES]
    n_items = n_tiles + N_CLASSES - 1
    idx = jnp.arange(n_items)
    at_a = jnp.arange(n_tiles) + jnp.sum(cb[None, :] <= ta[:, None], axis=1)
    at_b = jnp.arange(N_CLASSES - 1) + jnp.sum(ta[None, :] < cb[:, None], axis=1)
    starts = _place(at_a, ta, n_items) + _place(at_b, cb, n_items)
    ends = jnp.concatenate([starts[1:], jnp.full((1,), t, jnp.int32)])
    real = ends > starts
    n_real = jnp.sum(real)
    dest = jnp.where(real, _before_sum(real.astype(jnp.int32)), n_real + _before_sum(1 - real.astype(jnp.int32)))
    starts, ends = _place(dest, starts, n_items), _place(dest, ends, n_items)
    valid = idx < n_real
    last_real = jnp.maximum(n_real - 1, 0)
    starts = jnp.where(valid, starts, jnp.sum(jnp.where(idx == last_real, starts, 0)))
    tile = jnp.minimum(starts // rows, n_tiles - 1)
    klass = jnp.sum(class_off[None, :N_CLASSES] <= starts[:, None], axis=1) - 1
    pair_lo, pair_hi = _pair_table()
    grp = klass // PAIRS_PER_GROUP
    pair_hot = (klass % PAIRS_PER_GROUP)[:, None] == jnp.arange(PAIRS_PER_GROUP)[None, :]
    ea = grp * EXPERTS_PER_GROUP + jnp.sum(jnp.where(pair_hot, pair_lo[None, :], 0), axis=1)
    eb = grp * EXPERTS_PER_GROUP + jnp.sum(jnp.where(pair_hot, pair_hi[None, :], 0), axis=1)
    lo = jnp.where(valid, starts - tile * rows, 0)
    hi = jnp.where(valid, ends - tile * rows, 0)
    tile_prev = jnp.concatenate([tile[:1], tile[:-1]])
    tile_next = jnp.concatenate([tile[1:], tile[-1:]])
    first = valid & ((idx == 0) | (tile != tile_prev))
    last = valid & ((idx == last_real) | (tile != tile_next))
    items = tuple(a.astype(jnp.int32) for a in (tile, ea, eb, lo, hi, first, last))
    return pos.reshape(nt, 1, tm).astype(jnp.int32), items


def _t5_bucket(dist):
    n = jnp.maximum(dist, 0)
    max_exact = NUM_BUCKETS // 2
    large = max_exact + (jnp.log(jnp.maximum(n, max_exact).astype(F32) / max_exact)
                         / math.log(MAX_DISTANCE / max_exact)
                         * (NUM_BUCKETS - max_exact)).astype(jnp.int32)
    return jnp.where(n < max_exact, n, jnp.minimum(large, NUM_BUCKETS - 1))


def _bias_tables(rel_bias):
    kk = jnp.arange(MOBA_BLOCK)[:, None]
    qq = jnp.arange(MOBA_BLOCK)[None, :]
    rb = rel_bias.astype(F32) * LOG2E
    d_own = qq - kk

    def lookup(dist):
        oh = jax.nn.one_hot(_t5_bucket(dist), NUM_BUCKETS, dtype=F32)
        return jnp.einsum('hn,kqn->hkq', rb, oh, precision=lax.Precision.HIGHEST)

    own = jnp.where((d_own >= 0)[None], lookup(d_own), NEG_INF)
    prv = lookup(d_own + MOBA_BLOCK)
    tab = jnp.concatenate([prv, own], axis=1)
    far = rb[:, NUM_BUCKETS - 1]
    return tab, far


def _ssm_params(lam_re, lam_im, log_step, b_re, b_im, c_re, c_im):
    g, p = lam_re.shape
    step = jnp.exp(log_step)[:, None]
    decay = jnp.exp(lam_re * step)
    a_re = decay * jnp.cos(lam_im * step)
    a_im = decay * jnp.sin(lam_im * step)
    denom = lam_re * lam_re + lam_im * lam_im
    nr, ni = a_re - 1.0, a_im
    coef_re = (nr * lam_re + ni * lam_im) / denom
    coef_im = (ni * lam_re - nr * lam_im) / denom
    bb_re = coef_re[..., None] * b_re - coef_im[..., None] * b_im
    bb_im = coef_re[..., None] * b_im + coef_im[..., None] * b_re
    eye = jnp.eye(g, dtype=F32)
    hc = b_re.shape[2]

    def in_mat(m):
        return (eye[:, None, :, None] * m.transpose(0, 2, 1)[:, :, None, :]).reshape(g * hc, g * p)

    def out_mat(m):
        return (eye[:, None, :, None] * m.transpose(0, 2, 1)[:, :, None, :]).reshape(g * p, g * hc)

    pc, ps = g * hc // SSM_PARTS, g * p // SSM_PARTS
    in_re, in_im, out_re, out_im = in_mat(bb_re), in_mat(bb_im), out_mat(c_re), out_mat(c_im)
    bb = jnp.stack([jnp.concatenate([m[h * pc:(h + 1) * pc, h * ps:(h + 1) * ps] for m in (in_re, in_im)], axis=1)
                    for h in range(SSM_PARTS)]).astype(BF16)
    cc = jnp.stack([jnp.concatenate([m[h * ps:(h + 1) * ps, h * pc:(h + 1) * pc] for m in (out_re, -out_im)], axis=0)
                    for h in range(SSM_PARTS)]).astype(BF16)
    return bb, cc, a_re.reshape(1, g * p), a_im.reshape(1, g * p)


def kernel(x, ln1_g, w_in, b_gate, ssm_lambda_re, ssm_lambda_im, ssm_log_step, ssm_b_re, ssm_b_im,
           ssm_c_re, ssm_c_im, ssm_d, w_glu, b_glu, w_up_ssm, w_up_attn, rel_bias, w_out, ln2_g,
           w_router_group, b_router_group, w_router_expert, b_router_expert, w1, w3, w2, ln_f_g):
    assert w_in.shape[0] == 1, "single-layer block"
    batch, seq, d = x.shape
    t = batch * seq
    d_ssm = w_glu.shape[1]
    d_attn = w_up_attn.shape[1]
    n_heads = d_attn // HEAD_DIM
    o1, o2, o3, o4 = d_ssm, d_ssm + d_attn, d_ssm + 2 * d_attn, d_ssm + 3 * d_attn
    x2 = x.reshape(t, d)

    wl = w_in[0]
    w_main = jnp.concatenate([wl[:, :o1], wl[:, o2:o3], wl[:, o4:]], axis=1).astype(BF16)
    wqT = (wl[:, o1:o2] * (HEAD_DIM ** -0.5 * LOG2E)).T.astype(BF16)
    wv = wl[:, o3:o4].T.reshape(n_heads, HEAD_DIM, d)
    wvT = jnp.concatenate([wv, jnp.zeros((n_heads, BF16_ROWS, d), F32)], axis=1)
    wvT = wvT.reshape(n_heads * V_ROWS, d).astype(BF16)
    vb = jnp.concatenate([jnp.zeros((n_heads, HEAD_DIM, 1), F32), jnp.ones((n_heads, BF16_ROWS, 1), F32)],
                         axis=1).reshape(n_heads * V_ROWS, 1)
    u, k3, kmean, qT, vT3, gates = _inproj(
        x2, ln1_g[0][None], w_main, wqT, wvT, vb, b_gate[0][None], batch=batch, seq=seq, tm=512)

    bb, cc, ar, ai = _ssm_params(ssm_lambda_re[0], ssm_lambda_im[0], ssm_log_step[0],
                                 ssm_b_re[0], ssm_b_im[0], ssm_c_re[0], ssm_c_im[0])
    y_ssm = _s5(u.reshape(batch, seq, d_ssm), bb, cc, ar, ai, ssm_d[0].reshape(1, d_ssm),
                w_glu[0].astype(BF16), b_glu[0][None], ts=128, chunk=512)

    bias, far = _bias_tables(rel_bias)
    y_attn = _moba(far, qT, k3, vT3, kmean.reshape(batch, seq // MOBA_BLOCK, d_attn), bias,
                   batch=batch, seq=seq)

    wr = jnp.zeros((d, ROUTER_LANES), F32)
    wr = wr.at[:, :N_GROUPS].set(w_router_group[0])
    wr = wr.at[:, EXPERT_LANE0:EXPERT_LANE0 + N_EXPERTS].set(w_router_expert[0]).astype(BF16)
    br = jnp.zeros((1, ROUTER_LANES), F32)
    br = br.at[0, :N_GROUPS].set(b_router_group[0])
    br = br.at[0, EXPERT_LANE0:EXPERT_LANE0 + N_EXPERTS].set(b_router_expert[0])
    x1e, cls, rank, cnt = _merge(
        x2, y_ssm.reshape(t, d_ssm), y_attn.reshape(t, d_attn), gates,
        w_up_ssm[0].astype(BF16), w_up_attn[0].astype(BF16), w_out[0].astype(BF16),
        ln2_g[0][None], wr, br, tm=512)

    pos, items = _routing_tables(cls, rank, cnt, rows=MOE_ROWS)
    x1s = _permute_rows(pos, x1e, scatter=True)
    ys = _moe(items, x1s, w1[0].astype(BF16), w3[0].astype(BF16), w2[0].astype(BF16),
              ln2_g[0][None], ln_f_g[None], rows=MOE_ROWS)
    y = _permute_rows(pos, ys, scatter=False)
    return y.reshape(batch, seq, d)
```

```python
import functools
import math

import jax
import jax.numpy as jnp
from jax import lax
from jax.experimental import pallas as pl
from jax.experimental.pallas import tpu as pltpu

F32 = jnp.float32
BF16 = jnp.bfloat16

SSM_GROUP = 16
SSM_STATE = 64
HEAD_DIM = 64
MOBA_BLOCK = 256
MOBA_TOP_K = 3
NUM_BUCKETS = 32
MAX_DISTANCE = 128
N_GROUPS = 4
EXPERTS_PER_GROUP = 8
N_EXPERTS = N_GROUPS * EXPERTS_PER_GROUP
PAIRS_PER_GROUP = EXPERTS_PER_GROUP * (EXPERTS_PER_GROUP - 1) // 2
N_CLASSES = N_GROUPS * PAIRS_PER_GROUP
MOE_ROWS = 256
SSM_PARTS = 2
MOBA_HEADS_PER_STEP = 8
RMS_EPS = 1e-6
NEG_INF = -1e30
LOG2E = 1.4426950408889634

LANES = 128
MXU_DEPTH = 256
BF16_ROWS = 16
V_ROWS = HEAD_DIM + BF16_ROWS
ROUTER_LANES = LANES
EXPERT_LANE0 = N_GROUPS
VMEM_LIMIT = 56 * 1024 * 1024


def _sigmoid(x):
    return 1.0 / (1.0 + jnp.exp(-x))


def _rms(x, g):
    ms = jnp.mean(x * x, axis=-1, keepdims=True)
    return x * lax.rsqrt(ms + RMS_EPS) * g


def _const_spec(shape):
    n = len(shape)
    return pl.BlockSpec(shape, lambda *_: (0,) * n)


def _inproj_kernel(x_ref, g_ref, w_ref, wqT_ref, wvT_ref, vb_ref, bg_ref,
                   u_ref, k_ref, kmean_ref, qT_ref, vT_ref, gates_ref, *, nblk, d_ssm, d_attn):
    hn = _rms(x_ref[...], g_ref[...]).astype(BF16)
    acc = jnp.dot(hn, w_ref[...], preferred_element_type=F32)
    u_ref[...] = acc[:, :d_ssm]
    kf = acc[:, d_ssm:d_ssm + d_attn].reshape(nblk, MOBA_BLOCK, d_attn)
    k_ref[...] = kf.astype(BF16)
    kmean_ref[...] = jnp.mean(kf, axis=1, keepdims=True)
    gates_ref[...] = (acc[:, d_ssm + d_attn:] + bg_ref[...]).astype(BF16)
    nt = (((1,), (1,)), ((), ()))
    qT = lax.dot_general(wqT_ref[...], hn, nt, preferred_element_type=F32)
    qT_ref[0] = qT.astype(BF16)
    vT = lax.dot_general(wvT_ref[...], hn, nt, preferred_element_type=F32) + vb_ref[...]
    for j in range(nblk):
        vT_ref[j] = vT[:, j * MOBA_BLOCK:(j + 1) * MOBA_BLOCK].astype(BF16)


def _inproj(x2, g1, w_main, wqT, wvT, vb, bg, *, batch, seq, tm):
    t, d = x2.shape
    d_attn = wqT.shape[0]
    v_rows = wvT.shape[0]
    d_ssm = w_main.shape[1] - d_attn - bg.shape[1]
    nblk = tm // MOBA_BLOCK
    tiles_per_seq = seq // tm
    nb_total = t // MOBA_BLOCK
    kern = functools.partial(_inproj_kernel, nblk=nblk, d_ssm=d_ssm, d_attn=d_attn)
    return pl.pallas_call(
        kern,
        grid=(t // tm,),
        in_specs=[
            pl.BlockSpec((tm, d), lambda i: (i, 0)),
            _const_spec(g1.shape),
            _const_spec(w_main.shape),
            _const_spec(wqT.shape),
            _const_spec(wvT.shape),
            _const_spec(vb.shape),
            _const_spec(bg.shape),
        ],
        out_specs=[
            pl.BlockSpec((tm, d_ssm), lambda i: (i, 0)),
            pl.BlockSpec((nblk, MOBA_BLOCK, d_attn), lambda i: (i, 0, 0)),
            pl.BlockSpec((nblk, 1, d_attn), lambda i: (i, 0, 0)),
            pl.BlockSpec((1, d_attn, tm), lambda i: (i // tiles_per_seq, 0, i % tiles_per_seq)),
            pl.BlockSpec((nblk, v_rows, MOBA_BLOCK), lambda i: (i, 0, 0)),
            pl.BlockSpec((tm, bg.shape[1]), lambda i: (i, 0)),
        ],
        out_shape=[
            jax.ShapeDtypeStruct((t, d_ssm), F32),
            jax.ShapeDtypeStruct((nb_total, MOBA_BLOCK, d_attn), BF16),
            jax.ShapeDtypeStruct((nb_total, 1, d_attn), F32),
            jax.ShapeDtypeStruct((batch, d_attn, seq), BF16),
            jax.ShapeDtypeStruct((nb_total, v_rows, MOBA_BLOCK), BF16),
            jax.ShapeDtypeStruct((t, bg.shape[1]), BF16),
        ],
        compiler_params=pltpu.CompilerParams(
            dimension_semantics=("arbitrary",), vmem_limit_bytes=VMEM_LIMIT),
    )(x2, g1, w_main, wqT, wvT, vb, bg)


def _s5_kernel(u_ref, bb_ref, cc_ref, ar_ref, ai_ref, d_ref, wglu_ref, bglu_ref,
               y_ref, utb_ref, sbuf_ref, state_ref, ytb_ref, *, batch, ts, n_state, chunk):
    d_ssm = u_ref.shape[2]
    n_slab = d_ssm // LANES

    @pl.when(pl.program_id(0) == 0)
    def _():
        state_ref[...] = jnp.zeros_like(state_ref)

    for b in range(batch):
        ub = u_ref[b]
        for j in range(n_slab):
            utb_ref[j, pl.ds(b, ts, stride=batch), :] = ub[:, j * LANES:(j + 1) * LANES]
    u_tb = jnp.concatenate([utb_ref[j] for j in range(n_slab)], axis=1)
    n_part, pc, pw = bb_ref.shape
    ps = pw // 2
    u_bf = u_tb.astype(BF16)
    for h in range(n_part):
        sbuf_ref[:, h * pw:(h + 1) * pw] = jnp.dot(u_bf[:, h * pc:(h + 1) * pc], bb_ref[h],
                                                   preferred_element_type=F32)

    for c in range(n_state // chunk):
        h, cc_ = divmod(c * chunk, ps)
        re = slice(h * pw + cc_, h * pw + cc_ + chunk)
        im = slice(h * pw + ps + cc_, h * pw + ps + cc_ + chunk)
        ar = jnp.broadcast_to(ar_ref[:, c * chunk:(c + 1) * chunk], (batch, chunk))
        ai = jnp.broadcast_to(ai_ref[:, c * chunk:(c + 1) * chunk], (batch, chunk))

        def step(t, carry, re=re, im=im, ar=ar, ai=ai):
            hr, hi = carry
            r0 = pl.multiple_of(t * batch, batch)
            nhr = ar * hr - ai * hi + sbuf_ref[pl.ds(r0, batch), re]
            nhi = ar * hi + ai * hr + sbuf_ref[pl.ds(r0, batch), im]
            sbuf_ref[pl.ds(r0, batch), re] = nhr
            sbuf_ref[pl.ds(r0, batch), im] = nhi
            return nhr, nhi

        hr, hi = lax.fori_loop(0, ts, step, (state_ref[:, re], state_ref[:, im]), unroll=8)
        state_ref[:, re] = hr
        state_ref[:, im] = hi

    y = jnp.concatenate(
        [jnp.dot(sbuf_ref[:, h * pw:(h + 1) * pw].astype(BF16), cc_ref[h], preferred_element_type=F32)
         for h in range(n_part)], axis=1)
    y = y + d_ref[...] * u_tb
    g = jax.nn.gelu(y)
    z = jnp.dot(g.astype(BF16), wglu_ref[...], preferred_element_type=F32) + bglu_ref[...]
    out = g * _sigmoid(z)
    for j in range(n_slab):
        ytb_ref[j] = out[:, j * LANES:(j + 1) * LANES]
    for b in range(batch):
        y_ref[b] = jnp.concatenate(
            [ytb_ref[j, pl.ds(b, ts, stride=batch), :] for j in range(n_slab)], axis=1).astype(BF16)


def _s5(u3, bb, cc, ar, ai, dvec, wglu, bglu, *, ts, chunk):
    batch, seq, d_ssm = u3.shape
    n_state = ar.shape[1]
    kern = functools.partial(_s5_kernel, batch=batch, ts=ts, n_state=n_state, chunk=chunk)
    return pl.pallas_call(
        kern,
        grid=(seq // ts,),
        in_specs=[
            pl.BlockSpec((batch, ts, d_ssm), lambda i: (0, i, 0)),
            _const_spec(bb.shape), _const_spec(cc.shape), _const_spec(ar.shape), _const_spec(ai.shape),
            _const_spec(dvec.shape), _const_spec(wglu.shape), _const_spec(bglu.shape),
        ],
        out_specs=pl.BlockSpec((batch, ts, d_ssm), lambda i: (0, i, 0)),
        out_shape=jax.ShapeDtypeStruct((batch, seq, d_ssm), BF16),
        scratch_shapes=[
            pltpu.VMEM((d_ssm // LANES, ts * batch, LANES), F32),
            pltpu.VMEM((ts * batch, 2 * n_state), F32),
            pltpu.VMEM((batch, 2 * n_state), F32),
            pltpu.VMEM((d_ssm // LANES, ts * batch, LANES), F32),
        ],
        compiler_params=pltpu.CompilerParams(
            dimension_semantics=("arbitrary",), vmem_limit_bytes=VMEM_LIMIT),
    )(u3, bb, cc, ar, ai, dvec, wglu, bglu)


def _select_blocks(gate, blk_f, n_pick):
    sel = jnp.zeros(gate.shape, jnp.bool_)
    for _ in range(n_pick):
        mx = jnp.max(gate, axis=0, keepdims=True)
        idx = jnp.min(jnp.where(gate == mx, blk_f, float(gate.shape[0])), axis=0, keepdims=True)
        hit = blk_f == idx
        sel = sel | hit
        gate = jnp.where(hit, -jnp.inf, gate)
    return sel


def _moba_kernel(farb_ref, qT_ref, k_ref, vT_ref, kmean_ref, bias_ref, o_ref,
                 mask_ref, s_ref, p0_ref, p1_ref, acc_ref, *, nb):
    hg = pl.program_id(1)
    i = pl.program_id(2)
    tq = qT_ref.shape[2]
    hw = qT_ref.shape[1]
    nh = hw // HEAD_DIM
    qT = qT_ref[0]
    row = lax.broadcasted_iota(jnp.int32, qT.shape, 0)
    km = kmean_ref[0]
    km_hi = km.astype(BF16)
    km_lo = (km - km_hi.astype(F32)).astype(BF16)
    blk = lax.broadcasted_iota(jnp.int32, (nb, tq), 0)
    blk_f = blk.astype(F32)
    prev = jnp.maximum(i - 1, 0)
    n_far = jnp.maximum(i - 1, 0)
    n_iter = (n_far + 3) // 4

    gw = min(hw, MXU_DEPTH)
    groups = [slice(j * HEAD_DIM // gw * gw, j * HEAD_DIM // gw * gw + gw) for j in range(nh)]

    def qk(j, keys):
        return jnp.dot(keys[:, groups[j]], qTs[j], preferred_element_type=F32)

    qTs, near_masks = [], []
    for j in range(nh):
        qTh = jnp.where((row >= j * HEAD_DIM) & (row < (j + 1) * HEAD_DIM), qT, jnp.zeros_like(qT))
        qTs.append(qTh[groups[j]])
        gate = qk(j, km_hi) + qk(j, km_lo)
        gate = jnp.where(blk < i, gate, NEG_INF)
        sel = _select_blocks(gate, blk_f, MOBA_TOP_K) & (blk < i)
        mask_ref[j] = jnp.where(sel & (blk < i - 1), farb_ref[nh * hg + j], NEG_INF)
        mprev = jnp.max(jnp.where(sel & (blk == i - 1), 0.0, NEG_INF), axis=0, keepdims=True)
        near_masks.append((mprev, jnp.zeros_like(mprev)))

    def far_keys(n0):
        n0 = pl.multiple_of(n0, 2)
        return k_ref[pl.ds(n0, 2)].reshape(2 * MOBA_BLOCK, hw)

    def far_masks(j, n0):
        return mask_ref[j, pl.ds(n0, 1), :], mask_ref[j, pl.ds(n0 + 1, 1), :]

    def block_max(s, masks):
        return jnp.maximum(jnp.max(s[:MOBA_BLOCK], axis=0, keepdims=True) + masks[0],
                           jnp.max(s[MOBA_BLOCK:], axis=0, keepdims=True) + masks[1])

    def probs(j, p_ref, s, shift, masks):
        p_ref[j, :MOBA_BLOCK] = jnp.exp2((s[:MOBA_BLOCK] - (shift - masks[0])).astype(BF16))
        p_ref[j, MOBA_BLOCK:] = jnp.exp2((s[MOBA_BLOCK:] - (shift - masks[1])).astype(BF16))

    def pv(j, p_ref, va, vb):
        vrows = slice(j * V_ROWS, (j + 1) * V_ROWS)
        vv = jnp.concatenate([va[vrows, :], vb[vrows, :]], axis=1)
        acc_ref[j] += jnp.dot(vv, p_ref[j], preferred_element_type=F32)

    near_keys = jnp.concatenate([k_ref[prev], k_ref[i]], axis=0)
    near_max = []
    for j in range(nh):
        s = qk(j, near_keys) + bias_ref[j]
        s_ref[j] = s
        near_max.append(block_max(s, near_masks[j]))

    def attend(shift):
        acc_ref[...] = jnp.zeros_like(acc_ref)
        first = far_keys(0)
        for j in range(nh):
            probs(j, p0_ref, s_ref[j], shift[j], near_masks[j])
            probs(j, p1_ref, qk(j, first), shift[j], far_masks(j, 0))

        def trip(g, carry):
            f0 = 4 * g
            n2 = f0 + 2
            n3 = jnp.minimum(f0 + 4, nb - 2)
            va0 = vT_ref[jnp.where(g == 0, prev, f0 - 2)]
            vb0 = vT_ref[jnp.where(g == 0, i, f0 - 1)]
            va1 = vT_ref[f0]
            vb1 = vT_ref[f0 + 1]
            k2 = far_keys(n2)
            k3_ = far_keys(n3)
            for j in range(nh):
                pv(j, p0_ref, va0, vb0)
                probs(j, p0_ref, qk(j, k2), shift[j], far_masks(j, n2))
                pv(j, p1_ref, va1, vb1)
                probs(j, p1_ref, qk(j, k3_), shift[j], far_masks(j, n3))
            return carry

        lax.fori_loop(0, n_iter, trip, 0)
        last = 4 * n_iter
        va = vT_ref[jnp.where(n_iter == 0, prev, last - 2)]
        vb = vT_ref[jnp.where(n_iter == 0, i, last - 1)]
        for j in range(nh):
            pv(j, p0_ref, va, vb)

    attend(near_max)
    acc = acc_ref[...]
    overflow = jnp.max(jnp.where(jnp.isfinite(acc), 0.0, 1.0)) > 0.0

    @pl.when(overflow)
    def _():
        def far_max(n, m_run):
            kp = k_ref[n]
            return tuple(jnp.maximum(m_run[j], jnp.max(qk(j, kp), axis=0, keepdims=True) + mask_ref[j, pl.ds(n, 1), :])
                         for j in range(nh))

        attend(lax.fori_loop(0, n_far, far_max, tuple(near_max)))

    outs = [acc_ref[j][:HEAD_DIM] / acc_ref[j][HEAD_DIM:HEAD_DIM + 1] for j in range(nh)]
    o_ref[0] = jnp.concatenate(outs, axis=0).T.astype(BF16)


def _moba(farb, qT, k3, vT3, kmean, bias, *, batch, seq):
    nb = seq // MOBA_BLOCK
    assert nb % 4 == 0, "far blocks are consumed four per loop trip"
    d_attn = qT.shape[1]
    nh = MOBA_HEADS_PER_STEP
    hw = nh * HEAD_DIM
    kern = functools.partial(_moba_kernel, nb=nb)
    return pl.pallas_call(
        kern,
        grid=(batch, d_attn // hw, nb),
        in_specs=[
            pl.BlockSpec(memory_space=pltpu.SMEM),
            pl.BlockSpec((1, hw, MOBA_BLOCK), lambda b, h, i: (b, h, i)),
            pl.BlockSpec((nb, MOBA_BLOCK, hw), lambda b, h, i: (b, 0, h), pipeline_mode=pl.Buffered(1)),
            pl.BlockSpec((nb, nh * V_ROWS, MOBA_BLOCK), lambda b, h, i: (b, h, 0), pipeline_mode=pl.Buffered(1)),
            pl.BlockSpec((1, nb, hw), lambda b, h, i: (b, 0, h)),
            pl.BlockSpec((nh, 2 * MOBA_BLOCK, MOBA_BLOCK), lambda b, h, i: (h, 0, 0), pipeline_mode=pl.Buffered(1)),
        ],
        out_specs=pl.BlockSpec((1, MOBA_BLOCK, hw), lambda b, h, i: (b, i, h)),
        out_shape=jax.ShapeDtypeStruct((batch, seq, d_attn), BF16),
        scratch_shapes=[
            pltpu.VMEM((nh, nb, MOBA_BLOCK), F32),
            pltpu.VMEM((nh, 2 * MOBA_BLOCK, MOBA_BLOCK), F32),
            pltpu.VMEM((nh, 2 * MOBA_BLOCK, MOBA_BLOCK), BF16),
            pltpu.VMEM((nh, 2 * MOBA_BLOCK, MOBA_BLOCK), BF16),
            pltpu.VMEM((nh, V_ROWS, MOBA_BLOCK), F32),
        ],
        compiler_params=pltpu.CompilerParams(
            dimension_semantics=("arbitrary", "arbitrary", "arbitrary"),
            vmem_limit_bytes=VMEM_LIMIT),
    )(farb, qT, k3, vT3, kmean, bias)


def _route(logits):
    lane = lax.broadcasted_iota(jnp.int32, logits.shape, 1)
    lane_f = lane.astype(F32)
    big = float(ROUTER_LANES)
    ninf = -jnp.inf
    gmask = lane < N_GROUPS
    gmax = jnp.max(jnp.where(gmask, logits, ninf), axis=1, keepdims=True)
    gsum = jnp.sum(jnp.where(gmask, jnp.exp(logits - gmax), 0.0), axis=1, keepdims=True)
    g_val = 1.0 / gsum
    g_idx = jnp.min(jnp.where(gmask & (logits == gmax), lane_f, big), axis=1, keepdims=True)
    e_grp = ((lane - EXPERT_LANE0) >> 3).astype(F32)
    emask = (lane >= EXPERT_LANE0) & (lane < EXPERT_LANE0 + N_EXPERTS) & (e_grp == g_idx)
    el = jnp.where(emask, logits, ninf)
    e1 = jnp.max(el, axis=1, keepdims=True)
    i1 = jnp.min(jnp.where(emask & (el == e1), lane_f, big), axis=1, keepdims=True)
    el2 = jnp.where(lane_f == i1, ninf, el)
    e2 = jnp.max(el2, axis=1, keepdims=True)
    i2 = jnp.min(jnp.where(emask & (lane_f != i1) & (el2 == e2), lane_f, big), axis=1, keepdims=True)
    tt = jnp.exp(e2 - e1)
    w1 = g_val / (1.0 + tt)
    w2 = g_val * tt / (1.0 + tt)
    first_low = i1 < i2
    lo = jnp.minimum(i1, i2) - (EXPERT_LANE0 + EXPERTS_PER_GROUP * g_idx)
    hi = jnp.maximum(i1, i2) - (EXPERT_LANE0 + EXPERTS_PER_GROUP * g_idx)
    pair = lo * (2 * EXPERTS_PER_GROUP - 1 - lo) * 0.5 + (hi - lo - 1.0)
    cls = g_idx * PAIRS_PER_GROUP + pair
    return cls, jnp.where(first_low, w1, w2), jnp.where(first_low, w2, w1)


def _col_to_row(col):
    n = col.shape[0]
    r = lax.broadcasted_iota(jnp.int32, (n, n), 0)
    c = lax.broadcasted_iota(jnp.int32, (n, n), 1)
    return jnp.sum(jnp.where(r == c, col, 0.0), axis=0, keepdims=True)


def _merge_kernel(x_ref, ys_ref, ya_ref, gates_ref, wus_ref, wua_ref, wout_ref, g2_ref, wr_ref, br_ref,
                  tri_ref, x1e_ref, cls_ref, rank_ref, cnt_ref):
    d = x_ref.shape[1]
    a = jnp.dot(ys_ref[...], wus_ref[...], preferred_element_type=F32)
    b = jnp.dot(ya_ref[...], wua_ref[...], preferred_element_type=F32)
    gts = gates_ref[...].astype(F32)
    merged = _sigmoid(gts[:, :d]) * a + _sigmoid(gts[:, d:]) * b
    x1 = x_ref[...] + jnp.dot(merged.astype(BF16), wout_ref[...], preferred_element_type=F32)
    xn = _rms(x1, g2_ref[...]).astype(BF16)
    logits = jnp.dot(xn, wr_ref[...], preferred_element_type=F32) + br_ref[...]
    cls, wa, wb = _route(logits)
    lane = lax.broadcasted_iota(jnp.int32, logits.shape, 1)
    x1e_ref[:, :d] = x1
    x1e_ref[:, d:] = jnp.where(lane == 0, wa, jnp.where(lane == 1, wb, 0.0))
    onehot = lane.astype(F32) == cls
    before = jnp.dot(tri_ref[...], onehot.astype(BF16), preferred_element_type=F32)
    rank = jnp.sum(jnp.where(onehot, before, 0.0), axis=1, keepdims=True)
    cls_ref[0] = _col_to_row(cls).astype(jnp.int32)
    rank_ref[0] = _col_to_row(rank).astype(jnp.int32)
    cnt_ref[0] = jnp.sum(onehot.astype(F32), axis=0, keepdims=True).astype(jnp.int32)


def _merge(x2, ys, ya, gates, wus, wua, wout, g2, wr, br, *, tm):
    t, d = x2.shape
    nt = t // tm
    tri = jnp.tril(jnp.ones((tm, tm), F32), -1).astype(BF16)
    return pl.pallas_call(
        _merge_kernel,
        grid=(nt,),
        in_specs=[
            pl.BlockSpec((tm, d), lambda i: (i, 0)),
            pl.BlockSpec((tm, ys.shape[1]), lambda i: (i, 0)),
            pl.BlockSpec((tm, ya.shape[1]), lambda i: (i, 0)),
            pl.BlockSpec((tm, gates.shape[1]), lambda i: (i, 0)),
            _const_spec(wus.shape), _const_spec(wua.shape), _const_spec(wout.shape),
            _const_spec(g2.shape), _const_spec(wr.shape), _const_spec(br.shape), _const_spec(tri.shape),
        ],
        out_specs=[
            pl.BlockSpec((tm, d + ROUTER_LANES), lambda i: (i, 0)),
            pl.BlockSpec((1, 1, tm), lambda i: (i, 0, 0)),
            pl.BlockSpec((1, 1, tm), lambda i: (i, 0, 0)),
            pl.BlockSpec((1, 1, ROUTER_LANES), lambda i: (i, 0, 0)),
        ],
        out_shape=[
            jax.ShapeDtypeStruct((t, d + ROUTER_LANES), F32),
            jax.ShapeDtypeStruct((nt, 1, tm), jnp.int32),
            jax.ShapeDtypeStruct((nt, 1, tm), jnp.int32),
            jax.ShapeDtypeStruct((nt, 1, ROUTER_LANES), jnp.int32),
        ],
        compiler_params=pltpu.CompilerParams(
            dimension_semantics=("arbitrary",), vmem_limit_bytes=VMEM_LIMIT),
    )(x2, ys, ya, gates, wus, wua, wout, g2, wr, br, tri)


def _permute_kernel(pos_ref, src_ref, dst_ref, sem, *, rows, scatter):
    def copy(r):
        p = pos_ref[0, 0, r]
        if scatter:
            return pltpu.make_async_copy(src_ref.at[pl.ds(r, 1)], dst_ref.at[pl.ds(p, 1)], sem)
        return pltpu.make_async_copy(src_ref.at[pl.ds(p, 1)], dst_ref.at[pl.ds(r, 1)], sem)

    def issue(r, carry):
        copy(r).start()
        return carry

    lax.fori_loop(0, rows, issue, 0, unroll=8)

    def drain(r, carry):
        copy(r).wait()
        return carry

    lax.fori_loop(0, rows, drain, 0, unroll=8)


def _permute_rows(pos, src, *, scatter):
    nt, _, rows = pos.shape
    width = src.shape[1]
    kern = functools.partial(_permute_kernel, rows=rows, scatter=scatter)
    tile = pl.BlockSpec((rows, width), lambda i: (i, 0))
    hbm = pl.BlockSpec(memory_space=pl.ANY)
    return pl.pallas_call(
        kern,
        grid=(nt,),
        in_specs=[
            pl.BlockSpec((1, 1, rows), lambda i: (i, 0, 0), memory_space=pltpu.SMEM),
            tile if scatter else hbm,
        ],
        out_specs=hbm if scatter else tile,
        out_shape=jax.ShapeDtypeStruct(src.shape, src.dtype),
        scratch_shapes=[pltpu.SemaphoreType.DMA(())],
        compiler_params=pltpu.CompilerParams(dimension_semantics=("arbitrary",)),
    )(pos, src)


def _expert(xn, w1_ref, w3_ref, w2_ref):
    h1 = jnp.dot(xn, w1_ref[0], preferred_element_type=F32)
    h3 = jnp.dot(xn, w3_ref[0], preferred_element_type=F32)
    hid = (h1 * _sigmoid(h1) * h3).astype(BF16)
    return jnp.dot(hid, w2_ref[0], preferred_element_type=F32)


def _moe_kernel(tile_ref, ea_ref, eb_ref, lo_ref, hi_ref, first_ref, last_ref,
                x_ref, w1a_ref, w3a_ref, w2a_ref, w1b_ref, w3b_ref, w2b_ref, g2_ref, gf_ref,
                y_ref, acc_ref):
    w = pl.program_id(0)
    d = y_ref.shape[1]
    lo = lo_ref[w]
    hi = hi_ref[w]

    @pl.when(hi > lo)
    def _():
        x1 = x_ref[:, :d]
        row = lax.broadcasted_iota(jnp.int32, (x1.shape[0], 1), 0)
        inseg = (row >= lo) & (row < hi)
        ca = jnp.where(inseg, x_ref[:, d:d + 1], 0.0)
        cb = jnp.where(inseg, x_ref[:, d + 1:d + 2], 0.0)
        xn = _rms(x1, g2_ref[...]).astype(BF16)
        contrib = (ca * _expert(xn, w1a_ref, w3a_ref, w2a_ref)
                   + cb * _expert(xn, w1b_ref, w3b_ref, w2b_ref))

        @pl.when(first_ref[w] == 1)
        def _():
            acc_ref[...] = contrib

        @pl.when(first_ref[w] == 0)
        def _():
            acc_ref[...] += contrib

        @pl.when(last_ref[w] == 1)
        def _():
            y_ref[...] = _rms(x1 + acc_ref[...], gf_ref[...])


def _moe(items, x1s, w1, w3, w2, g2, gf, *, rows):
    t, de_ = x1s.shape
    d = g2.shape[1]
    ne, _, de = w1.shape
    n_items = items[0].shape[0]

    def wa_map(w, tile, ea, eb, lo, hi, first, last):
        return (ea[w], 0, 0)

    def wb_map(w, tile, ea, eb, lo, hi, first, last):
        return (eb[w], 0, 0)

    def row_map(w, tile, ea, eb, lo, hi, first, last):
        return (tile[w], 0)

    def const_map(w, *_):
        return (0, 0)

    grid_spec = pltpu.PrefetchScalarGridSpec(
        num_scalar_prefetch=len(items),
        grid=(n_items,),
        in_specs=[
            pl.BlockSpec((rows, de_), row_map),
            pl.BlockSpec((1, d, de), wa_map), pl.BlockSpec((1, d, de), wa_map), pl.BlockSpec((1, de, d), wa_map),
            pl.BlockSpec((1, d, de), wb_map), pl.BlockSpec((1, d, de), wb_map), pl.BlockSpec((1, de, d), wb_map),
            pl.BlockSpec(g2.shape, const_map), pl.BlockSpec(gf.shape, const_map),
        ],
        out_specs=pl.BlockSpec((rows, d), row_map),
        scratch_shapes=[pltpu.VMEM((rows, d), F32)],
    )
    return pl.pallas_call(
        _moe_kernel,
        grid_spec=grid_spec,
        out_shape=jax.ShapeDtypeStruct((t, d), F32),
        compiler_params=pltpu.CompilerParams(
            dimension_semantics=("arbitrary",), vmem_limit_bytes=VMEM_LIMIT),
    )(*items, x1s, w1, w3, w2, w1, w3, w2, g2, gf)


def _pair_table():
    lo, hi = [], []
    for a in range(EXPERTS_PER_GROUP):
        for b in range(a + 1, EXPERTS_PER_GROUP):
            lo.append(a)
            hi.append(b)
    return jnp.asarray(lo, jnp.int32), jnp.asarray(hi, jnp.int32)


def _before_sum(v):
    ix = jnp.arange(v.shape[0])
    return jnp.sum(jnp.where(ix[None, :] < ix[:, None], v[None, :], 0), axis=1)


def _place(dest, vals, n):
    return jnp.sum(jnp.where(dest[None, :] == jnp.arange(n)[:, None], vals[None, :], 0), axis=1)


def _routing_tables(cls, rank, cnt, *, rows):
    nt, _, tm = cls.shape
    t = nt * tm
    lanes = cnt.shape[2]
    c = cnt[:, 0, :]
    tot = jnp.sum(c, axis=0)
    class_off = _before_sum(tot)
    tile_ix = jnp.arange(nt)
    tiles_before = jnp.sum(jnp.where((tile_ix[None, :] < tile_ix[:, None])[:, :, None], c[None], 0), axis=1)
    base = class_off[None, :] + tiles_before
    onehot = cls[:, 0, :, None] == jnp.arange(lanes)[None, None, :]
    pos = jnp.sum(jnp.where(onehot, base[:, None, :], 0), axis=2) + rank[:, 0, :]

    n_tiles = t // rows
    ta = jnp.arange(n_tiles, dtype=jnp.int32) * rows
    cb = class_off[1:N_CLASSES]
    n_items = n_tiles + N_CLASSES - 1
    idx = jnp.arange(n_items)
    at_a = jnp.arange(n_tiles) + jnp.sum(cb[None, :] <= ta[:, None], axis=1)
    at_b = jnp.arange(N_CLASSES - 1) + jnp.sum(ta[None, :] < cb[:, None], axis=1)
    starts = _place(at_a, ta, n_items) + _place(at_b, cb, n_items)
    ends = jnp.concatenate([starts[1:], jnp.full((1,), t, jnp.int32)])
    real = ends > starts
    n_real = jnp.sum(real)
    dest = jnp.where(real, _before_sum(real.astype(jnp.int32)), n_real + _before_sum(1 - real.astype(jnp.int32)))
    starts, ends = _place(dest, starts, n_items), _place(dest, ends, n_items)
    valid = idx < n_real
    last_real = jnp.maximum(n_real - 1, 0)
    starts = jnp.where(valid, starts, jnp.sum(jnp.where(idx == last_real, starts, 0)))
    tile = jnp.minimum(starts // rows, n_tiles - 1)
    klass = jnp.sum(class_off[None, :N_CLASSES] <= starts[:, None], axis=1) - 1
    pair_lo, pair_hi = _pair_table()
    grp = klass // PAIRS_PER_GROUP
    pair_hot = (klass % PAIRS_PER_GROUP)[:, None] == jnp.arange(PAIRS_PER_GROUP)[None, :]
    ea = grp * EXPERTS_PER_GROUP + jnp.sum(jnp.where(pair_hot, pair_lo[None, :], 0), axis=1)
    eb = grp * EXPERTS_PER_GROUP + jnp.sum(jnp.where(pair_hot, pair_hi[None, :], 0), axis=1)
    lo = jnp.where(valid, starts - tile * rows, 0)
    hi = jnp.where(valid, ends - tile * rows, 0)
    tile_prev = jnp.concatenate([tile[:1], tile[:-1]])
    tile_next = jnp.concatenate([tile[1:], tile[-1:]])
    first = valid & ((idx == 0) | (tile != tile_prev))
    last = valid & ((idx == last_real) | (tile != tile_next))
    items = tuple(a.astype(jnp.int32) for a in (tile, ea, eb, lo, hi, first, last))
    return pos.reshape(nt, 1, tm).astype(jnp.int32), items


def _t5_bucket(dist):
    n = jnp.maximum(dist, 0)
    max_exact = NUM_BUCKETS // 2
    large = max_exact + (jnp.log(jnp.maximum(n, max_exact).astype(F32) / max_exact)
                         / math.log(MAX_DISTANCE / max_exact)
                         * (NUM_BUCKETS - max_exact)).astype(jnp.int32)
    return jnp.where(n < max_exact, n, jnp.minimum(large, NUM_BUCKETS - 1))


def _bias_tables(rel_bias):
    kk = jnp.arange(MOBA_BLOCK)[:, None]
    qq = jnp.arange(MOBA_BLOCK)[None, :]
    rb = rel_bias.astype(F32) * LOG2E
    d_own = qq - kk

    def lookup(dist):
        oh = jax.nn.one_hot(_t5_bucket(dist), NUM_BUCKETS, dtype=F32)
        return jnp.einsum('hn,kqn->hkq', rb, oh, precision=lax.Precision.HIGHEST)

    own = jnp.where((d_own >= 0)[None], lookup(d_own), NEG_INF)
    prv = lookup(d_own + MOBA_BLOCK)
    tab = jnp.concatenate([prv, own], axis=1)
    far = rb[:, NUM_BUCKETS - 1]
    return tab, far


def _ssm_params(lam_re, lam_im, log_step, b_re, b_im, c_re, c_im):
    g, p = lam_re.shape
    step = jnp.exp(log_step)[:, None]
    decay = jnp.exp(lam_re * step)
    a_re = decay * jnp.cos(lam_im * step)
    a_im = decay * jnp.sin(lam_im * step)
    denom = lam_re * lam_re + lam_im * lam_im
    nr, ni = a_re - 1.0, a_im
    coef_re = (nr * lam_re + ni * lam_im) / denom
    coef_im = (ni * lam_re - nr * lam_im) / denom
    bb_re = coef_re[..., None] * b_re - coef_im[..., None] * b_im
    bb_im = coef_re[..., None] * b_im + coef_im[..., None] * b_re
    eye = jnp.eye(g, dtype=F32)
    hc = b_re.shape[2]

    def in_mat(m):
        return (eye[:, None, :, None] * m.transpose(0, 2, 1)[:, :, None, :]).reshape(g * hc, g * p)

    def out_mat(m):
        return (eye[:, None, :, None] * m.transpose(0, 2, 1)[:, :, None, :]).reshape(g * p, g * hc)

    pc, ps = g * hc // SSM_PARTS, g * p // SSM_PARTS
    in_re, in_im, out_re, out_im = in_mat(bb_re), in_mat(bb_im), out_mat(c_re), out_mat(c_im)
    bb = jnp.stack([jnp.concatenate([m[h * pc:(h + 1) * pc, h * ps:(h + 1) * ps] for m in (in_re, in_im)], axis=1)
                    for h in range(SSM_PARTS)]).astype(BF16)
    cc = jnp.stack([jnp.concatenate([m[h * ps:(h + 1) * ps, h * pc:(h + 1) * pc] for m in (out_re, -out_im)], axis=0)
                    for h in range(SSM_PARTS)]).astype(BF16)
    return bb, cc, a_re.reshape(1, g * p), a_im.reshape(1, g * p)


def kernel(x, ln1_g, w_in, b_gate, ssm_lambda_re, ssm_lambda_im, ssm_log_step, ssm_b_re, ssm_b_im,
           ssm_c_re, ssm_c_im, ssm_d, w_glu, b_glu, w_up_ssm, w_up_attn, rel_bias, w_out, ln2_g,
           w_router_group, b_router_group, w_router_expert, b_router_expert, w1, w3, w2, ln_f_g):
    assert w_in.shape[0] == 1, "single-layer block"
    batch, seq, d = x.shape
    t = batch * seq
    d_ssm = w_glu.shape[1]
    d_attn = w_up_attn.shape[1]
    n_heads = d_attn // HEAD_DIM
    o1, o2, o3, o4 = d_ssm, d_ssm + d_attn, d_ssm + 2 * d_attn, d_ssm + 3 * d_attn
    x2 = x.reshape(t, d)

    wl = w_in[0]
    w_main = jnp.concatenate([wl[:, :o1], wl[:, o2:o3], wl[:, o4:]], axis=1).astype(BF16)
    wqT = (wl[:, o1:o2] * (HEAD_DIM ** -0.5 * LOG2E)).T.astype(BF16)
    wv = wl[:, o3:o4].T.reshape(n_heads, HEAD_DIM, d)
    wvT = jnp.concatenate([wv, jnp.zeros((n_heads, BF16_ROWS, d), F32)], axis=1)
    wvT = wvT.reshape(n_heads * V_ROWS, d).astype(BF16)
    vb = jnp.concatenate([jnp.zeros((n_heads, HEAD_DIM, 1), F32), jnp.ones((n_heads, BF16_ROWS, 1), F32)],
                         axis=1).reshape(n_heads * V_ROWS, 1)
    u, k3, kmean, qT, vT3, gates = _inproj(
        x2, ln1_g[0][None], w_main, wqT, wvT, vb, b_gate[0][None], batch=batch, seq=seq, tm=512)

    bb, cc, ar, ai = _ssm_params(ssm_lambda_re[0], ssm_lambda_im[0], ssm_log_step[0],
                                 ssm_b_re[0], ssm_b_im[0], ssm_c_re[0], ssm_c_im[0])
    y_ssm = _s5(u.reshape(batch, seq, d_ssm), bb, cc, ar, ai, ssm_d[0].reshape(1, d_ssm),
                w_glu[0].astype(BF16), b_glu[0][None], ts=128, chunk=512)

    bias, far = _bias_tables(rel_bias)
    y_attn = _moba(far, qT, k3, vT3, kmean.reshape(batch, seq // MOBA_BLOCK, d_attn), bias,
                   batch=batch, seq=seq)

    wr = jnp.zeros((d, ROUTER_LANES), F32)
    wr = wr.at[:, :N_GROUPS].set(w_router_group[0])
    wr = wr.at[:, EXPERT_LANE0:EXPERT_LANE0 + N_EXPERTS].set(w_router_expert[0]).astype(BF16)
    br = jnp.zeros((1, ROUTER_LANES), F32)
    br = br.at[0, :N_GROUPS].set(b_router_group[0])
    br = br.at[0, EXPERT_LANE0:EXPERT_LANE0 + N_EXPERTS].set(b_router_expert[0])
    x1e, cls, rank, cnt = _merge(
        x2, y_ssm.reshape(t, d_ssm), y_attn.reshape(t, d_attn), gates,
        w_up_ssm[0].astype(BF16), w_up_attn[0].astype(BF16), w_out[0].astype(BF16),
        ln2_g[0][None], wr, br, tm=512)

    pos, items = _routing_tables(cls, rank, cnt, rows=MOE_ROWS)
    x1s = _permute_rows(pos, x1e, scatter=True)
    ys = _moe(items, x1s, w1[0].astype(BF16), w3[0].astype(BF16), w2[0].astype(BF16),
              ln2_g[0][None], ln_f_g[None], rows=MOE_ROWS)
    y = _permute_rows(pos, ys, scatter=False)
    return y.reshape(batch, seq, d)
```

```python
import functools
import math

import jax
import jax.numpy as jnp
from jax import lax
from jax.experimental import pallas as pl
from jax.experimental.pallas import tpu as pltpu
from jax.experimental.pallas import tpu_sc as plsc

F32 = jnp.float32
BF16 = jnp.bfloat16

SSM_GROUP = 16
SSM_STATE = 64
HEAD_DIM = 64
MOBA_BLOCK = 256
MOBA_TOP_K = 3
NUM_BUCKETS = 32
MAX_DISTANCE = 128
N_GROUPS = 4
EXPERTS_PER_GROUP = 8
N_EXPERTS = N_GROUPS * EXPERTS_PER_GROUP
PAIRS_PER_GROUP = EXPERTS_PER_GROUP * (EXPERTS_PER_GROUP - 1) // 2
N_CLASSES = N_GROUPS * PAIRS_PER_GROUP
MOE_ROWS = 256
SC_WINDOW = 32
SSM_PARTS = 2
MOBA_HEADS_PER_STEP = 8
RMS_EPS = 1e-6
NEG_INF = -1e30
LOG2E = 1.4426950408889634

LANES = 128
MXU_DEPTH = 256
BF16_ROWS = 16
V_ROWS = HEAD_DIM + BF16_ROWS
ROUTER_LANES = LANES
EXPERT_LANE0 = N_GROUPS
VMEM_LIMIT = 56 * 1024 * 1024


def _sigmoid(x):
    return 1.0 / (1.0 + jnp.exp(-x))


def _rms(x, g):
    ms = jnp.mean(x * x, axis=-1, keepdims=True)
    return x * lax.rsqrt(ms + RMS_EPS) * g


def _const_spec(shape):
    n = len(shape)
    return pl.BlockSpec(shape, lambda *_: (0,) * n)


def _inproj_kernel(x_ref, g_ref, w_ref, wqT_ref, wvT_ref, vb_ref, bg_ref,
                   u_ref, k_ref, kmean_ref, qT_ref, vT_ref, gates_ref, *, nblk, d_ssm, d_attn):
    hn = _rms(x_ref[...], g_ref[...]).astype(BF16)
    acc = jnp.dot(hn, w_ref[...], preferred_element_type=F32)
    u_ref[...] = acc[:, :d_ssm]
    kf = acc[:, d_ssm:d_ssm + d_attn].reshape(nblk, MOBA_BLOCK, d_attn)
    k_ref[...] = kf.astype(BF16)
    kmean_ref[...] = jnp.mean(kf, axis=1, keepdims=True)
    gates_ref[...] = (acc[:, d_ssm + d_attn:] + bg_ref[...]).astype(BF16)
    nt = (((1,), (1,)), ((), ()))
    qT = lax.dot_general(wqT_ref[...], hn, nt, preferred_element_type=F32)
    qT_ref[0] = qT.astype(BF16)
    vT = lax.dot_general(wvT_ref[...], hn, nt, preferred_element_type=F32) + vb_ref[...]
    for j in range(nblk):
        vT_ref[j] = vT[:, j * MOBA_BLOCK:(j + 1) * MOBA_BLOCK].astype(BF16)


def _inproj(x2, g1, w_main, wqT, wvT, vb, bg, *, batch, seq, tm):
    t, d = x2.shape
    d_attn = wqT.shape[0]
    v_rows = wvT.shape[0]
    d_ssm = w_main.shape[1] - d_attn - bg.shape[1]
    nblk = tm // MOBA_BLOCK
    tiles_per_seq = seq // tm
    nb_total = t // MOBA_BLOCK
    kern = functools.partial(_inproj_kernel, nblk=nblk, d_ssm=d_ssm, d_attn=d_attn)
    return pl.pallas_call(
        kern,
        grid=(t // tm,),
        in_specs=[
            pl.BlockSpec((tm, d), lambda i: (i, 0)),
            _const_spec(g1.shape),
            _const_spec(w_main.shape),
            _const_spec(wqT.shape),
            _const_spec(wvT.shape),
            _const_spec(vb.shape),
            _const_spec(bg.shape),
        ],
        out_specs=[
            pl.BlockSpec((tm, d_ssm), lambda i: (i, 0)),
            pl.BlockSpec((nblk, MOBA_BLOCK, d_attn), lambda i: (i, 0, 0)),
            pl.BlockSpec((nblk, 1, d_attn), lambda i: (i, 0, 0)),
            pl.BlockSpec((1, d_attn, tm), lambda i: (i // tiles_per_seq, 0, i % tiles_per_seq)),
            pl.BlockSpec((nblk, v_rows, MOBA_BLOCK), lambda i: (i, 0, 0)),
            pl.BlockSpec((tm, bg.shape[1]), lambda i: (i, 0)),
        ],
        out_shape=[
            jax.ShapeDtypeStruct((t, d_ssm), F32),
            jax.ShapeDtypeStruct((nb_total, MOBA_BLOCK, d_attn), BF16),
            jax.ShapeDtypeStruct((nb_total, 1, d_attn), F32),
            jax.ShapeDtypeStruct((batch, d_attn, seq), BF16),
            jax.ShapeDtypeStruct((nb_total, v_rows, MOBA_BLOCK), BF16),
            jax.ShapeDtypeStruct((t, bg.shape[1]), BF16),
        ],
        compiler_params=pltpu.CompilerParams(
            dimension_semantics=("arbitrary",), vmem_limit_bytes=VMEM_LIMIT),
    )(x2, g1, w_main, wqT, wvT, vb, bg)


def _s5_kernel(u_ref, bb_ref, cc_ref, ar_ref, ai_ref, d_ref, wglu_ref, bglu_ref,
               y_ref, utb_ref, sbuf_ref, state_ref, ytb_ref, *, batch, ts, n_state, chunk):
    d_ssm = u_ref.shape[2]
    n_slab = d_ssm // LANES

    @pl.when(pl.program_id(0) == 0)
    def _():
        state_ref[...] = jnp.zeros_like(state_ref)

    for b in range(batch):
        ub = u_ref[b]
        for j in range(n_slab):
            utb_ref[j, pl.ds(b, ts, stride=batch), :] = ub[:, j * LANES:(j + 1) * LANES]
    u_tb = jnp.concatenate([utb_ref[j] for j in range(n_slab)], axis=1)
    n_part, pc, pw = bb_ref.shape
    ps = pw // 2
    u_bf = u_tb.astype(BF16)
    for h in range(n_part):
        sbuf_ref[:, h * pw:(h + 1) * pw] = jnp.dot(u_bf[:, h * pc:(h + 1) * pc], bb_ref[h],
                                                   preferred_element_type=F32)

    for c in range(n_state // chunk):
        h, cc_ = divmod(c * chunk, ps)
        re = slice(h * pw + cc_, h * pw + cc_ + chunk)
        im = slice(h * pw + ps + cc_, h * pw + ps + cc_ + chunk)
        ar = jnp.broadcast_to(ar_ref[:, c * chunk:(c + 1) * chunk], (batch, chunk))
        ai = jnp.broadcast_to(ai_ref[:, c * chunk:(c + 1) * chunk], (batch, chunk))

        def step(t, carry, re=re, im=im, ar=ar, ai=ai):
            hr, hi = carry
            r0 = pl.multiple_of(t * batch, batch)
            nhr = ar * hr - ai * hi + sbuf_ref[pl.ds(r0, batch), re]
            nhi = ar * hi + ai * hr + sbuf_ref[pl.ds(r0, batch), im]
            sbuf_ref[pl.ds(r0, batch), re] = nhr
            sbuf_ref[pl.ds(r0, batch), im] = nhi
            return nhr, nhi

        hr, hi = lax.fori_loop(0, ts, step, (state_ref[:, re], state_ref[:, im]), unroll=8)
        state_ref[:, re] = hr
        state_ref[:, im] = hi

    y = jnp.concatenate(
        [jnp.dot(sbuf_ref[:, h * pw:(h + 1) * pw].astype(BF16), cc_ref[h], preferred_element_type=F32)
         for h in range(n_part)], axis=1)
    y = y + d_ref[...] * u_tb
    g = jax.nn.gelu(y)
    z = jnp.dot(g.astype(BF16), wglu_ref[...], preferred_element_type=F32) + bglu_ref[...]
    out = g * _sigmoid(z)
    for j in range(n_slab):
        ytb_ref[j] = out[:, j * LANES:(j + 1) * LANES]
    for b in range(batch):
        y_ref[b] = jnp.concatenate(
            [ytb_ref[j, pl.ds(b, ts, stride=batch), :] for j in range(n_slab)], axis=1).astype(BF16)


def _s5(u3, bb, cc, ar, ai, dvec, wglu, bglu, *, ts, chunk):
    batch, seq, d_ssm = u3.shape
    n_state = ar.shape[1]
    kern = functools.partial(_s5_kernel, batch=batch, ts=ts, n_state=n_state, chunk=chunk)
    return pl.pallas_call(
        kern,
        grid=(seq // ts,),
        in_specs=[
            pl.BlockSpec((batch, ts, d_ssm), lambda i: (0, i, 0)),
            _const_spec(bb.shape), _const_spec(cc.shape), _const_spec(ar.shape), _const_spec(ai.shape),
            _const_spec(dvec.shape), _const_spec(wglu.shape), _const_spec(bglu.shape),
        ],
        out_specs=pl.BlockSpec((batch, ts, d_ssm), lambda i: (0, i, 0)),
        out_shape=jax.ShapeDtypeStruct((batch, seq, d_ssm), BF16),
        scratch_shapes=[
            pltpu.VMEM((d_ssm // LANES, ts * batch, LANES), F32),
            pltpu.VMEM((ts * batch, 2 * n_state), F32),
            pltpu.VMEM((batch, 2 * n_state), F32),
            pltpu.VMEM((d_ssm // LANES, ts * batch, LANES), F32),
        ],
        compiler_params=pltpu.CompilerParams(
            dimension_semantics=("arbitrary",), vmem_limit_bytes=VMEM_LIMIT),
    )(u3, bb, cc, ar, ai, dvec, wglu, bglu)


def _select_blocks(gate, blk_f, n_pick):
    sel = jnp.zeros(gate.shape, jnp.bool_)
    for _ in range(n_pick):
        mx = jnp.max(gate, axis=0, keepdims=True)
        idx = jnp.min(jnp.where(gate == mx, blk_f, float(gate.shape[0])), axis=0, keepdims=True)
        hit = blk_f == idx
        sel = sel | hit
        gate = jnp.where(hit, -jnp.inf, gate)
    return sel


def _moba_kernel(farb_ref, qT_ref, k_ref, vT_ref, kmean_ref, bias_ref, o_ref,
                 mask_ref, s_ref, p0_ref, p1_ref, acc_ref, *, nb):
    hg = pl.program_id(1)
    i = pl.program_id(2)
    tq = qT_ref.shape[2]
    hw = qT_ref.shape[1]
    nh = hw // HEAD_DIM
    qT = qT_ref[0]
    row = lax.broadcasted_iota(jnp.int32, qT.shape, 0)
    km = kmean_ref[0]
    km_hi = km.astype(BF16)
    km_lo = (km - km_hi.astype(F32)).astype(BF16)
    blk = lax.broadcasted_iota(jnp.int32, (nb, tq), 0)
    blk_f = blk.astype(F32)
    prev = jnp.maximum(i - 1, 0)
    n_far = jnp.maximum(i - 1, 0)
    n_iter = (n_far + 3) // 4

    gw = min(hw, MXU_DEPTH)
    groups = [slice(j * HEAD_DIM // gw * gw, j * HEAD_DIM // gw * gw + gw) for j in range(nh)]

    def qk(j, keys):
        return jnp.dot(keys[:, groups[j]], qTs[j], preferred_element_type=F32)

    qTs, near_masks = [], []
    for j in range(nh):
        qTh = jnp.where((row >= j * HEAD_DIM) & (row < (j + 1) * HEAD_DIM), qT, jnp.zeros_like(qT))
        qTs.append(qTh[groups[j]])
        gate = qk(j, km_hi) + qk(j, km_lo)
        gate = jnp.where(blk < i, gate, NEG_INF)
        sel = _select_blocks(gate, blk_f, MOBA_TOP_K) & (blk < i)
        mask_ref[j] = jnp.where(sel & (blk < i - 1), farb_ref[nh * hg + j], NEG_INF)
        mprev = jnp.max(jnp.where(sel & (blk == i - 1), 0.0, NEG_INF), axis=0, keepdims=True)
        near_masks.append((mprev, jnp.zeros_like(mprev)))

    def far_keys(n0):
        n0 = pl.multiple_of(n0, 2)
        return k_ref[pl.ds(n0, 2)].reshape(2 * MOBA_BLOCK, hw)

    def far_masks(j, n0):
        return mask_ref[j, pl.ds(n0, 1), :], mask_ref[j, pl.ds(n0 + 1, 1), :]

    def block_max(s, masks):
        return jnp.maximum(jnp.max(s[:MOBA_BLOCK], axis=0, keepdims=True) + masks[0],
                           jnp.max(s[MOBA_BLOCK:], axis=0, keepdims=True) + masks[1])

    def probs(j, p_ref, s, shift, masks):
        p_ref[j, :MOBA_BLOCK] = jnp.exp2((s[:MOBA_BLOCK] - (shift - masks[0])).astype(BF16))
        p_ref[j, MOBA_BLOCK:] = jnp.exp2((s[MOBA_BLOCK:] - (shift - masks[1])).astype(BF16))

    def pv(j, p_ref, va, vb):
        vrows = slice(j * V_ROWS, (j + 1) * V_ROWS)
        vv = jnp.concatenate([va[vrows, :], vb[vrows, :]], axis=1)
        acc_ref[j] += jnp.dot(vv, p_ref[j], preferred_element_type=F32)

    near_keys = jnp.concatenate([k_ref[prev], k_ref[i]], axis=0)
    near_max = []
    for j in range(nh):
        s = qk(j, near_keys) + bias_ref[j]
        s_ref[j] = s
        near_max.append(block_max(s, near_masks[j]))

    def attend(shift):
        acc_ref[...] = jnp.zeros_like(acc_ref)
        first = far_keys(0)
        for j in range(nh):
            probs(j, p0_ref, s_ref[j], shift[j], near_masks[j])
            probs(j, p1_ref, qk(j, first), shift[j], far_masks(j, 0))

        def trip(g, carry):
            f0 = 4 * g
            n2 = f0 + 2
            n3 = jnp.minimum(f0 + 4, nb - 2)
            va0 = vT_ref[jnp.where(g == 0, prev, f0 - 2)]
            vb0 = vT_ref[jnp.where(g == 0, i, f0 - 1)]
            va1 = vT_ref[f0]
            vb1 = vT_ref[f0 + 1]
            k2 = far_keys(n2)
            k3_ = far_keys(n3)
            for j in range(nh):
                pv(j, p0_ref, va0, vb0)
                probs(j, p0_ref, qk(j, k2), shift[j], far_masks(j, n2))
                pv(j, p1_ref, va1, vb1)
                probs(j, p1_ref, qk(j, k3_), shift[j], far_masks(j, n3))
            return carry

        lax.fori_loop(0, n_iter, trip, 0)
        last = 4 * n_iter
        va = vT_ref[jnp.where(n_iter == 0, prev, last - 2)]
        vb = vT_ref[jnp.where(n_iter == 0, i, last - 1)]
        for j in range(nh):
            pv(j, p0_ref, va, vb)

    attend(near_max)
    acc = acc_ref[...]
    overflow = jnp.max(jnp.where(jnp.isfinite(acc), 0.0, 1.0)) > 0.0

    @pl.when(overflow)
    def _():
        def far_max(n, m_run):
            kp = k_ref[n]
            return tuple(jnp.maximum(m_run[j], jnp.max(qk(j, kp), axis=0, keepdims=True) + mask_ref[j, pl.ds(n, 1), :])
                         for j in range(nh))

        attend(lax.fori_loop(0, n_far, far_max, tuple(near_max)))

    outs = [acc_ref[j][:HEAD_DIM] / acc_ref[j][HEAD_DIM:HEAD_DIM + 1] for j in range(nh)]
    o_ref[0] = jnp.concatenate(outs, axis=0).T.astype(BF16)


def _moba(farb, qT, k3, vT3, kmean, bias, *, batch, seq):
    nb = seq // MOBA_BLOCK
    assert nb % 4 == 0, "far blocks are consumed four per loop trip"
    d_attn = qT.shape[1]
    nh = MOBA_HEADS_PER_STEP
    hw = nh * HEAD_DIM
    kern = functools.partial(_moba_kernel, nb=nb)
    return pl.pallas_call(
        kern,
        grid=(batch, d_attn // hw, nb),
        in_specs=[
            pl.BlockSpec(memory_space=pltpu.SMEM),
            pl.BlockSpec((1, hw, MOBA_BLOCK), lambda b, h, i: (b, h, i)),
            pl.BlockSpec((nb, MOBA_BLOCK, hw), lambda b, h, i: (b, 0, h), pipeline_mode=pl.Buffered(1)),
            pl.BlockSpec((nb, nh * V_ROWS, MOBA_BLOCK), lambda b, h, i: (b, h, 0), pipeline_mode=pl.Buffered(1)),
            pl.BlockSpec((1, nb, hw), lambda b, h, i: (b, 0, h)),
            pl.BlockSpec((nh, 2 * MOBA_BLOCK, MOBA_BLOCK), lambda b, h, i: (h, 0, 0), pipeline_mode=pl.Buffered(1)),
        ],
        out_specs=pl.BlockSpec((1, MOBA_BLOCK, hw), lambda b, h, i: (b, i, h)),
        out_shape=jax.ShapeDtypeStruct((batch, seq, d_attn), BF16),
        scratch_shapes=[
            pltpu.VMEM((nh, nb, MOBA_BLOCK), F32),
            pltpu.VMEM((nh, 2 * MOBA_BLOCK, MOBA_BLOCK), F32),
            pltpu.VMEM((nh, 2 * MOBA_BLOCK, MOBA_BLOCK), BF16),
            pltpu.VMEM((nh, 2 * MOBA_BLOCK, MOBA_BLOCK), BF16),
            pltpu.VMEM((nh, V_ROWS, MOBA_BLOCK), F32),
        ],
        compiler_params=pltpu.CompilerParams(
            dimension_semantics=("arbitrary", "arbitrary", "arbitrary"),
            vmem_limit_bytes=VMEM_LIMIT),
    )(farb, qT, k3, vT3, kmean, bias)


def _route(logits):
    lane = lax.broadcasted_iota(jnp.int32, logits.shape, 1)
    lane_f = lane.astype(F32)
    big = float(ROUTER_LANES)
    ninf = -jnp.inf
    gmask = lane < N_GROUPS
    gmax = jnp.max(jnp.where(gmask, logits, ninf), axis=1, keepdims=True)
    gsum = jnp.sum(jnp.where(gmask, jnp.exp(logits - gmax), 0.0), axis=1, keepdims=True)
    g_val = 1.0 / gsum
    g_idx = jnp.min(jnp.where(gmask & (logits == gmax), lane_f, big), axis=1, keepdims=True)
    e_grp = ((lane - EXPERT_LANE0) >> 3).astype(F32)
    emask = (lane >= EXPERT_LANE0) & (lane < EXPERT_LANE0 + N_EXPERTS) & (e_grp == g_idx)
    el = jnp.where(emask, logits, ninf)
    e1 = jnp.max(el, axis=1, keepdims=True)
    i1 = jnp.min(jnp.where(emask & (el == e1), lane_f, big), axis=1, keepdims=True)
    el2 = jnp.where(lane_f == i1, ninf, el)
    e2 = jnp.max(el2, axis=1, keepdims=True)
    i2 = jnp.min(jnp.where(emask & (lane_f != i1) & (el2 == e2), lane_f, big), axis=1, keepdims=True)
    tt = jnp.exp(e2 - e1)
    w1 = g_val / (1.0 + tt)
    w2 = g_val * tt / (1.0 + tt)
    first_low = i1 < i2
    lo = jnp.minimum(i1, i2) - (EXPERT_LANE0 + EXPERTS_PER_GROUP * g_idx)
    hi = jnp.maximum(i1, i2) - (EXPERT_LANE0 + EXPERTS_PER_GROUP * g_idx)
    pair = lo * (2 * EXPERTS_PER_GROUP - 1 - lo) * 0.5 + (hi - lo - 1.0)
    cls = g_idx * PAIRS_PER_GROUP + pair
    return cls, jnp.where(first_low, w1, w2), jnp.where(first_low, w2, w1)


def _col_to_row(col):
    n = col.shape[0]
    r = lax.broadcasted_iota(jnp.int32, (n, n), 0)
    c = lax.broadcasted_iota(jnp.int32, (n, n), 1)
    return jnp.sum(jnp.where(r == c, col, 0.0), axis=0, keepdims=True)


def _merge_kernel(x_ref, ys_ref, ya_ref, gates_ref, wus_ref, wua_ref, wout_ref, g2_ref, wr_ref, br_ref,
                  tri_ref, x1e_ref, cls_ref, rank_ref, cnt_ref):
    tm, d = x_ref.shape
    halves = [slice(0, tm // 2), slice(tm // 2, tm)]
    ups = [(jnp.dot(ys_ref[r, :], wus_ref[...], preferred_element_type=F32),
            jnp.dot(ya_ref[r, :], wua_ref[...], preferred_element_type=F32)) for r in halves]
    logits = []
    for r, (a, b) in zip(halves, ups):
        gts = gates_ref[r, :].astype(F32)
        merged = _sigmoid(gts[:, :d]) * a + _sigmoid(gts[:, d:]) * b
        x1 = x_ref[r, :] + jnp.dot(merged.astype(BF16), wout_ref[...], preferred_element_type=F32)
        x1e_ref[r, :d] = x1
        xn = _rms(x1, g2_ref[...]).astype(BF16)
        logits.append(jnp.dot(xn, wr_ref[...], preferred_element_type=F32) + br_ref[...])
    logits = jnp.concatenate(logits, axis=0)
    cls, wa, wb = _route(logits)
    lane = lax.broadcasted_iota(jnp.int32, logits.shape, 1)
    x1e_ref[:, d:] = jnp.where(lane == 0, wa, jnp.where(lane == 1, wb, 0.0))
    onehot = lane.astype(F32) == cls
    before = jnp.dot(tri_ref[...], onehot.astype(BF16), preferred_element_type=F32)
    rank = jnp.sum(jnp.where(onehot, before, 0.0), axis=1, keepdims=True)
    cls_ref[0] = _col_to_row(cls).astype(jnp.int32)
    rank_ref[0] = _col_to_row(rank).astype(jnp.int32)
    cnt_ref[0] = jnp.sum(onehot.astype(F32), axis=0, keepdims=True).astype(jnp.int32)


def _merge(x2, ys, ya, gates, wus, wua, wout, g2, wr, br, *, tm):
    t, d = x2.shape
    nt = t // tm
    tri = jnp.tril(jnp.ones((tm, tm), F32), -1).astype(BF16)
    return pl.pallas_call(
        _merge_kernel,
        grid=(nt,),
        in_specs=[
            pl.BlockSpec((tm, d), lambda i: (i, 0)),
            pl.BlockSpec((tm, ys.shape[1]), lambda i: (i, 0)),
            pl.BlockSpec((tm, ya.shape[1]), lambda i: (i, 0)),
            pl.BlockSpec((tm, gates.shape[1]), lambda i: (i, 0)),
            _const_spec(wus.shape), _const_spec(wua.shape), _const_spec(wout.shape),
            _const_spec(g2.shape), _const_spec(wr.shape), _const_spec(br.shape), _const_spec(tri.shape),
        ],
        out_specs=[
            pl.BlockSpec((tm, d + ROUTER_LANES), lambda i: (i, 0)),
            pl.BlockSpec((1, 1, tm), lambda i: (i, 0, 0)),
            pl.BlockSpec((1, 1, tm), lambda i: (i, 0, 0)),
            pl.BlockSpec((1, 1, ROUTER_LANES), lambda i: (i, 0, 0)),
        ],
        out_shape=[
            jax.ShapeDtypeStruct((t, d + ROUTER_LANES), F32),
            jax.ShapeDtypeStruct((nt, 1, tm), jnp.int32),
            jax.ShapeDtypeStruct((nt, 1, tm), jnp.int32),
            jax.ShapeDtypeStruct((nt, 1, ROUTER_LANES), jnp.int32),
        ],
        compiler_params=pltpu.CompilerParams(
            dimension_semantics=("arbitrary",), vmem_limit_bytes=VMEM_LIMIT),
    )(x2, ys, ya, gates, wus, wua, wout, g2, wr, br, tri)


def _permute_rows(pos, src, *, scatter):
    n, width = src.shape
    w = SC_WINDOW
    idx = jnp.pad(pos.reshape(n // w, w), ((0, 0), (0, LANES - w)))
    mesh = plsc.VectorSubcoreMesh(core_axis_name="core", subcore_axis_name="subcore")
    idx_spec = pl.BlockSpec((1, LANES), index_map=lambda i: (i, 0))
    row_spec = pl.BlockSpec((w, width), index_map=lambda i: (i, 0))

    @pl.kernel(out_type=jax.ShapeDtypeStruct(src.shape, src.dtype), mesh=mesh, scratch_types=[])
    def permute(src_hbm, idx_hbm, dst_hbm):
        if scatter:
            def body(rows_vmem, idx_vmem):
                pltpu.sync_copy(rows_vmem, dst_hbm.at[idx_vmem.at[0, pl.ds(0, w)]])
            in_specs, out_specs, args = [row_spec, idx_spec], [], (src_hbm, idx_hbm)
        else:
            def body(idx_vmem, rows_vmem):
                pltpu.sync_copy(src_hbm.at[idx_vmem.at[0, pl.ds(0, w)]], rows_vmem)
            in_specs, out_specs, args = [idx_spec], [row_spec], (idx_hbm, dst_hbm)
        pltpu.emit_pipeline(body, grid=(n // w,), in_specs=in_specs, out_specs=out_specs,
                            core_axis_name=("core", "subcore"),
                            dimension_semantics=(pltpu.PARALLEL,))(*args)

    return permute(src, idx)


def _expert(xn, w1_ref, w3_ref, w2_ref):
    h1 = jnp.dot(xn, w1_ref[0], preferred_element_type=F32)
    h3 = jnp.dot(xn, w3_ref[0], preferred_element_type=F32)
    hid = (h1 * _sigmoid(h1) * h3).astype(BF16)
    return jnp.dot(hid, w2_ref[0], preferred_element_type=F32)


def _moe_kernel(tile_ref, ea_ref, eb_ref, lo_ref, hi_ref, first_ref, last_ref,
                x_ref, w1a_ref, w3a_ref, w2a_ref, w1b_ref, w3b_ref, w2b_ref, g2_ref, gf_ref,
                y_ref, acc_ref):
    w = pl.program_id(0)
    d = y_ref.shape[1]
    lo = lo_ref[w]
    hi = hi_ref[w]

    @pl.when(hi > lo)
    def _():
        x1 = x_ref[:, :d]
        row = lax.broadcasted_iota(jnp.int32, (x1.shape[0], 1), 0)
        inseg = (row >= lo) & (row < hi)
        ca = jnp.where(inseg, x_ref[:, d:d + 1], 0.0)
        cb = jnp.where(inseg, x_ref[:, d + 1:d + 2], 0.0)
        xn = _rms(x1, g2_ref[...]).astype(BF16)
        contrib = (ca * _expert(xn, w1a_ref, w3a_ref, w2a_ref)
                   + cb * _expert(xn, w1b_ref, w3b_ref, w2b_ref))

        @pl.when(first_ref[w] == 1)
        def _():
            acc_ref[...] = contrib

        @pl.when(first_ref[w] == 0)
        def _():
            acc_ref[...] += contrib

        @pl.when(last_ref[w] == 1)
        def _():
            y_ref[...] = _rms(x1 + acc_ref[...], gf_ref[...])


def _moe(items, x1s, w1, w3, w2, g2, gf, *, rows):
    t, de_ = x1s.shape
    d = g2.shape[1]
    ne, _, de = w1.shape
    n_items = items[0].shape[0]

    def wa_map(w, tile, ea, eb, lo, hi, first, last):
        return (ea[w], 0, 0)

    def wb_map(w, tile, ea, eb, lo, hi, first, last):
        return (eb[w], 0, 0)

    def row_map(w, tile, ea, eb, lo, hi, first, last):
        return (tile[w], 0)

    def const_map(w, *_):
        return (0, 0)

    grid_spec = pltpu.PrefetchScalarGridSpec(
        num_scalar_prefetch=len(items),
        grid=(n_items,),
        in_specs=[
            pl.BlockSpec((rows, de_), row_map),
            pl.BlockSpec((1, d, de), wa_map), pl.BlockSpec((1, d, de), wa_map), pl.BlockSpec((1, de, d), wa_map),
            pl.BlockSpec((1, d, de), wb_map), pl.BlockSpec((1, d, de), wb_map), pl.BlockSpec((1, de, d), wb_map),
            pl.BlockSpec(g2.shape, const_map), pl.BlockSpec(gf.shape, const_map),
        ],
        out_specs=pl.BlockSpec((rows, d), row_map),
        scratch_shapes=[pltpu.VMEM((rows, d), F32)],
    )
    return pl.pallas_call(
        _moe_kernel,
        grid_spec=grid_spec,
        out_shape=jax.ShapeDtypeStruct((t, d), F32),
        compiler_params=pltpu.CompilerParams(
            dimension_semantics=("arbitrary",), vmem_limit_bytes=VMEM_LIMIT),
    )(*items, x1s, w1, w3, w2, w1, w3, w2, g2, gf)


def _pair_table():
    lo, hi = [], []
    for a in range(EXPERTS_PER_GROUP):
        for b in range(a + 1, EXPERTS_PER_GROUP):
            lo.append(a)
            hi.append(b)
    return jnp.asarray(lo, jnp.int32), jnp.asarray(hi, jnp.int32)


def _before_sum(v):
    ix = jnp.arange(v.shape[0])
    return jnp.sum(jnp.where(ix[None, :] < ix[:, None], v[None, :], 0), axis=1)


def _place(dest, vals, n):
    return jnp.sum(jnp.where(dest[None, :] == jnp.arange(n)[:, None], vals[None, :], 0), axis=1)


def _routing_tables(cls, rank, cnt, *, rows):
    nt, _, tm = cls.shape
    t = nt * tm
    lanes = cnt.shape[2]
    c = cnt[:, 0, :]
    tot = jnp.sum(c, axis=0)
    class_off = _before_sum(tot)
    tile_ix = jnp.arange(nt)
    tiles_before = jnp.sum(jnp.where((tile_ix[None, :] < tile_ix[:, None])[:, :, None], c[None], 0), axis=1)
    base = class_off[None, :] + tiles_before
    onehot = cls[:, 0, :, None] == jnp.arange(lanes)[None, None, :]
    pos = jnp.sum(jnp.where(onehot, base[:, None, :], 0), axis=2) + rank[:, 0, :]

    n_tiles = t // rows
    ta = jnp.arange(n_tiles, dtype=jnp.int32) * rows
    cb = class_off[1:N_CLASSES]
    n_items = n_tiles + N_CLASSES - 1
    idx = jnp.arange(n_items)
    at_a = jnp.arange(n_tiles) + jnp.sum(cb[None, :] <= ta[:, None], axis=1)
    at_b = jnp.arange(N_CLASSES - 1) + jnp.sum(ta[None, :] < cb[:, None], axis=1)
    starts = _place(at_a, ta, n_items) + _place(at_b, cb, n_items)
    ends = jnp.concatenate([starts[1:], jnp.full((1,), t, jnp.int32)])
    real = ends > starts
    n_real = jnp.sum(real)
    dest = jnp.where(real, _before_sum(real.astype(jnp.int32)), n_real + _before_sum(1 - real.astype(jnp.int32)))
    starts, ends = _place(dest, starts, n_items), _place(dest, ends, n_items)
    valid = idx < n_real
    last_real = jnp.maximum(n_real - 1, 0)
    starts = jnp.where(valid, starts, jnp.sum(jnp.where(idx == last_real, starts, 0)))
    tile = jnp.minimum(starts // rows, n_tiles - 1)
    klass = jnp.sum(class_off[None, :N_CLASSES] <= starts[:, None], axis=1) - 1
    pair_lo, pair_hi = _pair_table()
    grp = klass // PAIRS_PER_GROUP
    pair_hot = (klass % PAIRS_PER_GROUP)[:, None] == jnp.arange(PAIRS_PER_GROUP)[None, :]
    ea = grp * EXPERTS_PER_GROUP + jnp.sum(jnp.where(pair_hot, pair_lo[None, :], 0), axis=1)
    eb = grp * EXPERTS_PER_GROUP + jnp.sum(jnp.where(pair_hot, pair_hi[None, :], 0), axis=1)
    lo = jnp.where(valid, starts - tile * rows, 0)
    hi = jnp.where(valid, ends - tile * rows, 0)
    tile_prev = jnp.concatenate([tile[:1], tile[:-1]])
    tile_next = jnp.concatenate([tile[1:], tile[-1:]])
    first = valid & ((idx == 0) | (tile != tile_prev))
    last = valid & ((idx == last_real) | (tile != tile_next))
    items = tuple(a.astype(jnp.int32) for a in (tile, ea, eb, lo, hi, first, last))
    return pos.reshape(t).astype(jnp.int32), items


def _t5_bucket(dist):
    n = jnp.maximum(dist, 0)
    max_exact = NUM_BUCKETS // 2
    large = max_exact + (jnp.log(jnp.maximum(n, max_exact).astype(F32) / max_exact)
                         / math.log(MAX_DISTANCE / max_exact)
                         * (NUM_BUCKETS - max_exact)).astype(jnp.int32)
    return jnp.where(n < max_exact, n, jnp.minimum(large, NUM_BUCKETS - 1))


def _bias_tables(rel_bias):
    kk = jnp.arange(MOBA_BLOCK)[:, None]
    qq = jnp.arange(MOBA_BLOCK)[None, :]
    rb = rel_bias.astype(F32) * LOG2E
    d_own = qq - kk

    def lookup(dist):
        oh = jax.nn.one_hot(_t5_bucket(dist), NUM_BUCKETS, dtype=F32)
        return jnp.einsum('hn,kqn->hkq', rb, oh, precision=lax.Precision.HIGHEST)

    own = jnp.where((d_own >= 0)[None], lookup(d_own), NEG_INF)
    prv = lookup(d_own + MOBA_BLOCK)
    tab = jnp.concatenate([prv, own], axis=1)
    far = rb[:, NUM_BUCKETS - 1]
    return tab, far


def _ssm_params(lam_re, lam_im, log_step, b_re, b_im, c_re, c_im):
    g, p = lam_re.shape
    step = jnp.exp(log_step)[:, None]
    decay = jnp.exp(lam_re * step)
    a_re = decay * jnp.cos(lam_im * step)
    a_im = decay * jnp.sin(lam_im * step)
    denom = lam_re * lam_re + lam_im * lam_im
    nr, ni = a_re - 1.0, a_im
    coef_re = (nr * lam_re + ni * lam_im) / denom
    coef_im = (ni * lam_re - nr * lam_im) / denom
    bb_re = coef_re[..., None] * b_re - coef_im[..., None] * b_im
    bb_im = coef_re[..., None] * b_im + coef_im[..., None] * b_re
    eye = jnp.eye(g, dtype=F32)
    hc = b_re.shape[2]

    def in_mat(m):
        return (eye[:, None, :, None] * m.transpose(0, 2, 1)[:, :, None, :]).reshape(g * hc, g * p)

    def out_mat(m):
        return (eye[:, None, :, None] * m.transpose(0, 2, 1)[:, :, None, :]).reshape(g * p, g * hc)

    pc, ps = g * hc // SSM_PARTS, g * p // SSM_PARTS
    in_re, in_im, out_re, out_im = in_mat(bb_re), in_mat(bb_im), out_mat(c_re), out_mat(c_im)
    bb = jnp.stack([jnp.concatenate([m[h * pc:(h + 1) * pc, h * ps:(h + 1) * ps] for m in (in_re, in_im)], axis=1)
                    for h in range(SSM_PARTS)]).astype(BF16)
    cc = jnp.stack([jnp.concatenate([m[h * ps:(h + 1) * ps, h * pc:(h + 1) * pc] for m in (out_re, -out_im)], axis=0)
                    for h in range(SSM_PARTS)]).astype(BF16)
    return bb, cc, a_re.reshape(1, g * p), a_im.reshape(1, g * p)


def kernel(x, ln1_g, w_in, b_gate, ssm_lambda_re, ssm_lambda_im, ssm_log_step, ssm_b_re, ssm_b_im,
           ssm_c_re, ssm_c_im, ssm_d, w_glu, b_glu, w_up_ssm, w_up_attn, rel_bias, w_out, ln2_g,
           w_router_group, b_router_group, w_router_expert, b_router_expert, w1, w3, w2, ln_f_g):
    assert w_in.shape[0] == 1, "single-layer block"
    batch, seq, d = x.shape
    t = batch * seq
    d_ssm = w_glu.shape[1]
    d_attn = w_up_attn.shape[1]
    n_heads = d_attn // HEAD_DIM
    o1, o2, o3, o4 = d_ssm, d_ssm + d_attn, d_ssm + 2 * d_attn, d_ssm + 3 * d_attn
    x2 = x.reshape(t, d)

    wl = w_in[0]
    w_main = jnp.concatenate([wl[:, :o1], wl[:, o2:o3], wl[:, o4:]], axis=1).astype(BF16)
    wqT = (wl[:, o1:o2] * (HEAD_DIM ** -0.5 * LOG2E)).T.astype(BF16)
    wv = wl[:, o3:o4].T.reshape(n_heads, HEAD_DIM, d)
    wvT = jnp.concatenate([wv, jnp.zeros((n_heads, BF16_ROWS, d), F32)], axis=1)
    wvT = wvT.reshape(n_heads * V_ROWS, d).astype(BF16)
    vb = jnp.concatenate([jnp.zeros((n_heads, HEAD_DIM, 1), F32), jnp.ones((n_heads, BF16_ROWS, 1), F32)],
                         axis=1).reshape(n_heads * V_ROWS, 1)
    u, k3, kmean, qT, vT3, gates = _inproj(
        x2, ln1_g[0][None], w_main, wqT, wvT, vb, b_gate[0][None], batch=batch, seq=seq, tm=512)

    bb, cc, ar, ai = _ssm_params(ssm_lambda_re[0], ssm_lambda_im[0], ssm_log_step[0],
                                 ssm_b_re[0], ssm_b_im[0], ssm_c_re[0], ssm_c_im[0])
    y_ssm = _s5(u.reshape(batch, seq, d_ssm), bb, cc, ar, ai, ssm_d[0].reshape(1, d_ssm),
                w_glu[0].astype(BF16), b_glu[0][None], ts=128, chunk=512)

    bias, far = _bias_tables(rel_bias)
    y_attn = _moba(far, qT, k3, vT3, kmean.reshape(batch, seq // MOBA_BLOCK, d_attn), bias,
                   batch=batch, seq=seq)

    wr = jnp.zeros((d, ROUTER_LANES), F32)
    wr = wr.at[:, :N_GROUPS].set(w_router_group[0])
    wr = wr.at[:, EXPERT_LANE0:EXPERT_LANE0 + N_EXPERTS].set(w_router_expert[0]).astype(BF16)
    br = jnp.zeros((1, ROUTER_LANES), F32)
    br = br.at[0, :N_GROUPS].set(b_router_group[0])
    br = br.at[0, EXPERT_LANE0:EXPERT_LANE0 + N_EXPERTS].set(b_router_expert[0])
    x1e, cls, rank, cnt = _merge(
        x2, y_ssm.reshape(t, d_ssm), y_attn.reshape(t, d_attn), gates,
        w_up_ssm[0].astype(BF16), w_up_attn[0].astype(BF16), w_out[0].astype(BF16),
        ln2_g[0][None], wr, br, tm=512)

    pos, items = _routing_tables(cls, rank, cnt, rows=MOE_ROWS)
    x1s = _permute_rows(pos, x1e, scatter=True)
    ys = _moe(items, x1s, w1[0].astype(BF16), w3[0].astype(BF16), w2[0].astype(BF16),
              ln2_g[0][None], ln_f_g[None], rows=MOE_ROWS)
    y = _permute_rows(pos, ys, scatter=False)
    return y.reshape(batch, seq, d)
```

```python
import functools
import math

import jax
import jax.numpy as jnp
from jax import lax
from jax.experimental import pallas as pl
from jax.experimental.pallas import tpu as pltpu
from jax.experimental.pallas import tpu_sc as plsc

F32 = jnp.float32
BF16 = jnp.bfloat16

SSM_GROUP = 16
SSM_STATE = 64
HEAD_DIM = 64
MOBA_BLOCK = 256
MOBA_TOP_K = 3
NUM_BUCKETS = 32
MAX_DISTANCE = 128
N_GROUPS = 4
EXPERTS_PER_GROUP = 8
N_EXPERTS = N_GROUPS * EXPERTS_PER_GROUP
PAIRS_PER_GROUP = EXPERTS_PER_GROUP * (EXPERTS_PER_GROUP - 1) // 2
N_CLASSES = N_GROUPS * PAIRS_PER_GROUP
MOE_ROWS = 256
SC_WINDOW = 32
SSM_PARTS = 2
MOBA_HEADS_PER_STEP = 8
RMS_EPS = 1e-6
NEG_INF = -1e30
LOG2E = 1.4426950408889634

LANES = 128
MXU_DEPTH = 256
BF16_ROWS = 16
V_ROWS = HEAD_DIM + BF16_ROWS
ROUTER_LANES = LANES
EXPERT_LANE0 = N_GROUPS
VMEM_LIMIT = 56 * 1024 * 1024


def _sigmoid(x):
    return 1.0 / (1.0 + jnp.exp(-x))


def _rms(x, g):
    ms = jnp.mean(x * x, axis=-1, keepdims=True)
    return x * lax.rsqrt(ms + RMS_EPS) * g


def _const_spec(shape):
    n = len(shape)
    return pl.BlockSpec(shape, lambda *_: (0,) * n)


def _inproj_kernel(x_ref, g_ref, w_ref, wqT_ref, wvT_ref, vb_ref, bg_ref,
                   u_ref, k_ref, kmean_ref, qT_ref, vT_ref, gates_ref, *, nblk, d_ssm, d_attn):
    hn = _rms(x_ref[...], g_ref[...]).astype(BF16)
    acc = jnp.dot(hn, w_ref[...], preferred_element_type=F32)
    u_ref[...] = acc[:, :d_ssm]
    kf = acc[:, d_ssm:d_ssm + d_attn].reshape(nblk, MOBA_BLOCK, d_attn)
    k_ref[...] = kf.astype(BF16)
    kmean_ref[...] = jnp.mean(kf, axis=1, keepdims=True)
    gates_ref[...] = (acc[:, d_ssm + d_attn:] + bg_ref[...]).astype(BF16)
    nt = (((1,), (1,)), ((), ()))
    qT = lax.dot_general(wqT_ref[...], hn, nt, preferred_element_type=F32)
    qT_ref[0] = qT.astype(BF16)
    vT = lax.dot_general(wvT_ref[...], hn, nt, preferred_element_type=F32) + vb_ref[...]
    for j in range(nblk):
        vT_ref[j] = vT[:, j * MOBA_BLOCK:(j + 1) * MOBA_BLOCK].astype(BF16)


def _inproj(x2, g1, w_main, wqT, wvT, vb, bg, *, batch, seq, tm):
    t, d = x2.shape
    d_attn = wqT.shape[0]
    v_rows = wvT.shape[0]
    d_ssm = w_main.shape[1] - d_attn - bg.shape[1]
    nblk = tm // MOBA_BLOCK
    tiles_per_seq = seq // tm
    nb_total = t // MOBA_BLOCK
    kern = functools.partial(_inproj_kernel, nblk=nblk, d_ssm=d_ssm, d_attn=d_attn)
    return pl.pallas_call(
        kern,
        grid=(t // tm,),
        in_specs=[
            pl.BlockSpec((tm, d), lambda i: (i, 0)),
            _const_spec(g1.shape),
            _const_spec(w_main.shape),
            _const_spec(wqT.shape),
            _const_spec(wvT.shape),
            _const_spec(vb.shape),
            _const_spec(bg.shape),
        ],
        out_specs=[
            pl.BlockSpec((tm, d_ssm), lambda i: (i, 0)),
            pl.BlockSpec((nblk, MOBA_BLOCK, d_attn), lambda i: (i, 0, 0)),
            pl.BlockSpec((nblk, 1, d_attn), lambda i: (i, 0, 0)),
            pl.BlockSpec((1, d_attn, tm), lambda i: (i // tiles_per_seq, 0, i % tiles_per_seq)),
            pl.BlockSpec((nblk, v_rows, MOBA_BLOCK), lambda i: (i, 0, 0)),
            pl.BlockSpec((tm, bg.shape[1]), lambda i: (i, 0)),
        ],
        out_shape=[
            jax.ShapeDtypeStruct((t, d_ssm), F32),
            jax.ShapeDtypeStruct((nb_total, MOBA_BLOCK, d_attn), BF16),
            jax.ShapeDtypeStruct((nb_total, 1, d_attn), F32),
            jax.ShapeDtypeStruct((batch, d_attn, seq), BF16),
            jax.ShapeDtypeStruct((nb_total, v_rows, MOBA_BLOCK), BF16),
            jax.ShapeDtypeStruct((t, bg.shape[1]), BF16),
        ],
        compiler_params=pltpu.CompilerParams(
            dimension_semantics=("arbitrary",), vmem_limit_bytes=VMEM_LIMIT),
    )(x2, g1, w_main, wqT, wvT, vb, bg)


def _s5_kernel(u_ref, bb_ref, cc_ref, ar_ref, ai_ref, d_ref, wglu_ref, bglu_ref,
               y_ref, utb_ref, sbuf_ref, state_ref, ytb_ref, *, batch, ts, n_state, chunk):
    d_ssm = u_ref.shape[2]
    n_slab = d_ssm // LANES

    @pl.when(pl.program_id(0) == 0)
    def _():
        state_ref[...] = jnp.zeros_like(state_ref)

    for b in range(batch):
        ub = u_ref[b]
        for j in range(n_slab):
            utb_ref[j, pl.ds(b, ts, stride=batch), :] = ub[:, j * LANES:(j + 1) * LANES]
    u_tb = jnp.concatenate([utb_ref[j] for j in range(n_slab)], axis=1)
    n_part, pc, pw = bb_ref.shape
    ps = pw // 2
    u_bf = u_tb.astype(BF16)
    for h in range(n_part):
        sbuf_ref[:, h * pw:(h + 1) * pw] = jnp.dot(u_bf[:, h * pc:(h + 1) * pc], bb_ref[h],
                                                   preferred_element_type=F32)

    for c in range(n_state // chunk):
        h, cc_ = divmod(c * chunk, ps)
        re = slice(h * pw + cc_, h * pw + cc_ + chunk)
        im = slice(h * pw + ps + cc_, h * pw + ps + cc_ + chunk)
        ar = jnp.broadcast_to(ar_ref[:, c * chunk:(c + 1) * chunk], (batch, chunk))
        ai = jnp.broadcast_to(ai_ref[:, c * chunk:(c + 1) * chunk], (batch, chunk))

        def step(t, carry, re=re, im=im, ar=ar, ai=ai):
            hr, hi = carry
            r0 = pl.multiple_of(t * batch, batch)
            nhr = ar * hr - ai * hi + sbuf_ref[pl.ds(r0, batch), re]
            nhi = ar * hi + ai * hr + sbuf_ref[pl.ds(r0, batch), im]
            sbuf_ref[pl.ds(r0, batch), re] = nhr
            sbuf_ref[pl.ds(r0, batch), im] = nhi
            return nhr, nhi

        hr, hi = lax.fori_loop(0, ts, step, (state_ref[:, re], state_ref[:, im]), unroll=8)
        state_ref[:, re] = hr
        state_ref[:, im] = hi

    y = jnp.concatenate(
        [jnp.dot(sbuf_ref[:, h * pw:(h + 1) * pw].astype(BF16), cc_ref[h], preferred_element_type=F32)
         for h in range(n_part)], axis=1)
    y = y + d_ref[...] * u_tb
    g = jax.nn.gelu(y)
    z = jnp.dot(g.astype(BF16), wglu_ref[...], preferred_element_type=F32) + bglu_ref[...]
    out = g * _sigmoid(z)
    for j in range(n_slab):
        ytb_ref[j] = out[:, j * LANES:(j + 1) * LANES]
    for b in range(batch):
        y_ref[b] = jnp.concatenate(
            [ytb_ref[j, pl.ds(b, ts, stride=batch), :] for j in range(n_slab)], axis=1).astype(BF16)


def _s5(u3, bb, cc, ar, ai, dvec, wglu, bglu, *, ts, chunk):
    batch, seq, d_ssm = u3.shape
    n_state = ar.shape[1]
    kern = functools.partial(_s5_kernel, batch=batch, ts=ts, n_state=n_state, chunk=chunk)
    return pl.pallas_call(
        kern,
        grid=(seq // ts,),
        in_specs=[
            pl.BlockSpec((batch, ts, d_ssm), lambda i: (0, i, 0)),
            _const_spec(bb.shape), _const_spec(cc.shape), _const_spec(ar.shape), _const_spec(ai.shape),
            _const_spec(dvec.shape), _const_spec(wglu.shape), _const_spec(bglu.shape),
        ],
        out_specs=pl.BlockSpec((batch, ts, d_ssm), lambda i: (0, i, 0)),
        out_shape=jax.ShapeDtypeStruct((batch, seq, d_ssm), BF16),
        scratch_shapes=[
            pltpu.VMEM((d_ssm // LANES, ts * batch, LANES), F32),
            pltpu.VMEM((ts * batch, 2 * n_state), F32),
            pltpu.VMEM((batch, 2 * n_state), F32),
            pltpu.VMEM((d_ssm // LANES, ts * batch, LANES), F32),
        ],
        compiler_params=pltpu.CompilerParams(
            dimension_semantics=("arbitrary",), vmem_limit_bytes=VMEM_LIMIT),
    )(u3, bb, cc, ar, ai, dvec, wglu, bglu)


def _select_blocks(gate, blk_f, n_pick):
    sel = jnp.zeros(gate.shape, jnp.bool_)
    for _ in range(n_pick):
        mx = jnp.max(gate, axis=0, keepdims=True)
        idx = jnp.min(jnp.where(gate == mx, blk_f, float(gate.shape[0])), axis=0, keepdims=True)
        hit = blk_f == idx
        sel = sel | hit
        gate = jnp.where(hit, -jnp.inf, gate)
    return sel


def _moba_kernel(farb_ref, qT_ref, k_ref, vT_ref, kmean_ref, bias_ref, o_ref,
                 mask_ref, s_ref, p0_ref, p1_ref, acc_ref, *, nb):
    hg = pl.program_id(1)
    i = pl.program_id(2)
    tq = qT_ref.shape[2]
    hw = qT_ref.shape[1]
    nh = hw // HEAD_DIM
    qT = qT_ref[0]
    row = lax.broadcasted_iota(jnp.int32, qT.shape, 0)
    km = kmean_ref[0]
    km_hi = km.astype(BF16)
    km_lo = (km - km_hi.astype(F32)).astype(BF16)
    blk = lax.broadcasted_iota(jnp.int32, (nb, tq), 0)
    blk_f = blk.astype(F32)
    prev = jnp.maximum(i - 1, 0)
    n_far = jnp.maximum(i - 1, 0)
    n_steps = (n_far + 1) // 2
    n_iter = n_steps // 2

    gw = min(hw, MXU_DEPTH)
    groups = [slice(j * HEAD_DIM // gw * gw, j * HEAD_DIM // gw * gw + gw) for j in range(nh)]

    def qk(j, keys):
        return jnp.dot(keys[:, groups[j]], qTs[j], preferred_element_type=F32)

    qTs, near_masks = [], []
    for j in range(nh):
        qTh = jnp.where((row >= j * HEAD_DIM) & (row < (j + 1) * HEAD_DIM), qT, jnp.zeros_like(qT))
        qTs.append(qTh[groups[j]])
        gate = qk(j, km_hi) + qk(j, km_lo)
        gate = jnp.where(blk < i, gate, NEG_INF)
        sel = _select_blocks(gate, blk_f, MOBA_TOP_K) & (blk < i)
        mask_ref[j] = jnp.where(sel & (blk < i - 1), farb_ref[nh * hg + j], NEG_INF)
        mprev = jnp.max(jnp.where(sel & (blk == i - 1), 0.0, NEG_INF), axis=0, keepdims=True)
        near_masks.append((mprev, jnp.zeros_like(mprev)))

    def far_keys(n0):
        n0 = pl.multiple_of(n0, 2)
        return k_ref[pl.ds(n0, 2)].reshape(2 * MOBA_BLOCK, hw)

    def far_masks(j, n0):
        return mask_ref[j, pl.ds(n0, 1), :], mask_ref[j, pl.ds(n0 + 1, 1), :]

    def block_max(s, masks):
        return jnp.maximum(jnp.max(s[:MOBA_BLOCK], axis=0, keepdims=True) + masks[0],
                           jnp.max(s[MOBA_BLOCK:], axis=0, keepdims=True) + masks[1])

    def probs(j, p_ref, s, shift, masks):
        p_ref[j, :MOBA_BLOCK] = jnp.exp2((s[:MOBA_BLOCK] - (shift - masks[0])).astype(BF16))
        p_ref[j, MOBA_BLOCK:] = jnp.exp2((s[MOBA_BLOCK:] - (shift - masks[1])).astype(BF16))

    def pv(j, p_ref, va, vb):
        vrows = slice(j * V_ROWS, (j + 1) * V_ROWS)
        vv = jnp.concatenate([va[vrows, :], vb[vrows, :]], axis=1)
        acc_ref[j] += jnp.dot(vv, p_ref[j], preferred_element_type=F32)

    near_keys = jnp.concatenate([k_ref[prev], k_ref[i]], axis=0)
    near_max = []
    for j in range(nh):
        s = qk(j, near_keys) + bias_ref[j]
        s_ref[j] = s
        near_max.append(block_max(s, near_masks[j]))

    def attend(shift):
        acc_ref[...] = jnp.zeros_like(acc_ref)
        first = far_keys(0)
        for j in range(nh):
            probs(j, p0_ref, s_ref[j], shift[j], near_masks[j])
            probs(j, p1_ref, qk(j, first), shift[j], far_masks(j, 0))

        def trip(g, carry):
            f0 = 4 * g
            n2 = f0 + 2
            n3 = jnp.minimum(f0 + 4, nb - 2)
            va0 = vT_ref[jnp.where(g == 0, prev, f0 - 2)]
            vb0 = vT_ref[jnp.where(g == 0, i, f0 - 1)]
            va1 = vT_ref[f0]
            vb1 = vT_ref[f0 + 1]
            k2 = far_keys(n2)
            k3_ = far_keys(n3)
            for j in range(nh):
                pv(j, p0_ref, va0, vb0)
                probs(j, p0_ref, qk(j, k2), shift[j], far_masks(j, n2))
                pv(j, p1_ref, va1, vb1)
                probs(j, p1_ref, qk(j, k3_), shift[j], far_masks(j, n3))
            return carry

        lax.fori_loop(0, n_iter, trip, 0)
        last = 4 * n_iter
        va = vT_ref[jnp.where(n_iter == 0, prev, last - 2)]
        vb = vT_ref[jnp.where(n_iter == 0, i, last - 1)]
        for j in range(nh):
            pv(j, p0_ref, va, vb)

        @pl.when(n_steps % 2 == 1)
        def _():
            va1 = vT_ref[last]
            vb1 = vT_ref[last + 1]
            for j in range(nh):
                pv(j, p1_ref, va1, vb1)

    attend(near_max)
    acc = acc_ref[...]
    overflow = jnp.max(jnp.where(jnp.isfinite(acc), 0.0, 1.0)) > 0.0

    @pl.when(overflow)
    def _():
        def far_max(n, m_run):
            kp = k_ref[n]
            return tuple(jnp.maximum(m_run[j], jnp.max(qk(j, kp), axis=0, keepdims=True) + mask_ref[j, pl.ds(n, 1), :])
                         for j in range(nh))

        attend(lax.fori_loop(0, n_far, far_max, tuple(near_max)))

    outs = [acc_ref[j][:HEAD_DIM] / acc_ref[j][HEAD_DIM:HEAD_DIM + 1] for j in range(nh)]
    o_ref[0] = jnp.concatenate(outs, axis=0).T.astype(BF16)


def _moba(farb, qT, k3, vT3, kmean, bias, *, batch, seq):
    nb = seq // MOBA_BLOCK
    assert nb % 4 == 0, "far blocks are consumed four per loop trip"
    d_attn = qT.shape[1]
    nh = MOBA_HEADS_PER_STEP
    hw = nh * HEAD_DIM
    kern = functools.partial(_moba_kernel, nb=nb)
    return pl.pallas_call(
        kern,
        grid=(batch, d_attn // hw, nb),
        in_specs=[
            pl.BlockSpec(memory_space=pltpu.SMEM),
            pl.BlockSpec((1, hw, MOBA_BLOCK), lambda b, h, i: (b, h, i)),
            pl.BlockSpec((nb, MOBA_BLOCK, hw), lambda b, h, i: (b, 0, h), pipeline_mode=pl.Buffered(1)),
            pl.BlockSpec((nb, nh * V_ROWS, MOBA_BLOCK), lambda b, h, i: (b, h, 0), pipeline_mode=pl.Buffered(1)),
            pl.BlockSpec((1, nb, hw), lambda b, h, i: (b, 0, h)),
            pl.BlockSpec((nh, 2 * MOBA_BLOCK, MOBA_BLOCK), lambda b, h, i: (h, 0, 0), pipeline_mode=pl.Buffered(1)),
        ],
        out_specs=pl.BlockSpec((1, MOBA_BLOCK, hw), lambda b, h, i: (b, i, h)),
        out_shape=jax.ShapeDtypeStruct((batch, seq, d_attn), BF16),
        scratch_shapes=[
            pltpu.VMEM((nh, nb, MOBA_BLOCK), F32),
            pltpu.VMEM((nh, 2 * MOBA_BLOCK, MOBA_BLOCK), F32),
            pltpu.VMEM((nh, 2 * MOBA_BLOCK, MOBA_BLOCK), BF16),
            pltpu.VMEM((nh, 2 * MOBA_BLOCK, MOBA_BLOCK), BF16),
            pltpu.VMEM((nh, V_ROWS, MOBA_BLOCK), F32),
        ],
        compiler_params=pltpu.CompilerParams(
            dimension_semantics=("arbitrary", "arbitrary", "arbitrary"),
            vmem_limit_bytes=VMEM_LIMIT),
    )(farb, qT, k3, vT3, kmean, bias)


def _route(logits):
    lane = lax.broadcasted_iota(jnp.int32, logits.shape, 1)
    lane_f = lane.astype(F32)
    big = float(ROUTER_LANES)
    ninf = -jnp.inf
    gmask = lane < N_GROUPS
    gmax = jnp.max(jnp.where(gmask, logits, ninf), axis=1, keepdims=True)
    gsum = jnp.sum(jnp.where(gmask, jnp.exp(logits - gmax), 0.0), axis=1, keepdims=True)
    g_val = 1.0 / gsum
    g_idx = jnp.min(jnp.where(gmask & (logits == gmax), lane_f, big), axis=1, keepdims=True)
    e_grp = ((lane - EXPERT_LANE0) >> 3).astype(F32)
    emask = (lane >= EXPERT_LANE0) & (lane < EXPERT_LANE0 + N_EXPERTS) & (e_grp == g_idx)
    el = jnp.where(emask, logits, ninf)
    e1 = jnp.max(el, axis=1, keepdims=True)
    i1 = jnp.min(jnp.where(emask & (el == e1), lane_f, big), axis=1, keepdims=True)
    el2 = jnp.where(lane_f == i1, ninf, el)
    e2 = jnp.max(el2, axis=1, keepdims=True)
    i2 = jnp.min(jnp.where(emask & (lane_f != i1) & (el2 == e2), lane_f, big), axis=1, keepdims=True)
    tt = jnp.exp(e2 - e1)
    w1 = g_val / (1.0 + tt)
    w2 = g_val * tt / (1.0 + tt)
    first_low = i1 < i2
    lo = jnp.minimum(i1, i2) - (EXPERT_LANE0 + EXPERTS_PER_GROUP * g_idx)
    hi = jnp.maximum(i1, i2) - (EXPERT_LANE0 + EXPERTS_PER_GROUP * g_idx)
    pair = lo * (2 * EXPERTS_PER_GROUP - 1 - lo) * 0.5 + (hi - lo - 1.0)
    cls = g_idx * PAIRS_PER_GROUP + pair
    return cls, jnp.where(first_low, w1, w2), jnp.where(first_low, w2, w1)


def _col_to_row(col):
    n = col.shape[0]
    r = lax.broadcasted_iota(jnp.int32, (n, n), 0)
    c = lax.broadcasted_iota(jnp.int32, (n, n), 1)
    return jnp.sum(jnp.where(r == c, col, 0.0), axis=0, keepdims=True)


def _merge_kernel(x_ref, ys_ref, ya_ref, gates_ref, wus_ref, wua_ref, wout_ref, g2_ref, wr_ref, br_ref,
                  tri_ref, x1e_ref, cls_ref, rank_ref, cnt_ref):
    d = x_ref.shape[1]
    a = jnp.dot(ys_ref[...], wus_ref[...], preferred_element_type=F32)
    b = jnp.dot(ya_ref[...], wua_ref[...], preferred_element_type=F32)
    gts = gates_ref[...].astype(F32)
    merged = _sigmoid(gts[:, :d]) * a + _sigmoid(gts[:, d:]) * b
    x1 = x_ref[...] + jnp.dot(merged.astype(BF16), wout_ref[...], preferred_element_type=F32)
    xn = _rms(x1, g2_ref[...]).astype(BF16)
    logits = jnp.dot(xn, wr_ref[...], preferred_element_type=F32) + br_ref[...]
    cls, wa, wb = _route(logits)
    lane = lax.broadcasted_iota(jnp.int32, logits.shape, 1)
    x1e_ref[:, :d] = x1
    x1e_ref[:, d:] = jnp.where(lane == 0, wa, jnp.where(lane == 1, wb, 0.0))
    onehot = lane.astype(F32) == cls
    before = jnp.dot(tri_ref[...], onehot.astype(BF16), preferred_element_type=F32)
    rank = jnp.sum(jnp.where(onehot, before, 0.0), axis=1, keepdims=True)
    cls_ref[0] = _col_to_row(cls).astype(jnp.int32)
    rank_ref[0] = _col_to_row(rank).astype(jnp.int32)
    cnt_ref[0] = jnp.sum(onehot.astype(F32), axis=0, keepdims=True).astype(jnp.int32)


def _merge(x2, ys, ya, gates, wus, wua, wout, g2, wr, br, *, tm):
    t, d = x2.shape
    nt = t // tm
    tri = jnp.tril(jnp.ones((tm, tm), F32), -1).astype(BF16)
    return pl.pallas_call(
        _merge_kernel,
        grid=(nt,),
        in_specs=[
            pl.BlockSpec((tm, d), lambda i: (i, 0)),
            pl.BlockSpec((tm, ys.shape[1]), lambda i: (i, 0)),
            pl.BlockSpec((tm, ya.shape[1]), lambda i: (i, 0)),
            pl.BlockSpec((tm, gates.shape[1]), lambda i: (i, 0)),
            _const_spec(wus.shape), _const_spec(wua.shape), _const_spec(wout.shape),
            _const_spec(g2.shape), _const_spec(wr.shape), _const_spec(br.shape), _const_spec(tri.shape),
        ],
        out_specs=[
            pl.BlockSpec((tm, d + ROUTER_LANES), lambda i: (i, 0)),
            pl.BlockSpec((1, 1, tm), lambda i: (i, 0, 0)),
            pl.BlockSpec((1, 1, tm), lambda i: (i, 0, 0)),
            pl.BlockSpec((1, 1, ROUTER_LANES), lambda i: (i, 0, 0)),
        ],
        out_shape=[
            jax.ShapeDtypeStruct((t, d + ROUTER_LANES), F32),
            jax.ShapeDtypeStruct((nt, 1, tm), jnp.int32),
            jax.ShapeDtypeStruct((nt, 1, tm), jnp.int32),
            jax.ShapeDtypeStruct((nt, 1, ROUTER_LANES), jnp.int32),
        ],
        compiler_params=pltpu.CompilerParams(
            dimension_semantics=("arbitrary",), vmem_limit_bytes=VMEM_LIMIT),
    )(x2, ys, ya, gates, wus, wua, wout, g2, wr, br, tri)


def _permute_rows(pos, src, *, scatter):
    n, width = src.shape
    w = SC_WINDOW
    idx = jnp.pad(pos.reshape(n // w, w), ((0, 0), (0, LANES - w)))
    mesh = plsc.VectorSubcoreMesh(core_axis_name="core", subcore_axis_name="subcore")
    idx_spec = pl.BlockSpec((1, LANES), index_map=lambda i: (i, 0))
    row_spec = pl.BlockSpec((w, width), index_map=lambda i: (i, 0))

    @pl.kernel(out_type=jax.ShapeDtypeStruct(src.shape, src.dtype), mesh=mesh, scratch_types=[])
    def permute(src_hbm, idx_hbm, dst_hbm):
        if scatter:
            def body(rows_vmem, idx_vmem):
                pltpu.sync_copy(rows_vmem, dst_hbm.at[idx_vmem.at[0, pl.ds(0, w)]])
            in_specs, out_specs, args = [row_spec, idx_spec], [], (src_hbm, idx_hbm)
        else:
            def body(idx_vmem, rows_vmem):
                pltpu.sync_copy(src_hbm.at[idx_vmem.at[0, pl.ds(0, w)]], rows_vmem)
            in_specs, out_specs, args = [idx_spec], [row_spec], (idx_hbm, dst_hbm)
        pltpu.emit_pipeline(body, grid=(n // w,), in_specs=in_specs, out_specs=out_specs,
                            core_axis_name=("core", "subcore"),
                            dimension_semantics=(pltpu.PARALLEL,))(*args)

    return permute(src, idx)


def _expert(xn, w1_ref, w3_ref, w2_ref):
    h1 = jnp.dot(xn, w1_ref[0], preferred_element_type=F32)
    h3 = jnp.dot(xn, w3_ref[0], preferred_element_type=F32)
    hid = (h1 * _sigmoid(h1) * h3).astype(BF16)
    return jnp.dot(hid, w2_ref[0], preferred_element_type=F32)


def _moe_kernel(tile_ref, ea_ref, eb_ref, lo_ref, hi_ref, first_ref, last_ref,
                x_ref, w1a_ref, w3a_ref, w2a_ref, w1b_ref, w3b_ref, w2b_ref, g2_ref, gf_ref,
                y_ref, acc_ref):
    w = pl.program_id(0)
    d = y_ref.shape[1]
    lo = lo_ref[w]
    hi = hi_ref[w]

    @pl.when(hi > lo)
    def _():
        x1 = x_ref[:, :d]
        row = lax.broadcasted_iota(jnp.int32, (x1.shape[0], 1), 0)
        inseg = (row >= lo) & (row < hi)
        ca = jnp.where(inseg, x_ref[:, d:d + 1], 0.0)
        cb = jnp.where(inseg, x_ref[:, d + 1:d + 2], 0.0)
        xn = _rms(x1, g2_ref[...]).astype(BF16)
        contrib = (ca * _expert(xn, w1a_ref, w3a_ref, w2a_ref)
                   + cb * _expert(xn, w1b_ref, w3b_ref, w2b_ref))

        @pl.when(first_ref[w] == 1)
        def _():
            acc_ref[...] = contrib

        @pl.when(first_ref[w] == 0)
        def _():
            acc_ref[...] += contrib

        @pl.when(last_ref[w] == 1)
        def _():
            y_ref[...] = _rms(x1 + acc_ref[...], gf_ref[...])


def _moe(items, x1s, w1, w3, w2, g2, gf, *, rows):
    t, de_ = x1s.shape
    d = g2.shape[1]
    ne, _, de = w1.shape
    n_items = items[0].shape[0]

    def wa_map(w, tile, ea, eb, lo, hi, first, last):
        return (ea[w], 0, 0)

    def wb_map(w, tile, ea, eb, lo, hi, first, last):
        return (eb[w], 0, 0)

    def row_map(w, tile, ea, eb, lo, hi, first, last):
        return (tile[w], 0)

    def const_map(w, *_):
        return (0, 0)

    grid_spec = pltpu.PrefetchScalarGridSpec(
        num_scalar_prefetch=len(items),
        grid=(n_items,),
        in_specs=[
            pl.BlockSpec((rows, de_), row_map),
            pl.BlockSpec((1, d, de), wa_map), pl.BlockSpec((1, d, de), wa_map), pl.BlockSpec((1, de, d), wa_map),
            pl.BlockSpec((1, d, de), wb_map), pl.BlockSpec((1, d, de), wb_map), pl.BlockSpec((1, de, d), wb_map),
            pl.BlockSpec(g2.shape, const_map), pl.BlockSpec(gf.shape, const_map),
        ],
        out_specs=pl.BlockSpec((rows, d), row_map),
        scratch_shapes=[pltpu.VMEM((rows, d), F32)],
    )
    return pl.pallas_call(
        _moe_kernel,
        grid_spec=grid_spec,
        out_shape=jax.ShapeDtypeStruct((t, d), F32),
        compiler_params=pltpu.CompilerParams(
            dimension_semantics=("arbitrary",), vmem_limit_bytes=VMEM_LIMIT),
    )(*items, x1s, w1, w3, w2, w1, w3, w2, g2, gf)


def _pair_table():
    lo, hi = [], []
    for a in range(EXPERTS_PER_GROUP):
        for b in range(a + 1, EXPERTS_PER_GROUP):
            lo.append(a)
            hi.append(b)
    return jnp.asarray(lo, jnp.int32), jnp.asarray(hi, jnp.int32)


def _before_sum(v):
    ix = jnp.arange(v.shape[0])
    return jnp.sum(jnp.where(ix[None, :] < ix[:, None], v[None, :], 0), axis=1)


def _place(dest, vals, n):
    return jnp.sum(jnp.where(dest[None, :] == jnp.arange(n)[:, None], vals[None, :], 0), axis=1)


def _routing_tables(cls, rank, cnt, *, rows):
    nt, _, tm = cls.shape
    t = nt * tm
    lanes = cnt.shape[2]
    c = cnt[:, 0, :]
    tot = jnp.sum(c, axis=0)
    class_off = _before_sum(tot)
    tile_ix = jnp.arange(nt)
    tiles_before = jnp.sum(jnp.where((tile_ix[None, :] < tile_ix[:, None])[:, :, None], c[None], 0), axis=1)
    base = class_off[None, :] + tiles_before
    onehot = cls[:, 0, :, None] == jnp.arange(lanes)[None, None, :]
    pos = jnp.sum(jnp.where(onehot, base[:, None, :], 0), axis=2) + rank[:, 0, :]

    n_tiles = t // rows
    ta = jnp.arange(n_tiles, dtype=jnp.int32) * rows
    cb = class_off[1:N_CLASSES]
    n_items = n_tiles + N_CLASSES - 1
    idx = jnp.arange(n_items)
    at_a = jnp.arange(n_tiles) + jnp.sum(cb[None, :] <= ta[:, None], axis=1)
    at_b = jnp.arange(N_CLASSES - 1) + jnp.sum(ta[None, :] < cb[:, None], axis=1)
    starts = _place(at_a, ta, n_items) + _place(at_b, cb, n_items)
    ends = jnp.concatenate([starts[1:], jnp.full((1,), t, jnp.int32)])
    real = ends > starts
    n_real = jnp.sum(real)
    dest = jnp.where(real, _before_sum(real.astype(jnp.int32)), n_real + _before_sum(1 - real.astype(jnp.int32)))
    starts, ends = _place(dest, starts, n_items), _place(dest, ends, n_items)
    valid = idx < n_real
    last_real = jnp.maximum(n_real - 1, 0)
    starts = jnp.where(valid, starts, jnp.sum(jnp.where(idx == last_real, starts, 0)))
    tile = jnp.minimum(starts // rows, n_tiles - 1)
    klass = jnp.sum(class_off[None, :N_CLASSES] <= starts[:, None], axis=1) - 1
    pair_lo, pair_hi = _pair_table()
    grp = klass // PAIRS_PER_GROUP
    pair_hot = (klass % PAIRS_PER_GROUP)[:, None] == jnp.arange(PAIRS_PER_GROUP)[None, :]
    ea = grp * EXPERTS_PER_GROUP + jnp.sum(jnp.where(pair_hot, pair_lo[None, :], 0), axis=1)
    eb = grp * EXPERTS_PER_GROUP + jnp.sum(jnp.where(pair_hot, pair_hi[None, :], 0), axis=1)
    lo = jnp.where(valid, starts - tile * rows, 0)
    hi = jnp.where(valid, ends - tile * rows, 0)
    tile_prev = jnp.concatenate([tile[:1], tile[:-1]])
    tile_next = jnp.concatenate([tile[1:], tile[-1:]])
    first = valid & ((idx == 0) | (tile != tile_prev))
    last = valid & ((idx == last_real) | (tile != tile_next))
    items = tuple(a.astype(jnp.int32) for a in (tile, ea, eb, lo, hi, first, last))
    return pos.reshape(t).astype(jnp.int32), items


def _t5_bucket(dist):
    n = jnp.maximum(dist, 0)
    max_exact = NUM_BUCKETS // 2
    large = max_exact + (jnp.log(jnp.maximum(n, max_exact).astype(F32) / max_exact)
                         / math.log(MAX_DISTANCE / max_exact)
                         * (NUM_BUCKETS - max_exact)).astype(jnp.int32)
    return jnp.where(n < max_exact, n, jnp.minimum(large, NUM_BUCKETS - 1))


def _bias_tables(rel_bias):
    kk = jnp.arange(MOBA_BLOCK)[:, None]
    qq = jnp.arange(MOBA_BLOCK)[None, :]
    rb = rel_bias.astype(F32) * LOG2E
    d_own = qq - kk

    def lookup(dist):
        oh = jax.nn.one_hot(_t5_bucket(dist), NUM_BUCKETS, dtype=F32)
        return jnp.einsum('hn,kqn->hkq', rb, oh, precision=lax.Precision.HIGHEST)

    own = jnp.where((d_own >= 0)[None], lookup(d_own), NEG_INF)
    prv = lookup(d_own + MOBA_BLOCK)
    tab = jnp.concatenate([prv, own], axis=1)
    far = rb[:, NUM_BUCKETS - 1]
    return tab, far


def _ssm_params(lam_re, lam_im, log_step, b_re, b_im, c_re, c_im):
    g, p = lam_re.shape
    step = jnp.exp(log_step)[:, None]
    decay = jnp.exp(lam_re * step)
    a_re = decay * jnp.cos(lam_im * step)
    a_im = decay * jnp.sin(lam_im * step)
    denom = lam_re * lam_re + lam_im * lam_im
    nr, ni = a_re - 1.0, a_im
    coef_re = (nr * lam_re + ni * lam_im) / denom
    coef_im = (ni * lam_re - nr * lam_im) / denom
    bb_re = coef_re[..., None] * b_re - coef_im[..., None] * b_im
    bb_im = coef_re[..., None] * b_im + coef_im[..., None] * b_re
    eye = jnp.eye(g, dtype=F32)
    hc = b_re.shape[2]

    def in_mat(m):
        return (eye[:, None, :, None] * m.transpose(0, 2, 1)[:, :, None, :]).reshape(g * hc, g * p)

    def out_mat(m):
        return (eye[:, None, :, None] * m.transpose(0, 2, 1)[:, :, None, :]).reshape(g * p, g * hc)

    pc, ps = g * hc // SSM_PARTS, g * p // SSM_PARTS
    in_re, in_im, out_re, out_im = in_mat(bb_re), in_mat(bb_im), out_mat(c_re), out_mat(c_im)
    bb = jnp.stack([jnp.concatenate([m[h * pc:(h + 1) * pc, h * ps:(h + 1) * ps] for m in (in_re, in_im)], axis=1)
                    for h in range(SSM_PARTS)]).astype(BF16)
    cc = jnp.stack([jnp.concatenate([m[h * ps:(h + 1) * ps, h * pc:(h + 1) * pc] for m in (out_re, -out_im)], axis=0)
                    for h in range(SSM_PARTS)]).astype(BF16)
    return bb, cc, a_re.reshape(1, g * p), a_im.reshape(1, g * p)


def kernel(x, ln1_g, w_in, b_gate, ssm_lambda_re, ssm_lambda_im, ssm_log_step, ssm_b_re, ssm_b_im,
           ssm_c_re, ssm_c_im, ssm_d, w_glu, b_glu, w_up_ssm, w_up_attn, rel_bias, w_out, ln2_g,
           w_router_group, b_router_group, w_router_expert, b_router_expert, w1, w3, w2, ln_f_g):
    assert w_in.shape[0] == 1, "single-layer block"
    batch, seq, d = x.shape
    t = batch * seq
    d_ssm = w_glu.shape[1]
    d_attn = w_up_attn.shape[1]
    n_heads = d_attn // HEAD_DIM
    o1, o2, o3, o4 = d_ssm, d_ssm + d_attn, d_ssm + 2 * d_attn, d_ssm + 3 * d_attn
    x2 = x.reshape(t, d)

    wl = w_in[0]
    w_main = jnp.concatenate([wl[:, :o1], wl[:, o2:o3], wl[:, o4:]], axis=1).astype(BF16)
    wqT = (wl[:, o1:o2] * (HEAD_DIM ** -0.5 * LOG2E)).T.astype(BF16)
    wv = wl[:, o3:o4].T.reshape(n_heads, HEAD_DIM, d)
    wvT = jnp.concatenate([wv, jnp.zeros((n_heads, BF16_ROWS, d), F32)], axis=1)
    wvT = wvT.reshape(n_heads * V_ROWS, d).astype(BF16)
    vb = jnp.concatenate([jnp.zeros((n_heads, HEAD_DIM, 1), F32), jnp.ones((n_heads, BF16_ROWS, 1), F32)],
                         axis=1).reshape(n_heads * V_ROWS, 1)
    u, k3, kmean, qT, vT3, gates = _inproj(
        x2, ln1_g[0][None], w_main, wqT, wvT, vb, b_gate[0][None], batch=batch, seq=seq, tm=512)

    bb, cc, ar, ai = _ssm_params(ssm_lambda_re[0], ssm_lambda_im[0], ssm_log_step[0],
                                 ssm_b_re[0], ssm_b_im[0], ssm_c_re[0], ssm_c_im[0])
    y_ssm = _s5(u.reshape(batch, seq, d_ssm), bb, cc, ar, ai, ssm_d[0].reshape(1, d_ssm),
                w_glu[0].astype(BF16), b_glu[0][None], ts=128, chunk=512)

    bias, far = _bias_tables(rel_bias)
    y_attn = _moba(far, qT, k3, vT3, kmean.reshape(batch, seq // MOBA_BLOCK, d_attn), bias,
                   batch=batch, seq=seq)

    wr = jnp.zeros((d, ROUTER_LANES), F32)
    wr = wr.at[:, :N_GROUPS].set(w_router_group[0])
    wr = wr.at[:, EXPERT_LANE0:EXPERT_LANE0 + N_EXPERTS].set(w_router_expert[0]).astype(BF16)
    br = jnp.zeros((1, ROUTER_LANES), F32)
    br = br.at[0, :N_GROUPS].set(b_router_group[0])
    br = br.at[0, EXPERT_LANE0:EXPERT_LANE0 + N_EXPERTS].set(b_router_expert[0])
    x1e, cls, rank, cnt = _merge(
        x2, y_ssm.reshape(t, d_ssm), y_attn.reshape(t, d_attn), gates,
        w_up_ssm[0].astype(BF16), w_up_attn[0].astype(BF16), w_out[0].astype(BF16),
        ln2_g[0][None], wr, br, tm=512)

    pos, items = _routing_tables(cls, rank, cnt, rows=MOE_ROWS)
    x1s = _permute_rows(pos, x1e, scatter=True)
    ys = _moe(items, x1s, w1[0].astype(BF16), w3[0].astype(BF16), w2[0].astype(BF16),
              ln2_g[0][None], ln_f_g[None], rows=MOE_ROWS)
    y = _permute_rows(pos, ys, scatter=False)
    return y.reshape(batch, seq, d)
```

```python
import functools
import math

import jax
import jax.numpy as jnp
from jax import lax
from jax.experimental import pallas as pl
from jax.experimental.pallas import tpu as pltpu
from jax.experimental.pallas import tpu_sc as plsc

F32 = jnp.float32
BF16 = jnp.bfloat16

SSM_GROUP = 16
SSM_STATE = 64
HEAD_DIM = 64
MOBA_BLOCK = 256
MOBA_TOP_K = 3
NUM_BUCKETS = 32
MAX_DISTANCE = 128
N_GROUPS = 4
EXPERTS_PER_GROUP = 8
N_EXPERTS = N_GROUPS * EXPERTS_PER_GROUP
PAIRS_PER_GROUP = EXPERTS_PER_GROUP * (EXPERTS_PER_GROUP - 1) // 2
N_CLASSES = N_GROUPS * PAIRS_PER_GROUP
MOE_ROWS = 256
SC_WINDOW = 32
SSM_PARTS = 2
MOBA_HEADS_PER_STEP = 8
RMS_EPS = 1e-6
NEG_INF = -1e30
LOG2E = 1.4426950408889634

LANES = 128
MXU_DEPTH = 256
BF16_ROWS = 16
V_ROWS = HEAD_DIM + BF16_ROWS
ROUTER_LANES = LANES
EXPERT_LANE0 = N_GROUPS
VMEM_LIMIT = 56 * 1024 * 1024


def _sigmoid(x):
    return 1.0 / (1.0 + jnp.exp(-x))


def _rms(x, g):
    ms = jnp.mean(x * x, axis=-1, keepdims=True)
    return x * lax.rsqrt(ms + RMS_EPS) * g


def _const_spec(shape):
    n = len(shape)
    return pl.BlockSpec(shape, lambda *_: (0,) * n)


def _inproj_kernel(x_ref, g_ref, w_ref, wqT_ref, wvT_ref, vb_ref, bg_ref,
                   u_ref, k_ref, kmean_ref, qT_ref, vT_ref, gates_ref, *, nblk, d_ssm, d_attn):
    hn = _rms(x_ref[...], g_ref[...]).astype(BF16)
    acc = jnp.dot(hn, w_ref[...], preferred_element_type=F32)
    u_ref[...] = acc[:, :d_ssm]
    kf = acc[:, d_ssm:d_ssm + d_attn].reshape(nblk, MOBA_BLOCK, d_attn)
    k_ref[...] = kf.astype(BF16)
    kmean_ref[...] = jnp.mean(kf, axis=1, keepdims=True)
    gates_ref[...] = (acc[:, d_ssm + d_attn:] + bg_ref[...]).astype(BF16)
    nt = (((1,), (1,)), ((), ()))
    qT = lax.dot_general(wqT_ref[...], hn, nt, preferred_element_type=F32)
    qT_ref[0] = qT.astype(BF16)
    vT = lax.dot_general(wvT_ref[...], hn, nt, preferred_element_type=F32) + vb_ref[...]
    for j in range(nblk):
        vT_ref[j] = vT[:, j * MOBA_BLOCK:(j + 1) * MOBA_BLOCK].astype(BF16)


def _inproj(x2, g1, w_main, wqT, wvT, vb, bg, *, batch, seq, tm):
    t, d = x2.shape
    d_attn = wqT.shape[0]
    v_rows = wvT.shape[0]
    d_ssm = w_main.shape[1] - d_attn - bg.shape[1]
    nblk = tm // MOBA_BLOCK
    tiles_per_seq = seq // tm
    nb_total = t // MOBA_BLOCK
    kern = functools.partial(_inproj_kernel, nblk=nblk, d_ssm=d_ssm, d_attn=d_attn)
    return pl.pallas_call(
        kern,
        grid=(t // tm,),
        in_specs=[
            pl.BlockSpec((tm, d), lambda i: (i, 0)),
            _const_spec(g1.shape),
            _const_spec(w_main.shape),
            _const_spec(wqT.shape),
            _const_spec(wvT.shape),
            _const_spec(vb.shape),
            _const_spec(bg.shape),
        ],
        out_specs=[
            pl.BlockSpec((tm, d_ssm), lambda i: (i, 0)),
            pl.BlockSpec((nblk, MOBA_BLOCK, d_attn), lambda i: (i, 0, 0)),
            pl.BlockSpec((nblk, 1, d_attn), lambda i: (i, 0, 0)),
            pl.BlockSpec((1, d_attn, tm), lambda i: (i // tiles_per_seq, 0, i % tiles_per_seq)),
            pl.BlockSpec((nblk, v_rows, MOBA_BLOCK), lambda i: (i, 0, 0)),
            pl.BlockSpec((tm, bg.shape[1]), lambda i: (i, 0)),
        ],
        out_shape=[
            jax.ShapeDtypeStruct((t, d_ssm), F32),
            jax.ShapeDtypeStruct((nb_total, MOBA_BLOCK, d_attn), BF16),
            jax.ShapeDtypeStruct((nb_total, 1, d_attn), F32),
            jax.ShapeDtypeStruct((batch, d_attn, seq), BF16),
            jax.ShapeDtypeStruct((nb_total, v_rows, MOBA_BLOCK), BF16),
            jax.ShapeDtypeStruct((t, bg.shape[1]), BF16),
        ],
        compiler_params=pltpu.CompilerParams(
            dimension_semantics=("arbitrary",), vmem_limit_bytes=VMEM_LIMIT),
    )(x2, g1, w_main, wqT, wvT, vb, bg)


def _s5_kernel(u_ref, bb_ref, cc_ref, ar_ref, ai_ref, d_ref, wglu_ref, bglu_ref,
               y_ref, utb_ref, sbuf_ref, state_ref, ytb_ref, *, batch, ts, n_state, chunk):
    d_ssm = u_ref.shape[2]
    n_slab = d_ssm // LANES

    @pl.when(pl.program_id(0) == 0)
    def _():
        state_ref[...] = jnp.zeros_like(state_ref)

    for b in range(batch):
        ub = u_ref[b]
        for j in range(n_slab):
            utb_ref[j, pl.ds(b, ts, stride=batch), :] = ub[:, j * LANES:(j + 1) * LANES]
    u_tb = jnp.concatenate([utb_ref[j] for j in range(n_slab)], axis=1)
    n_part, pc, pw = bb_ref.shape
    ps = pw // 2
    u_bf = u_tb.astype(BF16)
    for h in range(n_part):
        sbuf_ref[:, h * pw:(h + 1) * pw] = jnp.dot(u_bf[:, h * pc:(h + 1) * pc], bb_ref[h],
                                                   preferred_element_type=F32)

    for c in range(n_state // chunk):
        h, cc_ = divmod(c * chunk, ps)
        re = slice(h * pw + cc_, h * pw + cc_ + chunk)
        im = slice(h * pw + ps + cc_, h * pw + ps + cc_ + chunk)
        ar = jnp.broadcast_to(ar_ref[:, c * chunk:(c + 1) * chunk], (batch, chunk))
        ai = jnp.broadcast_to(ai_ref[:, c * chunk:(c + 1) * chunk], (batch, chunk))

        def step(t, carry, re=re, im=im, ar=ar, ai=ai):
            hr, hi = carry
            r0 = pl.multiple_of(t * batch, batch)
            nhr = ar * hr - ai * hi + sbuf_ref[pl.ds(r0, batch), re]
            nhi = ar * hi + ai * hr + sbuf_ref[pl.ds(r0, batch), im]
            sbuf_ref[pl.ds(r0, batch), re] = nhr
            sbuf_ref[pl.ds(r0, batch), im] = nhi
            return nhr, nhi

        hr, hi = lax.fori_loop(0, ts, step, (state_ref[:, re], state_ref[:, im]), unroll=8)
        state_ref[:, re] = hr
        state_ref[:, im] = hi

    y = jnp.concatenate(
        [jnp.dot(sbuf_ref[:, h * pw:(h + 1) * pw].astype(BF16), cc_ref[h], preferred_element_type=F32)
         for h in range(n_part)], axis=1)
    y = y + d_ref[...] * u_tb
    g = jax.nn.gelu(y)
    z = jnp.dot(g.astype(BF16), wglu_ref[...], preferred_element_type=F32) + bglu_ref[...]
    out = g * _sigmoid(z)
    for j in range(n_slab):
        ytb_ref[j] = out[:, j * LANES:(j + 1) * LANES]
    for b in range(batch):
        y_ref[b] = jnp.concatenate(
            [ytb_ref[j, pl.ds(b, ts, stride=batch), :] for j in range(n_slab)], axis=1).astype(BF16)


def _s5(u3, bb, cc, ar, ai, dvec, wglu, bglu, *, ts, chunk):
    batch, seq, d_ssm = u3.shape
    n_state = ar.shape[1]
    kern = functools.partial(_s5_kernel, batch=batch, ts=ts, n_state=n_state, chunk=chunk)
    return pl.pallas_call(
        kern,
        grid=(seq // ts,),
        in_specs=[
            pl.BlockSpec((batch, ts, d_ssm), lambda i: (0, i, 0)),
            _const_spec(bb.shape), _const_spec(cc.shape), _const_spec(ar.shape), _const_spec(ai.shape),
            _const_spec(dvec.shape), _const_spec(wglu.shape), _const_spec(bglu.shape),
        ],
        out_specs=pl.BlockSpec((batch, ts, d_ssm), lambda i: (0, i, 0)),
        out_shape=jax.ShapeDtypeStruct((batch, seq, d_ssm), BF16),
        scratch_shapes=[
            pltpu.VMEM((d_ssm // LANES, ts * batch, LANES), F32),
            pltpu.VMEM((ts * batch, 2 * n_state), F32),
            pltpu.VMEM((batch, 2 * n_state), F32),
            pltpu.VMEM((d_ssm // LANES, ts * batch, LANES), F32),
        ],
        compiler_params=pltpu.CompilerParams(
            dimension_semantics=("arbitrary",), vmem_limit_bytes=VMEM_LIMIT),
    )(u3, bb, cc, ar, ai, dvec, wglu, bglu)


def _select_blocks(gate, blk_f, n_pick):
    sel = jnp.zeros(gate.shape, jnp.bool_)
    for _ in range(n_pick):
        mx = jnp.max(gate, axis=0, keepdims=True)
        idx = jnp.min(jnp.where(gate == mx, blk_f, float(gate.shape[0])), axis=0, keepdims=True)
        hit = blk_f == idx
        sel = sel | hit
        gate = jnp.where(hit, -jnp.inf, gate)
    return sel


def _moba_kernel(farb_ref, qT_ref, k_ref, vT_ref, kmean_ref, bias_ref, o_ref,
                 mask_ref, s_ref, p0_ref, p1_ref, acc_ref, *, nb):
    hg = pl.program_id(1)
    i = pl.program_id(2)
    tq = qT_ref.shape[2]
    hw = qT_ref.shape[1]
    nh = hw // HEAD_DIM
    qT = qT_ref[0]
    row = lax.broadcasted_iota(jnp.int32, qT.shape, 0)
    km = kmean_ref[0]
    km_hi = km.astype(BF16)
    km_lo = (km - km_hi.astype(F32)).astype(BF16)
    blk = lax.broadcasted_iota(jnp.int32, (nb, tq), 0)
    blk_f = blk.astype(F32)
    prev = jnp.maximum(i - 1, 0)
    n_far = jnp.maximum(i - 1, 0)
    n_steps = (n_far + 1) // 2
    n_iter = n_steps // 2

    gw = min(hw, MXU_DEPTH)
    groups = [slice(j * HEAD_DIM // gw * gw, j * HEAD_DIM // gw * gw + gw) for j in range(nh)]

    def qk(j, keys):
        return jnp.dot(keys[:, groups[j]], qTs[j], preferred_element_type=F32)

    qTs, near_masks = [], []
    for j in range(nh):
        qTh = jnp.where((row >= j * HEAD_DIM) & (row < (j + 1) * HEAD_DIM), qT, jnp.zeros_like(qT))
        qTs.append(qTh[groups[j]])
        gate = qk(j, km_hi) + qk(j, km_lo)
        gate = jnp.where(blk < i, gate, NEG_INF)
        sel = _select_blocks(gate, blk_f, MOBA_TOP_K) & (blk < i)
        mask_ref[j] = jnp.where(sel & (blk < i - 1), farb_ref[nh * hg + j], NEG_INF)
        mprev = jnp.max(jnp.where(sel & (blk == i - 1), 0.0, NEG_INF), axis=0, keepdims=True)
        near_masks.append((mprev, jnp.zeros_like(mprev)))

    def far_keys(n0):
        n0 = pl.multiple_of(n0, 2)
        return k_ref[pl.ds(n0, 2)].reshape(2 * MOBA_BLOCK, hw)

    def far_masks(j, n0):
        return mask_ref[j, pl.ds(n0, 1), :], mask_ref[j, pl.ds(n0 + 1, 1), :]

    def block_max(s, masks):
        return jnp.maximum(jnp.max(s[:MOBA_BLOCK], axis=0, keepdims=True) + masks[0],
                           jnp.max(s[MOBA_BLOCK:], axis=0, keepdims=True) + masks[1])

    def probs(j, p_ref, s, shift, masks):
        p_ref[j, :MOBA_BLOCK] = jnp.exp2((s[:MOBA_BLOCK] - (shift - masks[0])).astype(BF16))
        p_ref[j, MOBA_BLOCK:] = jnp.exp2((s[MOBA_BLOCK:] - (shift - masks[1])).astype(BF16))

    def pv(j, p_ref, va, vb):
        vrows = slice(j * V_ROWS, (j + 1) * V_ROWS)
        vv = jnp.concatenate([va[vrows, :], vb[vrows, :]], axis=1)
        acc_ref[j] += jnp.dot(vv, p_ref[j], preferred_element_type=F32)

    near_keys = jnp.concatenate([k_ref[prev], k_ref[i]], axis=0)
    near_max = []
    for j in range(nh):
        s = qk(j, near_keys) + bias_ref[j]
        s_ref[j] = s
        near_max.append(block_max(s, near_masks[j]))

    def attend(shift):
        acc_ref[...] = jnp.zeros_like(acc_ref)
        first = far_keys(0)
        for j in range(nh):
            probs(j, p0_ref, s_ref[j], shift[j], near_masks[j])
            probs(j, p1_ref, qk(j, first), shift[j], far_masks(j, 0))

        def trip(g, carry):
            f0 = 4 * g
            n2 = f0 + 2
            n3 = jnp.minimum(f0 + 4, nb - 2)
            va0 = vT_ref[jnp.where(g == 0, prev, f0 - 2)]
            vb0 = vT_ref[jnp.where(g == 0, i, f0 - 1)]
            va1 = vT_ref[f0]
            vb1 = vT_ref[f0 + 1]
            k2 = far_keys(n2)
            k3_ = far_keys(n3)
            for j in range(nh):
                pv(j, p0_ref, va0, vb0)
                probs(j, p0_ref, qk(j, k2), shift[j], far_masks(j, n2))
                pv(j, p1_ref, va1, vb1)
                probs(j, p1_ref, qk(j, k3_), shift[j], far_masks(j, n3))
            return carry

        lax.fori_loop(0, n_iter, trip, 0)
        last = 4 * n_iter
        va = vT_ref[jnp.where(n_iter == 0, prev, last - 2)]
        vb = vT_ref[jnp.where(n_iter == 0, i, last - 1)]
        for j in range(nh):
            pv(j, p0_ref, va, vb)

        @pl.when(n_steps % 2 == 1)
        def _():
            va1 = vT_ref[last]
            vb1 = vT_ref[last + 1]
            for j in range(nh):
                pv(j, p1_ref, va1, vb1)

    attend(near_max)
    acc = acc_ref[...]
    overflow = jnp.max(jnp.where(jnp.isfinite(acc), 0.0, 1.0)) > 0.0

    @pl.when(overflow)
    def _():
        def far_max(n, m_run):
            kp = k_ref[n]
            return tuple(jnp.maximum(m_run[j], jnp.max(qk(j, kp), axis=0, keepdims=True) + mask_ref[j, pl.ds(n, 1), :])
                         for j in range(nh))

        attend(lax.fori_loop(0, n_far, far_max, tuple(near_max)))

    outs = [acc_ref[j][:HEAD_DIM] / acc_ref[j][HEAD_DIM:HEAD_DIM + 1] for j in range(nh)]
    o_ref[0] = jnp.concatenate(outs, axis=0).T.astype(BF16)


def _moba(farb, qT, k3, vT3, kmean, bias, *, batch, seq):
    nb = seq // MOBA_BLOCK
    assert nb % 4 == 0, "far blocks are consumed four per loop trip"
    d_attn = qT.shape[1]
    nh = MOBA_HEADS_PER_STEP
    hw = nh * HEAD_DIM
    kern = functools.partial(_moba_kernel, nb=nb)
    return pl.pallas_call(
        kern,
        grid=(batch, d_attn // hw, nb),
        in_specs=[
            pl.BlockSpec(memory_space=pltpu.SMEM),
            pl.BlockSpec((1, hw, MOBA_BLOCK), lambda b, h, i: (b, h, i)),
            pl.BlockSpec((nb, MOBA_BLOCK, hw), lambda b, h, i: (b, 0, h)),
            pl.BlockSpec((nb, nh * V_ROWS, MOBA_BLOCK), lambda b, h, i: (b, h, 0)),
            pl.BlockSpec((1, nb, hw), lambda b, h, i: (b, 0, h)),
            pl.BlockSpec((nh, 2 * MOBA_BLOCK, MOBA_BLOCK), lambda b, h, i: (h, 0, 0), pipeline_mode=pl.Buffered(1)),
        ],
        out_specs=pl.BlockSpec((1, MOBA_BLOCK, hw), lambda b, h, i: (b, i, h)),
        out_shape=jax.ShapeDtypeStruct((batch, seq, d_attn), BF16),
        scratch_shapes=[
            pltpu.VMEM((nh, nb, MOBA_BLOCK), F32),
            pltpu.VMEM((nh, 2 * MOBA_BLOCK, MOBA_BLOCK), F32),
            pltpu.VMEM((nh, 2 * MOBA_BLOCK, MOBA_BLOCK), BF16),
            pltpu.VMEM((nh, 2 * MOBA_BLOCK, MOBA_BLOCK), BF16),
            pltpu.VMEM((nh, V_ROWS, MOBA_BLOCK), F32),
        ],
        compiler_params=pltpu.CompilerParams(
            dimension_semantics=("arbitrary", "arbitrary", "arbitrary"),
            vmem_limit_bytes=VMEM_LIMIT),
    )(farb, qT, k3, vT3, kmean, bias)


def _route(logits):
    lane = lax.broadcasted_iota(jnp.int32, logits.shape, 1)
    lane_f = lane.astype(F32)
    big = float(ROUTER_LANES)
    ninf = -jnp.inf
    gmask = lane < N_GROUPS
    gmax = jnp.max(jnp.where(gmask, logits, ninf), axis=1, keepdims=True)
    gsum = jnp.sum(jnp.where(gmask, jnp.exp(logits - gmax), 0.0), axis=1, keepdims=True)
    g_val = 1.0 / gsum
    g_idx = jnp.min(jnp.where(gmask & (logits == gmax), lane_f, big), axis=1, keepdims=True)
    e_grp = ((lane - EXPERT_LANE0) >> 3).astype(F32)
    emask = (lane >= EXPERT_LANE0) & (lane < EXPERT_LANE0 + N_EXPERTS) & (e_grp == g_idx)
    el = jnp.where(emask, logits, ninf)
    e1 = jnp.max(el, axis=1, keepdims=True)
    i1 = jnp.min(jnp.where(emask & (el == e1), lane_f, big), axis=1, keepdims=True)
    el2 = jnp.where(lane_f == i1, ninf, el)
    e2 = jnp.max(el2, axis=1, keepdims=True)
    i2 = jnp.min(jnp.where(emask & (lane_f != i1) & (el2 == e2), lane_f, big), axis=1, keepdims=True)
    tt = jnp.exp(e2 - e1)
    w1 = g_val / (1.0 + tt)
    w2 = g_val * tt / (1.0 + tt)
    first_low = i1 < i2
    lo = jnp.minimum(i1, i2) - (EXPERT_LANE0 + EXPERTS_PER_GROUP * g_idx)
    hi = jnp.maximum(i1, i2) - (EXPERT_LANE0 + EXPERTS_PER_GROUP * g_idx)
    pair = lo * (2 * EXPERTS_PER_GROUP - 1 - lo) * 0.5 + (hi - lo - 1.0)
    cls = g_idx * PAIRS_PER_GROUP + pair
    return cls, jnp.where(first_low, w1, w2), jnp.where(first_low, w2, w1)


def _col_to_row(col):
    n = col.shape[0]
    r = lax.broadcasted_iota(jnp.int32, (n, n), 0)
    c = lax.broadcasted_iota(jnp.int32, (n, n), 1)
    return jnp.sum(jnp.where(r == c, col, 0.0), axis=0, keepdims=True)


def _merge_kernel(x_ref, ys_ref, ya_ref, gates_ref, wus_ref, wua_ref, wout_ref, g2_ref, wr_ref, br_ref,
                  tri_ref, x1e_ref, cls_ref, rank_ref, cnt_ref):
    d = x_ref.shape[1]
    a = jnp.dot(ys_ref[...], wus_ref[...], preferred_element_type=F32)
    b = jnp.dot(ya_ref[...], wua_ref[...], preferred_element_type=F32)
    gts = gates_ref[...].astype(F32)
    merged = _sigmoid(gts[:, :d]) * a + _sigmoid(gts[:, d:]) * b
    x1 = x_ref[...] + jnp.dot(merged.astype(BF16), wout_ref[...], preferred_element_type=F32)
    xn = _rms(x1, g2_ref[...]).astype(BF16)
    logits = jnp.dot(xn, wr_ref[...], preferred_element_type=F32) + br_ref[...]
    cls, wa, wb = _route(logits)
    lane = lax.broadcasted_iota(jnp.int32, logits.shape, 1)
    x1e_ref[:, :d] = x1
    x1e_ref[:, d:] = jnp.where(lane == 0, wa, jnp.where(lane == 1, wb, 0.0))
    onehot = lane.astype(F32) == cls
    before = jnp.dot(tri_ref[...], onehot.astype(BF16), preferred_element_type=F32)
    rank = jnp.sum(jnp.where(onehot, before, 0.0), axis=1, keepdims=True)
    cls_ref[0] = _col_to_row(cls).astype(jnp.int32)
    rank_ref[0] = _col_to_row(rank).astype(jnp.int32)
    cnt_ref[0] = jnp.sum(onehot.astype(F32), axis=0, keepdims=True).astype(jnp.int32)


def _merge(x2, ys, ya, gates, wus, wua, wout, g2, wr, br, *, tm):
    t, d = x2.shape
    nt = t // tm
    tri = jnp.tril(jnp.ones((tm, tm), F32), -1).astype(BF16)
    return pl.pallas_call(
        _merge_kernel,
        grid=(nt,),
        in_specs=[
            pl.BlockSpec((tm, d), lambda i: (i, 0)),
            pl.BlockSpec((tm, ys.shape[1]), lambda i: (i, 0)),
            pl.BlockSpec((tm, ya.shape[1]), lambda i: (i, 0)),
            pl.BlockSpec((tm, gates.shape[1]), lambda i: (i, 0)),
            _const_spec(wus.shape), _const_spec(wua.shape), _const_spec(wout.shape),
            _const_spec(g2.shape), _const_spec(wr.shape), _const_spec(br.shape), _const_spec(tri.shape),
        ],
        out_specs=[
            pl.BlockSpec((tm, d + ROUTER_LANES), lambda i: (i, 0)),
            pl.BlockSpec((1, 1, tm), lambda i: (i, 0, 0)),
            pl.BlockSpec((1, 1, tm), lambda i: (i, 0, 0)),
            pl.BlockSpec((1, 1, ROUTER_LANES), lambda i: (i, 0, 0)),
        ],
        out_shape=[
            jax.ShapeDtypeStruct((t, d + ROUTER_LANES), F32),
            jax.ShapeDtypeStruct((nt, 1, tm), jnp.int32),
            jax.ShapeDtypeStruct((nt, 1, tm), jnp.int32),
            jax.ShapeDtypeStruct((nt, 1, ROUTER_LANES), jnp.int32),
        ],
        compiler_params=pltpu.CompilerParams(
            dimension_semantics=("arbitrary",), vmem_limit_bytes=VMEM_LIMIT),
    )(x2, ys, ya, gates, wus, wua, wout, g2, wr, br, tri)


def _permute_rows(pos, src, *, scatter):
    n, width = src.shape
    w = SC_WINDOW
    idx = jnp.pad(pos.reshape(n // w, w), ((0, 0), (0, LANES - w)))
    mesh = plsc.VectorSubcoreMesh(core_axis_name="core", subcore_axis_name="subcore")
    idx_spec = pl.BlockSpec((1, LANES), index_map=lambda i: (i, 0))
    row_spec = pl.BlockSpec((w, width), index_map=lambda i: (i, 0))

    @pl.kernel(out_type=jax.ShapeDtypeStruct(src.shape, src.dtype), mesh=mesh, scratch_types=[])
    def permute(src_hbm, idx_hbm, dst_hbm):
        if scatter:
            def body(rows_vmem, idx_vmem):
                pltpu.sync_copy(rows_vmem, dst_hbm.at[idx_vmem.at[0, pl.ds(0, w)]])
            in_specs, out_specs, args = [row_spec, idx_spec], [], (src_hbm, idx_hbm)
        else:
            def body(idx_vmem, rows_vmem):
                pltpu.sync_copy(src_hbm.at[idx_vmem.at[0, pl.ds(0, w)]], rows_vmem)
            in_specs, out_specs, args = [idx_spec], [row_spec], (idx_hbm, dst_hbm)
        pltpu.emit_pipeline(body, grid=(n // w,), in_specs=in_specs, out_specs=out_specs,
                            core_axis_name=("core", "subcore"),
                            dimension_semantics=(pltpu.PARALLEL,))(*args)

    return permute(src, idx)


def _expert(xn, w1_ref, w3_ref, w2_ref):
    h1 = jnp.dot(xn, w1_ref[0], preferred_element_type=F32)
    h3 = jnp.dot(xn, w3_ref[0], preferred_element_type=F32)
    hid = (h1 * _sigmoid(h1) * h3).astype(BF16)
    return jnp.dot(hid, w2_ref[0], preferred_element_type=F32)


def _moe_kernel(tile_ref, ea_ref, eb_ref, lo_ref, hi_ref, first_ref, last_ref,
                x_ref, w1a_ref, w3a_ref, w2a_ref, w1b_ref, w3b_ref, w2b_ref, g2_ref, gf_ref,
                y_ref, acc_ref):
    w = pl.program_id(0)
    d = y_ref.shape[1]
    lo = lo_ref[w]
    hi = hi_ref[w]

    @pl.when(hi > lo)
    def _():
        x1 = x_ref[:, :d]
        row = lax.broadcasted_iota(jnp.int32, (x1.shape[0], 1), 0)
        inseg = (row >= lo) & (row < hi)
        ca = jnp.where(inseg, x_ref[:, d:d + 1], 0.0)
        cb = jnp.where(inseg, x_ref[:, d + 1:d + 2], 0.0)
        xn = _rms(x1, g2_ref[...]).astype(BF16)
        contrib = (ca * _expert(xn, w1a_ref, w3a_ref, w2a_ref)
                   + cb * _expert(xn, w1b_ref, w3b_ref, w2b_ref))

        @pl.when(first_ref[w] == 1)
        def _():
            acc_ref[...] = contrib

        @pl.when(first_ref[w] == 0)
        def _():
            acc_ref[...] += contrib

        @pl.when(last_ref[w] == 1)
        def _():
            y_ref[...] = _rms(x1 + acc_ref[...], gf_ref[...])


def _moe(items, x1s, w1, w3, w2, g2, gf, *, rows):
    t, de_ = x1s.shape
    d = g2.shape[1]
    ne, _, de = w1.shape
    n_items = items[0].shape[0]

    def wa_map(w, tile, ea, eb, lo, hi, first, last):
        return (ea[w], 0, 0)

    def wb_map(w, tile, ea, eb, lo, hi, first, last):
        return (eb[w], 0, 0)

    def row_map(w, tile, ea, eb, lo, hi, first, last):
        return (tile[w], 0)

    def const_map(w, *_):
        return (0, 0)

    grid_spec = pltpu.PrefetchScalarGridSpec(
        num_scalar_prefetch=len(items),
        grid=(n_items,),
        in_specs=[
            pl.BlockSpec((rows, de_), row_map),
            pl.BlockSpec((1, d, de), wa_map), pl.BlockSpec((1, d, de), wa_map), pl.BlockSpec((1, de, d), wa_map),
            pl.BlockSpec((1, d, de), wb_map), pl.BlockSpec((1, d, de), wb_map), pl.BlockSpec((1, de, d), wb_map),
            pl.BlockSpec(g2.shape, const_map), pl.BlockSpec(gf.shape, const_map),
        ],
        out_specs=pl.BlockSpec((rows, d), row_map),
        scratch_shapes=[pltpu.VMEM((rows, d), F32)],
    )
    return pl.pallas_call(
        _moe_kernel,
        grid_spec=grid_spec,
        out_shape=jax.ShapeDtypeStruct((t, d), F32),
        compiler_params=pltpu.CompilerParams(
            dimension_semantics=("arbitrary",), vmem_limit_bytes=VMEM_LIMIT),
    )(*items, x1s, w1, w3, w2, w1, w3, w2, g2, gf)


def _pair_table():
    lo, hi = [], []
    for a in range(EXPERTS_PER_GROUP):
        for b in range(a + 1, EXPERTS_PER_GROUP):
            lo.append(a)
            hi.append(b)
    return jnp.asarray(lo, jnp.int32), jnp.asarray(hi, jnp.int32)


def _before_sum(v):
    ix = jnp.arange(v.shape[0])
    return jnp.sum(jnp.where(ix[None, :] < ix[:, None], v[None, :], 0), axis=1)


def _place(dest, vals, n):
    return jnp.sum(jnp.where(dest[None, :] == jnp.arange(n)[:, None], vals[None, :], 0), axis=1)


def _routing_tables(cls, rank, cnt, *, rows):
    nt, _, tm = cls.shape
    t = nt * tm
    lanes = cnt.shape[2]
    c = cnt[:, 0, :]
    tot = jnp.sum(c, axis=0)
    class_off = _before_sum(tot)
    tile_ix = jnp.arange(nt)
    tiles_before = jnp.sum(jnp.where((tile_ix[None, :] < tile_ix[:, None])[:, :, None], c[None], 0), axis=1)
    base = class_off[None, :] + tiles_before
    onehot = cls[:, 0, :, None] == jnp.arange(lanes)[None, None, :]
    pos = jnp.sum(jnp.where(onehot, base[:, None, :], 0), axis=2) + rank[:, 0, :]

    n_tiles = t // rows
    ta = jnp.arange(n_tiles, dtype=jnp.int32) * rows
    cb = class_off[1:N_CLASSES]
    n_items = n_tiles + N_CLASSES - 1
    idx = jnp.arange(n_items)
    at_a = jnp.arange(n_tiles) + jnp.sum(cb[None, :] <= ta[:, None], axis=1)
    at_b = jnp.arange(N_CLASSES - 1) + jnp.sum(ta[None, :] < cb[:, None], axis=1)
    starts = _place(at_a, ta, n_items) + _place(at_b, cb, n_items)
    ends = jnp.concatenate([starts[1:], jnp.full((1,), t, jnp.int32)])
    real = ends > starts
    n_real = jnp.sum(real)
    dest = jnp.where(real, _before_sum(real.astype(jnp.int32)), n_real + _before_sum(1 - real.astype(jnp.int32)))
    starts, ends = _place(dest, starts, n_items), _place(dest, ends, n_items)
    valid = idx < n_real
    last_real = jnp.maximum(n_real - 1, 0)
    starts = jnp.where(valid, starts, jnp.sum(jnp.where(idx == last_real, starts, 0)))
    tile = jnp.minimum(starts // rows, n_tiles - 1)
    klass = jnp.sum(class_off[None, :N_CLASSES] <= starts[:, None], axis=1) - 1
    pair_lo, pair_hi = _pair_table()
    grp = klass // PAIRS_PER_GROUP
    pair_hot = (klass % PAIRS_PER_GROUP)[:, None] == jnp.arange(PAIRS_PER_GROUP)[None, :]
    ea = grp * EXPERTS_PER_GROUP + jnp.sum(jnp.where(pair_hot, pair_lo[None, :], 0), axis=1)
    eb = grp * EXPERTS_PER_GROUP + jnp.sum(jnp.where(pair_hot, pair_hi[None, :], 0), axis=1)
    lo = jnp.where(valid, starts - tile * rows, 0)
    hi = jnp.where(valid, ends - tile * rows, 0)
    tile_prev = jnp.concatenate([tile[:1], tile[:-1]])
    tile_next = jnp.concatenate([tile[1:], tile[-1:]])
    first = valid & ((idx == 0) | (tile != tile_prev))
    last = valid & ((idx == last_real) | (tile != tile_next))
    items = tuple(a.astype(jnp.int32) for a in (tile, ea, eb, lo, hi, first, last))
    return pos.reshape(t).astype(jnp.int32), items


def _t5_bucket(dist):
    n = jnp.maximum(dist, 0)
    max_exact = NUM_BUCKETS // 2
    large = max_exact + (jnp.log(jnp.maximum(n, max_exact).astype(F32) / max_exact)
                         / math.log(MAX_DISTANCE / max_exact)
                         * (NUM_BUCKETS - max_exact)).astype(jnp.int32)
    return jnp.where(n < max_exact, n, jnp.minimum(large, NUM_BUCKETS - 1))


def _bias_tables(rel_bias):
    kk = jnp.arange(MOBA_BLOCK)[:, None]
    qq = jnp.arange(MOBA_BLOCK)[None, :]
    rb = rel_bias.astype(F32) * LOG2E
    d_own = qq - kk

    def lookup(dist):
        oh = jax.nn.one_hot(_t5_bucket(dist), NUM_BUCKETS, dtype=F32)
        return jnp.einsum('hn,kqn->hkq', rb, oh, precision=lax.Precision.HIGHEST)

    own = jnp.where((d_own >= 0)[None], lookup(d_own), NEG_INF)
    prv = lookup(d_own + MOBA_BLOCK)
    tab = jnp.concatenate([prv, own], axis=1)
    far = rb[:, NUM_BUCKETS - 1]
    return tab, far


def _ssm_params(lam_re, lam_im, log_step, b_re, b_im, c_re, c_im):
    g, p = lam_re.shape
    step = jnp.exp(log_step)[:, None]
    decay = jnp.exp(lam_re * step)
    a_re = decay * jnp.cos(lam_im * step)
    a_im = decay * jnp.sin(lam_im * step)
    denom = lam_re * lam_re + lam_im * lam_im
    nr, ni = a_re - 1.0, a_im
    coef_re = (nr * lam_re + ni * lam_im) / denom
    coef_im = (ni * lam_re - nr * lam_im) / denom
    bb_re = coef_re[..., None] * b_re - coef_im[..., None] * b_im
    bb_im = coef_re[..., None] * b_im + coef_im[..., None] * b_re
    eye = jnp.eye(g, dtype=F32)
    hc = b_re.shape[2]

    def in_mat(m):
        return (eye[:, None, :, None] * m.transpose(0, 2, 1)[:, :, None, :]).reshape(g * hc, g * p)

    def out_mat(m):
        return (eye[:, None, :, None] * m.transpose(0, 2, 1)[:, :, None, :]).reshape(g * p, g * hc)

    pc, ps = g * hc // SSM_PARTS, g * p // SSM_PARTS
    in_re, in_im, out_re, out_im = in_mat(bb_re), in_mat(bb_im), out_mat(c_re), out_mat(c_im)
    bb = jnp.stack([jnp.concatenate([m[h * pc:(h + 1) * pc, h * ps:(h + 1) * ps] for m in (in_re, in_im)], axis=1)
                    for h in range(SSM_PARTS)]).astype(BF16)
    cc = jnp.stack([jnp.concatenate([m[h * ps:(h + 1) * ps, h * pc:(h + 1) * pc] for m in (out_re, -out_im)], axis=0)
                    for h in range(SSM_PARTS)]).astype(BF16)
    return bb, cc, a_re.reshape(1, g * p), a_im.reshape(1, g * p)


def kernel(x, ln1_g, w_in, b_gate, ssm_lambda_re, ssm_lambda_im, ssm_log_step, ssm_b_re, ssm_b_im,
           ssm_c_re, ssm_c_im, ssm_d, w_glu, b_glu, w_up_ssm, w_up_attn, rel_bias, w_out, ln2_g,
           w_router_group, b_router_group, w_router_expert, b_router_expert, w1, w3, w2, ln_f_g):
    assert w_in.shape[0] == 1, "single-layer block"
    batch, seq, d = x.shape
    t = batch * seq
    d_ssm = w_glu.shape[1]
    d_attn = w_up_attn.shape[1]
    n_heads = d_attn // HEAD_DIM
    o1, o2, o3, o4 = d_ssm, d_ssm + d_attn, d_ssm + 2 * d_attn, d_ssm + 3 * d_attn
    x2 = x.reshape(t, d)

    wl = w_in[0]
    w_main = jnp.concatenate([wl[:, :o1], wl[:, o2:o3], wl[:, o4:]], axis=1).astype(BF16)
    wqT = (wl[:, o1:o2] * (HEAD_DIM ** -0.5 * LOG2E)).T.astype(BF16)
    wv = wl[:, o3:o4].T.reshape(n_heads, HEAD_DIM, d)
    wvT = jnp.concatenate([wv, jnp.zeros((n_heads, BF16_ROWS, d), F32)], axis=1)
    wvT = wvT.reshape(n_heads * V_ROWS, d).astype(BF16)
    vb = jnp.concatenate([jnp.zeros((n_heads, HEAD_DIM, 1), F32), jnp.ones((n_heads, BF16_ROWS, 1), F32)],
                         axis=1).reshape(n_heads * V_ROWS, 1)
    u, k3, kmean, qT, vT3, gates = _inproj(
        x2, ln1_g[0][None], w_main, wqT, wvT, vb, b_gate[0][None], batch=batch, seq=seq, tm=512)

    bb, cc, ar, ai = _ssm_params(ssm_lambda_re[0], ssm_lambda_im[0], ssm_log_step[0],
                                 ssm_b_re[0], ssm_b_im[0], ssm_c_re[0], ssm_c_im[0])
    y_ssm = _s5(u.reshape(batch, seq, d_ssm), bb, cc, ar, ai, ssm_d[0].reshape(1, d_ssm),
                w_glu[0].astype(BF16), b_glu[0][None], ts=128, chunk=512)

    bias, far = _bias_tables(rel_bias)
    y_attn = _moba(far, qT, k3, vT3, kmean.reshape(batch, seq // MOBA_BLOCK, d_attn), bias,
                   batch=batch, seq=seq)

    wr = jnp.zeros((d, ROUTER_LANES), F32)
    wr = wr.at[:, :N_GROUPS].set(w_router_group[0])
    wr = wr.at[:, EXPERT_LANE0:EXPERT_LANE0 + N_EXPERTS].set(w_router_expert[0]).astype(BF16)
    br = jnp.zeros((1, ROUTER_LANES), F32)
    br = br.at[0, :N_GROUPS].set(b_router_group[0])
    br = br.at[0, EXPERT_LANE0:EXPERT_LANE0 + N_EXPERTS].set(b_router_expert[0])
    x1e, cls, rank, cnt = _merge(
        x2, y_ssm.reshape(t, d_ssm), y_attn.reshape(t, d_attn), gates,
        w_up_ssm[0].astype(BF16), w_up_attn[0].astype(BF16), w_out[0].astype(BF16),
        ln2_g[0][None], wr, br, tm=512)

    pos, items = _routing_tables(cls, rank, cnt, rows=MOE_ROWS)
    x1s = _permute_rows(pos, x1e, scatter=True)
    ys = _moe(items, x1s, w1[0].astype(BF16), w3[0].astype(BF16), w2[0].astype(BF16),
              ln2_g[0][None], ln_f_g[None], rows=MOE_ROWS)
    y = _permute_rows(pos, ys, scatter=False)
    return y.reshape(batch, seq, d)
```

```python
import functools
import math

import jax
import jax.numpy as jnp
from jax import lax
from jax.experimental import pallas as pl
from jax.experimental.pallas import tpu as pltpu
from jax.experimental.pallas import tpu_sc as plsc

F32 = jnp.float32
BF16 = jnp.bfloat16

SSM_GROUP = 16
SSM_STATE = 64
HEAD_DIM = 64
MOBA_BLOCK = 256
MOBA_TOP_K = 3
NUM_BUCKETS = 32
MAX_DISTANCE = 128
N_GROUPS = 4
EXPERTS_PER_GROUP = 8
N_EXPERTS = N_GROUPS * EXPERTS_PER_GROUP
PAIRS_PER_GROUP = EXPERTS_PER_GROUP * (EXPERTS_PER_GROUP - 1) // 2
N_CLASSES = N_GROUPS * PAIRS_PER_GROUP
MOE_ROWS = 256
SC_WINDOW = 32
SSM_PARTS = 2
MOBA_HEADS_PER_STEP = 8
RMS_EPS = 1e-6
NEG_INF = -1e30
LOG2E = 1.4426950408889634

LANES = 128
MXU_DEPTH = 256
BF16_ROWS = 16
V_ROWS = HEAD_DIM + BF16_ROWS
ROUTER_LANES = LANES
EXPERT_LANE0 = N_GROUPS
VMEM_LIMIT = 56 * 1024 * 1024


def _sigmoid(x):
    return 1.0 / (1.0 + jnp.exp(-x))


def _rms(x, g):
    ms = jnp.mean(x * x, axis=-1, keepdims=True)
    return x * lax.rsqrt(ms + RMS_EPS) * g


def _const_spec(shape):
    n = len(shape)
    return pl.BlockSpec(shape, lambda *_: (0,) * n)


def _inproj_kernel(x_ref, g_ref, w_ref, wqT_ref, wvT_ref, vb_ref, bg_ref,
                   u_ref, k_ref, kmean_ref, qT_ref, vT_ref, gates_ref, *, nblk, d_ssm, d_attn):
    hn = _rms(x_ref[...], g_ref[...]).astype(BF16)
    acc = jnp.dot(hn, w_ref[...], preferred_element_type=F32)
    u_ref[...] = acc[:, :d_ssm]
    kf = acc[:, d_ssm:d_ssm + d_attn].reshape(nblk, MOBA_BLOCK, d_attn)
    k_ref[...] = kf.astype(BF16)
    kmean_ref[...] = jnp.mean(kf, axis=1, keepdims=True)
    gates_ref[...] = (acc[:, d_ssm + d_attn:] + bg_ref[...]).astype(BF16)
    nt = (((1,), (1,)), ((), ()))
    qT = lax.dot_general(wqT_ref[...], hn, nt, preferred_element_type=F32)
    qT_ref[0] = qT.astype(BF16)
    vT = lax.dot_general(wvT_ref[...], hn, nt, preferred_element_type=F32) + vb_ref[...]
    for j in range(nblk):
        vT_ref[j] = vT[:, j * MOBA_BLOCK:(j + 1) * MOBA_BLOCK].astype(BF16)


def _inproj(x2, g1, w_main, wqT, wvT, vb, bg, *, batch, seq, tm):
    t, d = x2.shape
    d_attn = wqT.shape[0]
    v_rows = wvT.shape[0]
    d_ssm = w_main.shape[1] - d_attn - bg.shape[1]
    nblk = tm // MOBA_BLOCK
    tiles_per_seq = seq // tm
    nb_total = t // MOBA_BLOCK
    kern = functools.partial(_inproj_kernel, nblk=nblk, d_ssm=d_ssm, d_attn=d_attn)
    return pl.pallas_call(
        kern,
        grid=(t // tm,),
        in_specs=[
            pl.BlockSpec((tm, d), lambda i: (i, 0)),
            _const_spec(g1.shape),
            _const_spec(w_main.shape),
            _const_spec(wqT.shape),
            _const_spec(wvT.shape),
            _const_spec(vb.shape),
            _const_spec(bg.shape),
        ],
        out_specs=[
            pl.BlockSpec((tm, d_ssm), lambda i: (i, 0)),
            pl.BlockSpec((nblk, MOBA_BLOCK, d_attn), lambda i: (i, 0, 0)),
            pl.BlockSpec((nblk, 1, d_attn), lambda i: (i, 0, 0)),
            pl.BlockSpec((1, d_attn, tm), lambda i: (i // tiles_per_seq, 0, i % tiles_per_seq)),
            pl.BlockSpec((nblk, v_rows, MOBA_BLOCK), lambda i: (i, 0, 0)),
            pl.BlockSpec((tm, bg.shape[1]), lambda i: (i, 0)),
        ],
        out_shape=[
            jax.ShapeDtypeStruct((t, d_ssm), F32),
            jax.ShapeDtypeStruct((nb_total, MOBA_BLOCK, d_attn), BF16),
            jax.ShapeDtypeStruct((nb_total, 1, d_attn), F32),
            jax.ShapeDtypeStruct((batch, d_attn, seq), BF16),
            jax.ShapeDtypeStruct((nb_total, v_rows, MOBA_BLOCK), BF16),
            jax.ShapeDtypeStruct((t, bg.shape[1]), BF16),
        ],
        compiler_params=pltpu.CompilerParams(
            dimension_semantics=("arbitrary",), vmem_limit_bytes=VMEM_LIMIT),
    )(x2, g1, w_main, wqT, wvT, vb, bg)


def _s5_kernel(u_ref, bb_ref, cc_ref, ar_ref, ai_ref, d_ref, wglu_ref, bglu_ref,
               y_ref, utb_ref, sbuf_ref, state_ref, ytb_ref, *, batch, ts, n_state, chunk):
    d_ssm = u_ref.shape[2]
    n_slab = d_ssm // LANES

    @pl.when(pl.program_id(0) == 0)
    def _():
        state_ref[...] = jnp.zeros_like(state_ref)

    for b in range(batch):
        ub = u_ref[b]
        for j in range(n_slab):
            utb_ref[j, pl.ds(b, ts, stride=batch), :] = ub[:, j * LANES:(j + 1) * LANES]
    u_tb = jnp.concatenate([utb_ref[j] for j in range(n_slab)], axis=1)
    n_part, pc, pw = bb_ref.shape
    ps = pw // 2
    u_bf = u_tb.astype(BF16)
    for h in range(n_part):
        sbuf_ref[:, h * pw:(h + 1) * pw] = jnp.dot(u_bf[:, h * pc:(h + 1) * pc], bb_ref[h],
                                                   preferred_element_type=F32)

    for c in range(n_state // chunk):
        h, cc_ = divmod(c * chunk, ps)
        re = slice(h * pw + cc_, h * pw + cc_ + chunk)
        im = slice(h * pw + ps + cc_, h * pw + ps + cc_ + chunk)
        ar = jnp.broadcast_to(ar_ref[:, c * chunk:(c + 1) * chunk], (batch, chunk))
        ai = jnp.broadcast_to(ai_ref[:, c * chunk:(c + 1) * chunk], (batch, chunk))

        def step(t, carry, re=re, im=im, ar=ar, ai=ai):
            hr, hi = carry
            r0 = pl.multiple_of(t * batch, batch)
            nhr = ar * hr - ai * hi + sbuf_ref[pl.ds(r0, batch), re]
            nhi = ar * hi + ai * hr + sbuf_ref[pl.ds(r0, batch), im]
            sbuf_ref[pl.ds(r0, batch), re] = nhr
            sbuf_ref[pl.ds(r0, batch), im] = nhi
            return nhr, nhi

        hr, hi = lax.fori_loop(0, ts, step, (state_ref[:, re], state_ref[:, im]), unroll=8)
        state_ref[:, re] = hr
        state_ref[:, im] = hi

    y = jnp.concatenate(
        [jnp.dot(sbuf_ref[:, h * pw:(h + 1) * pw].astype(BF16), cc_ref[h], preferred_element_type=F32)
         for h in range(n_part)], axis=1)
    y = y + d_ref[...] * u_tb
    g = jax.nn.gelu(y)
    z = jnp.dot(g.astype(BF16), wglu_ref[...], preferred_element_type=F32) + bglu_ref[...]
    out = g * _sigmoid(z)
    for j in range(n_slab):
        ytb_ref[j] = out[:, j * LANES:(j + 1) * LANES]
    for b in range(batch):
        y_ref[b] = jnp.concatenate(
            [ytb_ref[j, pl.ds(b, ts, stride=batch), :] for j in range(n_slab)], axis=1).astype(BF16)


def _s5(u3, bb, cc, ar, ai, dvec, wglu, bglu, *, ts, chunk):
    batch, seq, d_ssm = u3.shape
    n_state = ar.shape[1]
    kern = functools.partial(_s5_kernel, batch=batch, ts=ts, n_state=n_state, chunk=chunk)
    return pl.pallas_call(
        kern,
        grid=(seq // ts,),
        in_specs=[
            pl.BlockSpec((batch, ts, d_ssm), lambda i: (0, i, 0)),
            _const_spec(bb.shape), _const_spec(cc.shape), _const_spec(ar.shape), _const_spec(ai.shape),
            _const_spec(dvec.shape), _const_spec(wglu.shape), _const_spec(bglu.shape),
        ],
        out_specs=pl.BlockSpec((batch, ts, d_ssm), lambda i: (0, i, 0)),
        out_shape=jax.ShapeDtypeStruct((batch, seq, d_ssm), BF16),
        scratch_shapes=[
            pltpu.VMEM((d_ssm // LANES, ts * batch, LANES), F32),
            pltpu.VMEM((ts * batch, 2 * n_state), F32),
            pltpu.VMEM((batch, 2 * n_state), F32),
            pltpu.VMEM((d_ssm // LANES, ts * batch, LANES), F32),
        ],
        compiler_params=pltpu.CompilerParams(
            dimension_semantics=("arbitrary",), vmem_limit_bytes=VMEM_LIMIT),
    )(u3, bb, cc, ar, ai, dvec, wglu, bglu)


def _select_blocks(gate, blk_f, n_pick):
    sel = jnp.zeros(gate.shape, jnp.bool_)
    for _ in range(n_pick):
        mx = jnp.max(gate, axis=0, keepdims=True)
        idx = jnp.min(jnp.where(gate == mx, blk_f, float(gate.shape[0])), axis=0, keepdims=True)
        hit = blk_f == idx
        sel = sel | hit
        gate = jnp.where(hit, -jnp.inf, gate)
    return sel


def _moba_kernel(farb_ref, qT_ref, k_ref, vT_ref, kmean_ref, bias_ref, o_ref,
                 mask_ref, s_ref, p0_ref, p1_ref, acc_ref, *, nb):
    hg = pl.program_id(1)
    i = pl.program_id(2)
    tq = qT_ref.shape[2]
    hw = qT_ref.shape[1]
    nh = hw // HEAD_DIM
    qT = qT_ref[0]
    row = lax.broadcasted_iota(jnp.int32, qT.shape, 0)
    km = kmean_ref[0]
    km_hi = km.astype(BF16)
    km_lo = (km - km_hi.astype(F32)).astype(BF16)
    blk = lax.broadcasted_iota(jnp.int32, (nb, tq), 0)
    blk_f = blk.astype(F32)
    prev = jnp.maximum(i - 1, 0)
    n_far = jnp.maximum(i - 1, 0)
    n_steps = (n_far + 1) // 2
    n_iter = n_steps // 2

    gw = min(hw, MXU_DEPTH)
    groups = [slice(j * HEAD_DIM // gw * gw, j * HEAD_DIM // gw * gw + gw) for j in range(nh)]

    def qk(j, keys):
        return jnp.dot(keys[:, groups[j]], qTs[j], preferred_element_type=F32)

    qTs, near_masks = [], []
    for j in range(nh):
        qTh = jnp.where((row >= j * HEAD_DIM) & (row < (j + 1) * HEAD_DIM), qT, jnp.zeros_like(qT))
        qTs.append(qTh[groups[j]])
        gate = qk(j, km_hi) + qk(j, km_lo)
        gate = jnp.where(blk < i, gate, NEG_INF)
        sel = _select_blocks(gate, blk_f, MOBA_TOP_K) & (blk < i)
        mask_ref[j] = jnp.where(sel & (blk < i - 1), farb_ref[nh * hg + j], NEG_INF)
        mprev = jnp.max(jnp.where(sel & (blk == i - 1), 0.0, NEG_INF), axis=0, keepdims=True)
        near_masks.append((mprev, jnp.zeros_like(mprev)))

    def far_keys(n0):
        n0 = pl.multiple_of(n0, 2)
        return k_ref[pl.ds(n0, 2)].reshape(2 * MOBA_BLOCK, hw)

    def far_masks(j, n0):
        return mask_ref[j, pl.ds(n0, 1), :], mask_ref[j, pl.ds(n0 + 1, 1), :]

    def block_max(s, masks):
        return jnp.maximum(jnp.max(s[:MOBA_BLOCK], axis=0, keepdims=True) + masks[0],
                           jnp.max(s[MOBA_BLOCK:], axis=0, keepdims=True) + masks[1])

    def probs(j, p_ref, s, shift, masks):
        p_ref[j, :MOBA_BLOCK] = jnp.exp2((s[:MOBA_BLOCK] - (shift - masks[0])).astype(BF16))
        p_ref[j, MOBA_BLOCK:] = jnp.exp2((s[MOBA_BLOCK:] - (shift - masks[1])).astype(BF16))

    def pv(j, p_ref, va, vb):
        vrows = slice(j * V_ROWS, (j + 1) * V_ROWS)
        vv = jnp.concatenate([va[vrows, :], vb[vrows, :]], axis=1)
        acc_ref[j] += jnp.dot(vv, p_ref[j], preferred_element_type=F32)

    near_keys = jnp.concatenate([k_ref[prev], k_ref[i]], axis=0)
    near_max = []
    for j in range(nh):
        s = qk(j, near_keys) + bias_ref[j]
        s_ref[j] = s
        near_max.append(block_max(s, near_masks[j]))

    def attend(shift):
        acc_ref[...] = jnp.zeros_like(acc_ref)
        first = far_keys(0)
        for j in range(nh):
            probs(j, p0_ref, s_ref[j], shift[j], near_masks[j])
            probs(j, p1_ref, qk(j, first), shift[j], far_masks(j, 0))

        def trip(g, carry):
            f0 = 4 * g
            n2 = f0 + 2
            n3 = jnp.minimum(f0 + 4, nb - 2)
            va0 = vT_ref[jnp.where(g == 0, prev, f0 - 2)]
            vb0 = vT_ref[jnp.where(g == 0, i, f0 - 1)]
            va1 = vT_ref[f0]
            vb1 = vT_ref[f0 + 1]
            k2 = far_keys(n2)
            k3_ = far_keys(n3)
            for j in range(nh):
                pv(j, p0_ref, va0, vb0)
                probs(j, p0_ref, qk(j, k2), shift[j], far_masks(j, n2))
                pv(j, p1_ref, va1, vb1)
                probs(j, p1_ref, qk(j, k3_), shift[j], far_masks(j, n3))
            return carry

        lax.fori_loop(0, n_iter, trip, 0)
        last = 4 * n_iter
        va = vT_ref[jnp.where(n_iter == 0, prev, last - 2)]
        vb = vT_ref[jnp.where(n_iter == 0, i, last - 1)]
        for j in range(nh):
            pv(j, p0_ref, va, vb)

        @pl.when(n_steps % 2 == 1)
        def _():
            va1 = vT_ref[last]
            vb1 = vT_ref[last + 1]
            for j in range(nh):
                pv(j, p1_ref, va1, vb1)

    attend(near_max)
    acc = acc_ref[...]
    overflow = jnp.max(jnp.where(jnp.isfinite(acc), 0.0, 1.0)) > 0.0

    @pl.when(overflow)
    def _():
        def far_max(n, m_run):
            kp = k_ref[n]
            return tuple(jnp.maximum(m_run[j], jnp.max(qk(j, kp), axis=0, keepdims=True) + mask_ref[j, pl.ds(n, 1), :])
                         for j in range(nh))

        attend(lax.fori_loop(0, n_far, far_max, tuple(near_max)))

    outs = [acc_ref[j][:HEAD_DIM] / acc_ref[j][HEAD_DIM:HEAD_DIM + 1] for j in range(nh)]
    o_ref[0] = jnp.concatenate(outs, axis=0).T.astype(BF16)


def _moba(farb, qT, k3, vT3, kmean, bias, *, batch, seq):
    nb = seq // MOBA_BLOCK
    assert nb % 4 == 0, "far blocks are consumed four per loop trip"
    d_attn = qT.shape[1]
    nh = MOBA_HEADS_PER_STEP
    hw = nh * HEAD_DIM
    kern = functools.partial(_moba_kernel, nb=nb)
    return pl.pallas_call(
        kern,
        grid=(batch, d_attn // hw, nb),
        in_specs=[
            pl.BlockSpec(memory_space=pltpu.SMEM),
            pl.BlockSpec((1, hw, MOBA_BLOCK), lambda b, h, i: (b, h, i)),
            pl.BlockSpec((nb, MOBA_BLOCK, hw), lambda b, h, i: (b, 0, h)),
            pl.BlockSpec((nb, nh * V_ROWS, MOBA_BLOCK), lambda b, h, i: (b, h, 0)),
            pl.BlockSpec((1, nb, hw), lambda b, h, i: (b, 0, h)),
            pl.BlockSpec((nh, 2 * MOBA_BLOCK, MOBA_BLOCK), lambda b, h, i: (h, 0, 0), pipeline_mode=pl.Buffered(1)),
        ],
        out_specs=pl.BlockSpec((1, MOBA_BLOCK, hw), lambda b, h, i: (b, i, h)),
        out_shape=jax.ShapeDtypeStruct((batch, seq, d_attn), BF16),
        scratch_shapes=[
            pltpu.VMEM((nh, nb, MOBA_BLOCK), F32),
            pltpu.VMEM((nh, 2 * MOBA_BLOCK, MOBA_BLOCK), F32),
            pltpu.VMEM((nh, 2 * MOBA_BLOCK, MOBA_BLOCK), BF16),
            pltpu.VMEM((nh, 2 * MOBA_BLOCK, MOBA_BLOCK), BF16),
            pltpu.VMEM((nh, V_ROWS, MOBA_BLOCK), F32),
        ],
        compiler_params=pltpu.CompilerParams(
            dimension_semantics=("arbitrary", "arbitrary", "arbitrary"),
            vmem_limit_bytes=VMEM_LIMIT),
    )(farb, qT, k3, vT3, kmean, bias)


def _route(logits):
    lane = lax.broadcasted_iota(jnp.int32, logits.shape, 1)
    lane_f = lane.astype(F32)
    big = float(ROUTER_LANES)
    ninf = -jnp.inf
    gmask = lane < N_GROUPS
    gmax = jnp.max(jnp.where(gmask, logits, ninf), axis=1, keepdims=True)
    gsum = jnp.sum(jnp.where(gmask, jnp.exp(logits - gmax), 0.0), axis=1, keepdims=True)
    g_val = 1.0 / gsum
    g_idx = jnp.min(jnp.where(gmask & (logits == gmax), lane_f, big), axis=1, keepdims=True)
    e_grp = ((lane - EXPERT_LANE0) >> 3).astype(F32)
    emask = (lane >= EXPERT_LANE0) & (lane < EXPERT_LANE0 + N_EXPERTS) & (e_grp == g_idx)
    el = jnp.where(emask, logits, ninf)
    e1 = jnp.max(el, axis=1, keepdims=True)
    i1 = jnp.min(jnp.where(emask & (el == e1), lane_f, big), axis=1, keepdims=True)
    el2 = jnp.where(lane_f == i1, ninf, el)
    e2 = jnp.max(el2, axis=1, keepdims=True)
    i2 = jnp.min(jnp.where(emask & (lane_f != i1) & (el2 == e2), lane_f, big), axis=1, keepdims=True)
    tt = jnp.exp(e2 - e1)
    w1 = g_val / (1.0 + tt)
    w2 = g_val * tt / (1.0 + tt)
    first_low = i1 < i2
    lo = jnp.minimum(i1, i2) - (EXPERT_LANE0 + EXPERTS_PER_GROUP * g_idx)
    hi = jnp.maximum(i1, i2) - (EXPERT_LANE0 + EXPERTS_PER_GROUP * g_idx)
    pair = lo * (2 * EXPERTS_PER_GROUP - 1 - lo) * 0.5 + (hi - lo - 1.0)
    cls = g_idx * PAIRS_PER_GROUP + pair
    return cls, jnp.where(first_low, w1, w2), jnp.where(first_low, w2, w1)


def _col_to_row(col):
    n = col.shape[0]
    r = lax.broadcasted_iota(jnp.int32, (n, n), 0)
    c = lax.broadcasted_iota(jnp.int32, (n, n), 1)
    return jnp.sum(jnp.where(r == c, col, 0.0), axis=0, keepdims=True)


def _merge_kernel(x_ref, ys_ref, ya_ref, gates_ref, wus_ref, wua_ref, wout_ref, g2_ref, wr_ref, br_ref,
                  tri_ref, x1e_ref, cls_ref, rank_ref, cnt_ref):
    d = x_ref.shape[1]
    a = jnp.dot(ys_ref[...], wus_ref[...], preferred_element_type=F32)
    b = jnp.dot(ya_ref[...], wua_ref[...], preferred_element_type=F32)
    gts = gates_ref[...].astype(F32)
    merged = _sigmoid(gts[:, :d]) * a + _sigmoid(gts[:, d:]) * b
    x1 = x_ref[...] + jnp.dot(merged.astype(BF16), wout_ref[...], preferred_element_type=F32)
    xn = _rms(x1, g2_ref[...]).astype(BF16)
    logits = jnp.dot(xn, wr_ref[...], preferred_element_type=F32) + br_ref[...]
    cls, wa, wb = _route(logits)
    lane = lax.broadcasted_iota(jnp.int32, logits.shape, 1)
    x1e_ref[:, :d] = x1
    x1e_ref[:, d:] = jnp.where(lane == 0, wa, jnp.where(lane == 1, wb, 0.0))
    onehot = lane.astype(F32) == cls
    before = jnp.dot(tri_ref[...], onehot.astype(BF16), preferred_element_type=F32)
    rank = jnp.sum(jnp.where(onehot, before, 0.0), axis=1, keepdims=True)
    cls_ref[0] = _col_to_row(cls).astype(jnp.int32)
    rank_ref[0] = _col_to_row(rank).astype(jnp.int32)
    cnt_ref[0] = jnp.sum(onehot.astype(F32), axis=0, keepdims=True).astype(jnp.int32)


def _merge(x2, ys, ya, gates, wus, wua, wout, g2, wr, br, *, tm):
    t, d = x2.shape
    nt = t // tm
    tri = jnp.tril(jnp.ones((tm, tm), F32), -1).astype(BF16)
    return pl.pallas_call(
        _merge_kernel,
        grid=(nt,),
        in_specs=[
            pl.BlockSpec((tm, d), lambda i: (i, 0)),
            pl.BlockSpec((tm, ys.shape[1]), lambda i: (i, 0)),
            pl.BlockSpec((tm, ya.shape[1]), lambda i: (i, 0)),
            pl.BlockSpec((tm, gates.shape[1]), lambda i: (i, 0)),
            _const_spec(wus.shape), _const_spec(wua.shape), _const_spec(wout.shape),
            _const_spec(g2.shape), _const_spec(wr.shape), _const_spec(br.shape), _const_spec(tri.shape),
        ],
        out_specs=[
            pl.BlockSpec((tm, d + ROUTER_LANES), lambda i: (i, 0)),
            pl.BlockSpec((1, 1, tm), lambda i: (i, 0, 0)),
            pl.BlockSpec((1, 1, tm), lambda i: (i, 0, 0)),
            pl.BlockSpec((1, 1, ROUTER_LANES), lambda i: (i, 0, 0)),
        ],
        out_shape=[
            jax.ShapeDtypeStruct((t, d + ROUTER_LANES), F32),
            jax.ShapeDtypeStruct((nt, 1, tm), jnp.int32),
            jax.ShapeDtypeStruct((nt, 1, tm), jnp.int32),
            jax.ShapeDtypeStruct((nt, 1, ROUTER_LANES), jnp.int32),
        ],
        compiler_params=pltpu.CompilerParams(
            dimension_semantics=("arbitrary",), vmem_limit_bytes=VMEM_LIMIT),
    )(x2, ys, ya, gates, wus, wua, wout, g2, wr, br, tri)


def _permute_rows(pos, src, *, scatter):
    n, width = src.shape
    w = SC_WINDOW
    idx = jnp.pad(pos.reshape(n // w, w), ((0, 0), (0, LANES - w)))
    mesh = plsc.VectorSubcoreMesh(core_axis_name="core", subcore_axis_name="subcore")
    idx_spec = pl.BlockSpec((1, LANES), index_map=lambda i: (i, 0))
    row_spec = pl.BlockSpec((w, width), index_map=lambda i: (i, 0))

    @pl.kernel(out_type=jax.ShapeDtypeStruct(src.shape, src.dtype), mesh=mesh, scratch_types=[])
    def permute(src_hbm, idx_hbm, dst_hbm):
        if scatter:
            def body(rows_vmem, idx_vmem):
                pltpu.sync_copy(rows_vmem, dst_hbm.at[idx_vmem.at[0, pl.ds(0, w)]])
            in_specs, out_specs, args = [row_spec, idx_spec], [], (src_hbm, idx_hbm)
        else:
            def body(idx_vmem, rows_vmem):
                pltpu.sync_copy(src_hbm.at[idx_vmem.at[0, pl.ds(0, w)]], rows_vmem)
            in_specs, out_specs, args = [idx_spec], [row_spec], (idx_hbm, dst_hbm)
        pltpu.emit_pipeline(body, grid=(n // w,), in_specs=in_specs, out_specs=out_specs,
                            core_axis_name=("core", "subcore"),
                            dimension_semantics=(pltpu.PARALLEL,))(*args)

    return permute(src, idx)


def _expert(xn, w1_ref, w3_ref, w2_ref):
    h1 = jnp.dot(xn, w1_ref[0], preferred_element_type=F32)
    h3 = jnp.dot(xn, w3_ref[0], preferred_element_type=F32)
    hid = (h1 * _sigmoid(h1) * h3).astype(BF16)
    return jnp.dot(hid, w2_ref[0], preferred_element_type=F32)


def _moe_kernel(tile_ref, ea_ref, eb_ref, lo_ref, hi_ref, first_ref, last_ref,
                x_ref, w1a_ref, w3a_ref, w2a_ref, w1b_ref, w3b_ref, w2b_ref, g2_ref, gf_ref,
                y_ref, acc_ref):
    w = pl.program_id(0)
    d = y_ref.shape[1]
    lo = lo_ref[w]
    hi = hi_ref[w]

    @pl.when(hi > lo)
    def _():
        x1 = x_ref[:, :d]
        row = lax.broadcasted_iota(jnp.int32, (x1.shape[0], 1), 0)
        inseg = (row >= lo) & (row < hi)
        ca = jnp.where(inseg, x_ref[:, d:d + 1], 0.0)
        cb = jnp.where(inseg, x_ref[:, d + 1:d + 2], 0.0)
        xn = _rms(x1, g2_ref[...]).astype(BF16)
        contrib = (ca * _expert(xn, w1a_ref, w3a_ref, w2a_ref)
                   + cb * _expert(xn, w1b_ref, w3b_ref, w2b_ref))

        @pl.when(first_ref[w] == 1)
        def _():
            acc_ref[...] = contrib

        @pl.when(first_ref[w] == 0)
        def _():
            acc_ref[...] += contrib

        @pl.when(last_ref[w] == 1)
        def _():
            y_ref[...] = _rms(x1 + acc_ref[...], gf_ref[...])


def _moe(items, x1s, w1, w3, w2, g2, gf, *, rows):
    t, de_ = x1s.shape
    d = g2.shape[1]
    ne, _, de = w1.shape
    n_items = items[0].shape[0]

    def wa_map(w, tile, ea, eb, lo, hi, first, last):
        return (ea[w], 0, 0)

    def wb_map(w, tile, ea, eb, lo, hi, first, last):
        return (eb[w], 0, 0)

    def row_map(w, tile, ea, eb, lo, hi, first, last):
        return (tile[w], 0)

    def const_map(w, *_):
        return (0, 0)

    grid_spec = pltpu.PrefetchScalarGridSpec(
        num_scalar_prefetch=len(items),
        grid=(n_items,),
        in_specs=[
            pl.BlockSpec((rows, de_), row_map),
            pl.BlockSpec((1, d, de), wa_map), pl.BlockSpec((1, d, de), wa_map), pl.BlockSpec((1, de, d), wa_map),
            pl.BlockSpec((1, d, de), wb_map), pl.BlockSpec((1, d, de), wb_map), pl.BlockSpec((1, de, d), wb_map),
            pl.BlockSpec(g2.shape, const_map), pl.BlockSpec(gf.shape, const_map),
        ],
        out_specs=pl.BlockSpec((rows, d), row_map),
        scratch_shapes=[pltpu.VMEM((rows, d), F32)],
    )
    return pl.pallas_call(
        _moe_kernel,
        grid_spec=grid_spec,
        out_shape=jax.ShapeDtypeStruct((t, d), F32),
        compiler_params=pltpu.CompilerParams(
            dimension_semantics=("arbitrary",), vmem_limit_bytes=VMEM_LIMIT),
    )(*items, x1s, w1, w3, w2, w1, w3, w2, g2, gf)


def _pair_table():
    lo, hi = [], []
    for a in range(EXPERTS_PER_GROUP):
        for b in range(a + 1, EXPERTS_PER_GROUP):
            lo.append(a)
            hi.append(b)
    return jnp.asarray(lo, jnp.int32), jnp.asarray(hi, jnp.int32)


def _before_sum(v):
    ix = jnp.arange(v.shape[0])
    return jnp.sum(jnp.where(ix[None, :] < ix[:, None], v[None, :], 0), axis=1)


def _place(dest, vals, n):
    return jnp.sum(jnp.where(dest[None, :] == jnp.arange(n)[:, None], vals[None, :], 0), axis=1)


def _routing_tables(cls, rank, cnt, *, rows):
    nt, _, tm = cls.shape
    t = nt * tm
    lanes = cnt.shape[2]
    c = cnt[:, 0, :]
    tot = jnp.sum(c, axis=0)
    class_off = _before_sum(tot)
    tile_ix = jnp.arange(nt)
    tiles_before = jnp.sum(jnp.where((tile_ix[None, :] < tile_ix[:, None])[:, :, None], c[None], 0), axis=1)
    base = class_off[None, :] + tiles_before
    onehot = cls[:, 0, :, None] == jnp.arange(lanes)[None, None, :]
    pos = jnp.sum(jnp.where(onehot, base[:, None, :], 0), axis=2) + rank[:, 0, :]

    n_tiles = t // rows
    ta = jnp.arange(n_tiles, dtype=jnp.int32) * rows
    cb = class_off[1:N_CLASSES]
    n_items = n_tiles + N_CLASSES - 1
    idx = jnp.arange(n_items)
    at_a = jnp.arange(n_tiles) + jnp.sum(cb[None, :] <= ta[:, None], axis=1)
    at_b = jnp.arange(N_CLASSES - 1) + jnp.sum(ta[None, :] < cb[:, None], axis=1)
    starts = _place(at_a, ta, n_items) + _place(at_b, cb, n_items)
    ends = jnp.concatenate([starts[1:], jnp.full((1,), t, jnp.int32)])
    real = ends > starts
    n_real = jnp.sum(real)
    dest = jnp.where(real, _before_sum(real.astype(jnp.int32)), n_real + _before_sum(1 - real.astype(jnp.int32)))
    starts, ends = _place(dest, starts, n_items), _place(dest, ends, n_items)
    valid = idx < n_real
    last_real = jnp.maximum(n_real - 1, 0)
    starts = jnp.where(valid, starts, jnp.sum(jnp.where(idx == last_real, starts, 0)))
    tile = jnp.minimum(starts // rows, n_tiles - 1)
    klass = jnp.sum(class_off[None, :N_CLASSES] <= starts[:, None], axis=1) - 1
    pair_lo, pair_hi = _pair_table()
    grp = klass // PAIRS_PER_GROUP
    pair_hot = (klass % PAIRS_PER_GROUP)[:, None] == jnp.arange(PAIRS_PER_GROUP)[None, :]
    ea = grp * EXPERTS_PER_GROUP + jnp.sum(jnp.where(pair_hot, pair_lo[None, :], 0), axis=1)
    eb = grp * EXPERTS_PER_GROUP + jnp.sum(jnp.where(pair_hot, pair_hi[None, :], 0), axis=1)
    lo = jnp.where(valid, starts - tile * rows, 0)
    hi = jnp.where(valid, ends - tile * rows, 0)
    tile_prev = jnp.concatenate([tile[:1], tile[:-1]])
    tile_next = jnp.concatenate([tile[1:], tile[-1:]])
    first = valid & ((idx == 0) | (tile != tile_prev))
    last = valid & ((idx == last_real) | (tile != tile_next))
    items = tuple(a.astype(jnp.int32) for a in (tile, ea, eb, lo, hi, first, last))
    return pos.reshape(t).astype(jnp.int32), items


def _t5_bucket(dist):
    n = jnp.maximum(dist, 0)
    max_exact = NUM_BUCKETS // 2
    large = max_exact + (jnp.log(jnp.maximum(n, max_exact).astype(F32) / max_exact)
                         / math.log(MAX_DISTANCE / max_exact)
                         * (NUM_BUCKETS - max_exact)).astype(jnp.int32)
    return jnp.where(n < max_exact, n, jnp.minimum(large, NUM_BUCKETS - 1))


def _bias_tables(rel_bias):
    kk = jnp.arange(MOBA_BLOCK)[:, None]
    qq = jnp.arange(MOBA_BLOCK)[None, :]
    rb = rel_bias.astype(F32) * LOG2E
    d_own = qq - kk

    def lookup(dist):
        oh = jax.nn.one_hot(_t5_bucket(dist), NUM_BUCKETS, dtype=F32)
        return jnp.einsum('hn,kqn->hkq', rb, oh, precision=lax.Precision.HIGHEST)

    own = jnp.where((d_own >= 0)[None], lookup(d_own), NEG_INF)
    prv = lookup(d_own + MOBA_BLOCK)
    tab = jnp.concatenate([prv, own], axis=1)
    far = rb[:, NUM_BUCKETS - 1]
    return tab, far


def _ssm_params(lam_re, lam_im, log_step, b_re, b_im, c_re, c_im):
    g, p = lam_re.shape
    step = jnp.exp(log_step)[:, None]
    decay = jnp.exp(lam_re * step)
    a_re = decay * jnp.cos(lam_im * step)
    a_im = decay * jnp.sin(lam_im * step)
    denom = lam_re * lam_re + lam_im * lam_im
    nr, ni = a_re - 1.0, a_im
    coef_re = (nr * lam_re + ni * lam_im) / denom
    coef_im = (ni * lam_re - nr * lam_im) / denom
    bb_re = coef_re[..., None] * b_re - coef_im[..., None] * b_im
    bb_im = coef_re[..., None] * b_im + coef_im[..., None] * b_re
    eye = jnp.eye(g, dtype=F32)
    hc = b_re.shape[2]

    def in_mat(m):
        return (eye[:, None, :, None] * m.transpose(0, 2, 1)[:, :, None, :]).reshape(g * hc, g * p)

    def out_mat(m):
        return (eye[:, None, :, None] * m.transpose(0, 2, 1)[:, :, None, :]).reshape(g * p, g * hc)

    pc, ps = g * hc // SSM_PARTS, g * p // SSM_PARTS
    in_re, in_im, out_re, out_im = in_mat(bb_re), in_mat(bb_im), out_mat(c_re), out_mat(c_im)
    bb = jnp.stack([jnp.concatenate([m[h * pc:(h + 1) * pc, h * ps:(h + 1) * ps] for m in (in_re, in_im)], axis=1)
                    for h in range(SSM_PARTS)]).astype(BF16)
    cc = jnp.stack([jnp.concatenate([m[h * ps:(h + 1) * ps, h * pc:(h + 1) * pc] for m in (out_re, -out_im)], axis=0)
                    for h in range(SSM_PARTS)]).astype(BF16)
    return bb, cc, a_re.reshape(1, g * p), a_im.reshape(1, g * p)


def kernel(x, ln1_g, w_in, b_gate, ssm_lambda_re, ssm_lambda_im, ssm_log_step, ssm_b_re, ssm_b_im,
           ssm_c_re, ssm_c_im, ssm_d, w_glu, b_glu, w_up_ssm, w_up_attn, rel_bias, w_out, ln2_g,
           w_router_group, b_router_group, w_router_expert, b_router_expert, w1, w3, w2, ln_f_g):
    assert w_in.shape[0] == 1, "single-layer block"
    batch, seq, d = x.shape
    t = batch * seq
    d_ssm = w_glu.shape[1]
    d_attn = w_up_attn.shape[1]
    n_heads = d_attn // HEAD_DIM
    o1, o2, o3, o4 = d_ssm, d_ssm + d_attn, d_ssm + 2 * d_attn, d_ssm + 3 * d_attn
    x2 = x.reshape(t, d)

    wl = w_in[0]
    w_main = jnp.concatenate([wl[:, :o1], wl[:, o2:o3], wl[:, o4:]], axis=1).astype(BF16)
    wqT = (wl[:, o1:o2] * (HEAD_DIM ** -0.5 * LOG2E)).T.astype(BF16)
    wv = wl[:, o3:o4].T.reshape(n_heads, HEAD_DIM, d)
    wvT = jnp.concatenate([wv, jnp.zeros((n_heads, BF16_ROWS, d), F32)], axis=1)
    wvT = wvT.reshape(n_heads * V_ROWS, d).astype(BF16)
    vb = jnp.concatenate([jnp.zeros((n_heads, HEAD_DIM, 1), F32), jnp.ones((n_heads, BF16_ROWS, 1), F32)],
                         axis=1).reshape(n_heads * V_ROWS, 1)
    u, k3, kmean, qT, vT3, gates = _inproj(
        x2, ln1_g[0][None], w_main, wqT, wvT, vb, b_gate[0][None], batch=batch, seq=seq, tm=512)

    bb, cc, ar, ai = _ssm_params(ssm_lambda_re[0], ssm_lambda_im[0], ssm_log_step[0],
                                 ssm_b_re[0], ssm_b_im[0], ssm_c_re[0], ssm_c_im[0])
    y_ssm = _s5(u.reshape(batch, seq, d_ssm), bb, cc, ar, ai, ssm_d[0].reshape(1, d_ssm),
                w_glu[0].astype(BF16), b_glu[0][None], ts=128, chunk=1024)

    bias, far = _bias_tables(rel_bias)
    y_attn = _moba(far, qT, k3, vT3, kmean.reshape(batch, seq // MOBA_BLOCK, d_attn), bias,
                   batch=batch, seq=seq)

    wr = jnp.zeros((d, ROUTER_LANES), F32)
    wr = wr.at[:, :N_GROUPS].set(w_router_group[0])
    wr = wr.at[:, EXPERT_LANE0:EXPERT_LANE0 + N_EXPERTS].set(w_router_expert[0]).astype(BF16)
    br = jnp.zeros((1, ROUTER_LANES), F32)
    br = br.at[0, :N_GROUPS].set(b_router_group[0])
    br = br.at[0, EXPERT_LANE0:EXPERT_LANE0 + N_EXPERTS].set(b_router_expert[0])
    x1e, cls, rank, cnt = _merge(
        x2, y_ssm.reshape(t, d_ssm), y_attn.reshape(t, d_attn), gates,
        w_up_ssm[0].astype(BF16), w_up_attn[0].astype(BF16), w_out[0].astype(BF16),
        ln2_g[0][None], wr, br, tm=512)

    pos, items = _routing_tables(cls, rank, cnt, rows=MOE_ROWS)
    x1s = _permute_rows(pos, x1e, scatter=True)
    ys = _moe(items, x1s, w1[0].astype(BF16), w3[0].astype(BF16), w2[0].astype(BF16),
              ln2_g[0][None], ln_f_g[None], rows=MOE_ROWS)
    y = _permute_rows(pos, ys, scatter=False)
    return y.reshape(batch, seq, d)
```

```python
import functools
import math

import jax
import jax.numpy as jnp
from jax import lax
from jax.experimental import pallas as pl
from jax.experimental.pallas import tpu as pltpu
from jax.experimental.pallas import tpu_sc as plsc

F32 = jnp.float32
BF16 = jnp.bfloat16

SSM_GROUP = 16
SSM_STATE = 64
HEAD_DIM = 64
MOBA_BLOCK = 256
MOBA_TOP_K = 3
NUM_BUCKETS = 32
MAX_DISTANCE = 128
N_GROUPS = 4
EXPERTS_PER_GROUP = 8
N_EXPERTS = N_GROUPS * EXPERTS_PER_GROUP
PAIRS_PER_GROUP = EXPERTS_PER_GROUP * (EXPERTS_PER_GROUP - 1) // 2
N_CLASSES = N_GROUPS * PAIRS_PER_GROUP
MOE_ROWS = 256
SC_WINDOW = 32
SSM_PARTS = 2
MOBA_HEADS_PER_STEP = 8
RMS_EPS = 1e-6
NEG_INF = -1e30
LOG2E = 1.4426950408889634

LANES = 128
MXU_DEPTH = 256
BF16_ROWS = 16
V_ROWS = HEAD_DIM + BF16_ROWS
ROUTER_LANES = LANES
EXPERT_LANE0 = N_GROUPS
VMEM_LIMIT = 56 * 1024 * 1024


def _sigmoid(x):
    return 1.0 / (1.0 + jnp.exp(-x))


def _rms(x, g):
    ms = jnp.mean(x * x, axis=-1, keepdims=True)
    return x * lax.rsqrt(ms + RMS_EPS) * g


def _const_spec(shape):
    n = len(shape)
    return pl.BlockSpec(shape, lambda *_: (0,) * n)


def _inproj_kernel(x_ref, g_ref, w_ref, wqT_ref, wvT_ref, vb_ref, bg_ref,
                   u_ref, k_ref, kmean_ref, qT_ref, vT_ref, gates_ref, *, nblk, d_ssm, d_attn):
    hn = _rms(x_ref[...], g_ref[...]).astype(BF16)
    acc = jnp.dot(hn, w_ref[...], preferred_element_type=F32)
    u_ref[...] = acc[:, :d_ssm]
    kf = acc[:, d_ssm:d_ssm + d_attn].reshape(nblk, MOBA_BLOCK, d_attn)
    k_ref[...] = kf.astype(BF16)
    kmean_ref[...] = jnp.mean(kf, axis=1, keepdims=True)
    gates_ref[...] = (acc[:, d_ssm + d_attn:] + bg_ref[...]).astype(BF16)
    nt = (((1,), (1,)), ((), ()))
    qT = lax.dot_general(wqT_ref[...], hn, nt, preferred_element_type=F32)
    qT_ref[0] = qT.astype(BF16)
    vT = lax.dot_general(wvT_ref[...], hn, nt, preferred_element_type=F32) + vb_ref[...]
    for j in range(nblk):
        vT_ref[j] = vT[:, j * MOBA_BLOCK:(j + 1) * MOBA_BLOCK].astype(BF16)


def _inproj(x2, g1, w_main, wqT, wvT, vb, bg, *, batch, seq, tm):
    t, d = x2.shape
    d_attn = wqT.shape[0]
    v_rows = wvT.shape[0]
    d_ssm = w_main.shape[1] - d_attn - bg.shape[1]
    nblk = tm // MOBA_BLOCK
    tiles_per_seq = seq // tm
    nb_total = t // MOBA_BLOCK
    kern = functools.partial(_inproj_kernel, nblk=nblk, d_ssm=d_ssm, d_attn=d_attn)
    return pl.pallas_call(
        kern,
        grid=(t // tm,),
        in_specs=[
            pl.BlockSpec((tm, d), lambda i: (i, 0)),
            _const_spec(g1.shape),
            _const_spec(w_main.shape),
            _const_spec(wqT.shape),
            _const_spec(wvT.shape),
            _const_spec(vb.shape),
            _const_spec(bg.shape),
        ],
        out_specs=[
            pl.BlockSpec((tm, d_ssm), lambda i: (i, 0)),
            pl.BlockSpec((nblk, MOBA_BLOCK, d_attn), lambda i: (i, 0, 0)),
            pl.BlockSpec((nblk, 1, d_attn), lambda i: (i, 0, 0)),
            pl.BlockSpec((1, d_attn, tm), lambda i: (i // tiles_per_seq, 0, i % tiles_per_seq)),
            pl.BlockSpec((nblk, v_rows, MOBA_BLOCK), lambda i: (i, 0, 0)),
            pl.BlockSpec((tm, bg.shape[1]), lambda i: (i, 0)),
        ],
        out_shape=[
            jax.ShapeDtypeStruct((t, d_ssm), F32),
            jax.ShapeDtypeStruct((nb_total, MOBA_BLOCK, d_attn), BF16),
            jax.ShapeDtypeStruct((nb_total, 1, d_attn), F32),
            jax.ShapeDtypeStruct((batch, d_attn, seq), BF16),
            jax.ShapeDtypeStruct((nb_total, v_rows, MOBA_BLOCK), BF16),
            jax.ShapeDtypeStruct((t, bg.shape[1]), BF16),
        ],
        compiler_params=pltpu.CompilerParams(
            dimension_semantics=("arbitrary",), vmem_limit_bytes=VMEM_LIMIT),
    )(x2, g1, w_main, wqT, wvT, vb, bg)


def _s5_kernel(u_ref, bb_ref, cc_ref, ar_ref, ai_ref, d_ref, wglu_ref, bglu_ref,
               y_ref, utb_ref, sbuf_ref, state_ref, ytb_ref, *, batch, ts, n_state, chunk):
    d_ssm = u_ref.shape[2]
    n_slab = d_ssm // LANES

    @pl.when(pl.program_id(0) == 0)
    def _():
        state_ref[...] = jnp.zeros_like(state_ref)

    for b in range(batch):
        ub = u_ref[b]
        for j in range(n_slab):
            utb_ref[j, pl.ds(b, ts, stride=batch), :] = ub[:, j * LANES:(j + 1) * LANES]
    u_tb = jnp.concatenate([utb_ref[j] for j in range(n_slab)], axis=1)
    n_part, pc, pw = bb_ref.shape
    ps = pw // 2
    u_bf = u_tb.astype(BF16)
    for h in range(n_part):
        sbuf_ref[:, h * pw:(h + 1) * pw] = jnp.dot(u_bf[:, h * pc:(h + 1) * pc], bb_ref[h],
                                                   preferred_element_type=F32)

    for c in range(n_state // chunk):
        h, cc_ = divmod(c * chunk, ps)
        re = slice(h * pw + cc_, h * pw + cc_ + chunk)
        im = slice(h * pw + ps + cc_, h * pw + ps + cc_ + chunk)
        ar = jnp.broadcast_to(ar_ref[:, c * chunk:(c + 1) * chunk], (batch, chunk))
        ai = jnp.broadcast_to(ai_ref[:, c * chunk:(c + 1) * chunk], (batch, chunk))

        def step(t, carry, re=re, im=im, ar=ar, ai=ai):
            hr, hi = carry
            r0 = pl.multiple_of(t * batch, batch)
            nhr = ar * hr - ai * hi + sbuf_ref[pl.ds(r0, batch), re]
            nhi = ar * hi + ai * hr + sbuf_ref[pl.ds(r0, batch), im]
            sbuf_ref[pl.ds(r0, batch), re] = nhr
            sbuf_ref[pl.ds(r0, batch), im] = nhi
            return nhr, nhi

        hr, hi = lax.fori_loop(0, ts, step, (state_ref[:, re], state_ref[:, im]), unroll=8)
        state_ref[:, re] = hr
        state_ref[:, im] = hi

    y = jnp.concatenate(
        [jnp.dot(sbuf_ref[:, h * pw:(h + 1) * pw].astype(BF16), cc_ref[h], preferred_element_type=F32)
         for h in range(n_part)], axis=1)
    y = y + d_ref[...] * u_tb
    g = jax.nn.gelu(y)
    z = jnp.dot(g.astype(BF16), wglu_ref[...], preferred_element_type=F32) + bglu_ref[...]
    out = g * _sigmoid(z)
    for j in range(n_slab):
        ytb_ref[j] = out[:, j * LANES:(j + 1) * LANES]
    for b in range(batch):
        y_ref[b] = jnp.concatenate(
            [ytb_ref[j, pl.ds(b, ts, stride=batch), :] for j in range(n_slab)], axis=1).astype(BF16)


def _s5(u3, bb, cc, ar, ai, dvec, wglu, bglu, *, ts, chunk):
    batch, seq, d_ssm = u3.shape
    n_state = ar.shape[1]
    kern = functools.partial(_s5_kernel, batch=batch, ts=ts, n_state=n_state, chunk=chunk)
    return pl.pallas_call(
        kern,
        grid=(seq // ts,),
        in_specs=[
            pl.BlockSpec((batch, ts, d_ssm), lambda i: (0, i, 0)),
            _const_spec(bb.shape), _const_spec(cc.shape), _const_spec(ar.shape), _const_spec(ai.shape),
            _const_spec(dvec.shape), _const_spec(wglu.shape), _const_spec(bglu.shape),
        ],
        out_specs=pl.BlockSpec((batch, ts, d_ssm), lambda i: (0, i, 0)),
        out_shape=jax.ShapeDtypeStruct((batch, seq, d_ssm), BF16),
        scratch_shapes=[
            pltpu.VMEM((d_ssm // LANES, ts * batch, LANES), F32),
            pltpu.VMEM((ts * batch, 2 * n_state), F32),
            pltpu.VMEM((batch, 2 * n_state), F32),
            pltpu.VMEM((d_ssm // LANES, ts * batch, LANES), F32),
        ],
        compiler_params=pltpu.CompilerParams(
            dimension_semantics=("arbitrary",), vmem_limit_bytes=VMEM_LIMIT),
    )(u3, bb, cc, ar, ai, dvec, wglu, bglu)


def _select_blocks(gate, blk_f, n_pick):
    sel = jnp.zeros(gate.shape, jnp.bool_)
    for _ in range(n_pick):
        mx = jnp.max(gate, axis=0, keepdims=True)
        idx = jnp.min(jnp.where(gate == mx, blk_f, float(gate.shape[0])), axis=0, keepdims=True)
        hit = blk_f == idx
        sel = sel | hit
        gate = jnp.where(hit, -jnp.inf, gate)
    return sel


def _moba_kernel(farb_ref, qT_ref, k_ref, vT_ref, kmean_ref, bias_ref, o_ref,
                 mask_ref, s_ref, p0_ref, p1_ref, acc_ref, *, nb):
    hg = pl.program_id(1)
    i = pl.program_id(2)
    tq = qT_ref.shape[2]
    hw = qT_ref.shape[1]
    nh = hw // HEAD_DIM
    qT = qT_ref[0]
    row = lax.broadcasted_iota(jnp.int32, qT.shape, 0)
    km = kmean_ref[0]
    km_hi = km.astype(BF16)
    km_lo = (km - km_hi.astype(F32)).astype(BF16)
    blk = lax.broadcasted_iota(jnp.int32, (nb, tq), 0)
    blk_f = blk.astype(F32)
    prev = jnp.maximum(i - 1, 0)
    n_far = jnp.maximum(i - 1, 0)
    n_steps = (n_far + 1) // 2
    n_iter = n_steps // 2

    gw = min(hw, MXU_DEPTH)
    groups = [slice(j * HEAD_DIM // gw * gw, j * HEAD_DIM // gw * gw + gw) for j in range(nh)]

    def qk(j, keys):
        return jnp.dot(keys[:, groups[j]], qTs[j], preferred_element_type=F32)

    qTs, near_masks = [], []
    for j in range(nh):
        qTh = jnp.where((row >= j * HEAD_DIM) & (row < (j + 1) * HEAD_DIM), qT, jnp.zeros_like(qT))
        qTs.append(qTh[groups[j]])
        gate = qk(j, km_hi) + qk(j, km_lo)
        gate = jnp.where(blk < i, gate, NEG_INF)
        sel = _select_blocks(gate, blk_f, MOBA_TOP_K) & (blk < i)
        mask_ref[j] = jnp.where(sel & (blk < i - 1), farb_ref[nh * hg + j], NEG_INF)
        mprev = jnp.max(jnp.where(sel & (blk == i - 1), 0.0, NEG_INF), axis=0, keepdims=True)
        near_masks.append((mprev, jnp.zeros_like(mprev)))

    def far_keys(n0):
        n0 = pl.multiple_of(n0, 2)
        return k_ref[pl.ds(n0, 2)].reshape(2 * MOBA_BLOCK, hw)

    def far_masks(j, n0):
        return mask_ref[j, pl.ds(n0, 1), :], mask_ref[j, pl.ds(n0 + 1, 1), :]

    def block_max(s, masks):
        return jnp.maximum(jnp.max(s[:MOBA_BLOCK], axis=0, keepdims=True) + masks[0],
                           jnp.max(s[MOBA_BLOCK:], axis=0, keepdims=True) + masks[1])

    def probs(j, p_ref, s, shift, masks):
        p_ref[j, :MOBA_BLOCK] = jnp.exp2((s[:MOBA_BLOCK] - (shift - masks[0])).astype(BF16))
        p_ref[j, MOBA_BLOCK:] = jnp.exp2((s[MOBA_BLOCK:] - (shift - masks[1])).astype(BF16))

    def pv(j, p_ref, va, vb):
        vrows = slice(j * V_ROWS, (j + 1) * V_ROWS)
        vv = jnp.concatenate([va[vrows, :], vb[vrows, :]], axis=1)
        acc_ref[j] += jnp.dot(vv, p_ref[j], preferred_element_type=F32)

    near_keys = jnp.concatenate([k_ref[prev], k_ref[i]], axis=0)
    near_max = []
    for j in range(nh):
        s = qk(j, near_keys) + bias_ref[j]
        s_ref[j] = s
        near_max.append(block_max(s, near_masks[j]))

    def attend(shift):
        acc_ref[...] = jnp.zeros_like(acc_ref)
        first = far_keys(0)
        for j in range(nh):
            probs(j, p0_ref, s_ref[j], shift[j], near_masks[j])
            probs(j, p1_ref, qk(j, first), shift[j], far_masks(j, 0))

        def trip(g, carry):
            f0 = 4 * g
            n2 = f0 + 2
            n3 = jnp.minimum(f0 + 4, nb - 2)
            va0 = vT_ref[jnp.where(g == 0, prev, f0 - 2)]
            vb0 = vT_ref[jnp.where(g == 0, i, f0 - 1)]
            va1 = vT_ref[f0]
            vb1 = vT_ref[f0 + 1]
            k2 = far_keys(n2)
            k3_ = far_keys(n3)
            for j in range(nh):
                pv(j, p0_ref, va0, vb0)
                probs(j, p0_ref, qk(j, k2), shift[j], far_masks(j, n2))
                pv(j, p1_ref, va1, vb1)
                probs(j, p1_ref, qk(j, k3_), shift[j], far_masks(j, n3))
            return carry

        lax.fori_loop(0, n_iter, trip, 0)
        last = 4 * n_iter
        va = vT_ref[jnp.where(n_iter == 0, prev, last - 2)]
        vb = vT_ref[jnp.where(n_iter == 0, i, last - 1)]
        for j in range(nh):
            pv(j, p0_ref, va, vb)

        @pl.when(n_steps % 2 == 1)
        def _():
            va1 = vT_ref[last]
            vb1 = vT_ref[last + 1]
            for j in range(nh):
                pv(j, p1_ref, va1, vb1)

    attend(near_max)
    acc = acc_ref[...]
    overflow = jnp.max(jnp.where(jnp.isfinite(acc), 0.0, 1.0)) > 0.0

    @pl.when(overflow)
    def _():
        def far_max(n, m_run):
            kp = k_ref[n]
            return tuple(jnp.maximum(m_run[j], jnp.max(qk(j, kp), axis=0, keepdims=True) + mask_ref[j, pl.ds(n, 1), :])
                         for j in range(nh))

        attend(lax.fori_loop(0, n_far, far_max, tuple(near_max)))

    outs = [acc_ref[j][:HEAD_DIM] / acc_ref[j][HEAD_DIM:HEAD_DIM + 1] for j in range(nh)]
    o_ref[0] = jnp.concatenate(outs, axis=0).T.astype(BF16)


def _moba(farb, qT, k3, vT3, kmean, bias, *, batch, seq):
    nb = seq // MOBA_BLOCK
    assert nb % 4 == 0, "far blocks are consumed four per loop trip"
    d_attn = qT.shape[1]
    nh = MOBA_HEADS_PER_STEP
    hw = nh * HEAD_DIM
    kern = functools.partial(_moba_kernel, nb=nb)
    return pl.pallas_call(
        kern,
        grid=(batch, d_attn // hw, nb),
        in_specs=[
            pl.BlockSpec(memory_space=pltpu.SMEM),
            pl.BlockSpec((1, hw, MOBA_BLOCK), lambda b, h, i: (b, h, i)),
            pl.BlockSpec((nb, MOBA_BLOCK, hw), lambda b, h, i: (b, 0, h)),
            pl.BlockSpec((nb, nh * V_ROWS, MOBA_BLOCK), lambda b, h, i: (b, h, 0)),
            pl.BlockSpec((1, nb, hw), lambda b, h, i: (b, 0, h)),
            pl.BlockSpec((nh, 2 * MOBA_BLOCK, MOBA_BLOCK), lambda b, h, i: (h, 0, 0), pipeline_mode=pl.Buffered(1)),
        ],
        out_specs=pl.BlockSpec((1, MOBA_BLOCK, hw), lambda b, h, i: (b, i, h)),
        out_shape=jax.ShapeDtypeStruct((batch, seq, d_attn), BF16),
        scratch_shapes=[
            pltpu.VMEM((nh, nb, MOBA_BLOCK), F32),
            pltpu.VMEM((nh, 2 * MOBA_BLOCK, MOBA_BLOCK), F32),
            pltpu.VMEM((nh, 2 * MOBA_BLOCK, MOBA_BLOCK), BF16),
            pltpu.VMEM((nh, 2 * MOBA_BLOCK, MOBA_BLOCK), BF16),
            pltpu.VMEM((nh, V_ROWS, MOBA_BLOCK), F32),
        ],
        compiler_params=pltpu.CompilerParams(
            dimension_semantics=("arbitrary", "arbitrary", "arbitrary"),
            vmem_limit_bytes=VMEM_LIMIT),
    )(farb, qT, k3, vT3, kmean, bias)


def _route(logits):
    lane = lax.broadcasted_iota(jnp.int32, logits.shape, 1)
    lane_f = lane.astype(F32)
    big = float(ROUTER_LANES)
    ninf = -jnp.inf
    gmask = lane < N_GROUPS
    gmax = jnp.max(jnp.where(gmask, logits, ninf), axis=1, keepdims=True)
    gsum = jnp.sum(jnp.where(gmask, jnp.exp(logits - gmax), 0.0), axis=1, keepdims=True)
    g_val = 1.0 / gsum
    g_idx = jnp.min(jnp.where(gmask & (logits == gmax), lane_f, big), axis=1, keepdims=True)
    e_grp = ((lane - EXPERT_LANE0) >> 3).astype(F32)
    emask = (lane >= EXPERT_LANE0) & (lane < EXPERT_LANE0 + N_EXPERTS) & (e_grp == g_idx)
    el = jnp.where(emask, logits, ninf)
    e1 = jnp.max(el, axis=1, keepdims=True)
    i1 = jnp.min(jnp.where(emask & (el == e1), lane_f, big), axis=1, keepdims=True)
    el2 = jnp.where(lane_f == i1, ninf, el)
    e2 = jnp.max(el2, axis=1, keepdims=True)
    i2 = jnp.min(jnp.where(emask & (lane_f != i1) & (el2 == e2), lane_f, big), axis=1, keepdims=True)
    tt = jnp.exp(e2 - e1)
    w1 = g_val / (1.0 + tt)
    w2 = g_val * tt / (1.0 + tt)
    first_low = i1 < i2
    lo = jnp.minimum(i1, i2) - (EXPERT_LANE0 + EXPERTS_PER_GROUP * g_idx)
    hi = jnp.maximum(i1, i2) - (EXPERT_LANE0 + EXPERTS_PER_GROUP * g_idx)
    pair = lo * (2 * EXPERTS_PER_GROUP - 1 - lo) * 0.5 + (hi - lo - 1.0)
    cls = g_idx * PAIRS_PER_GROUP + pair
    return cls, jnp.where(first_low, w1, w2), jnp.where(first_low, w2, w1)


def _col_to_row(col):
    n = col.shape[0]
    r = lax.broadcasted_iota(jnp.int32, (n, n), 0)
    c = lax.broadcasted_iota(jnp.int32, (n, n), 1)
    return jnp.sum(jnp.where(r == c, col, 0.0), axis=0, keepdims=True)


def _merge_kernel(x_ref, ys_ref, ya_ref, gates_ref, wus_ref, wua_ref, wout_ref, g2_ref, wr_ref, br_ref,
                  tri_ref, x1e_ref, cls_ref, rank_ref, cnt_ref):
    d = x_ref.shape[1]
    a = jnp.dot(ys_ref[...], wus_ref[...], preferred_element_type=F32)
    b = jnp.dot(ya_ref[...], wua_ref[...], preferred_element_type=F32)
    gts = gates_ref[...].astype(F32)
    merged = _sigmoid(gts[:, :d]) * a + _sigmoid(gts[:, d:]) * b
    x1 = x_ref[...] + jnp.dot(merged.astype(BF16), wout_ref[...], preferred_element_type=F32)
    xn = _rms(x1, g2_ref[...]).astype(BF16)
    logits = jnp.dot(xn, wr_ref[...], preferred_element_type=F32) + br_ref[...]
    cls, wa, wb = _route(logits)
    lane = lax.broadcasted_iota(jnp.int32, logits.shape, 1)
    x1e_ref[:, :d] = x1
    x1e_ref[:, d:] = jnp.where(lane == 0, wa, jnp.where(lane == 1, wb, 0.0))
    onehot = lane.astype(F32) == cls
    before = jnp.dot(tri_ref[...], onehot.astype(BF16), preferred_element_type=F32)
    rank = jnp.sum(jnp.where(onehot, before, 0.0), axis=1, keepdims=True)
    cls_ref[0] = _col_to_row(cls).astype(jnp.int32)
    rank_ref[0] = _col_to_row(rank).astype(jnp.int32)
    cnt_ref[0] = jnp.sum(onehot.astype(F32), axis=0, keepdims=True).astype(jnp.int32)


def _merge(x2, ys, ya, gates, wus, wua, wout, g2, wr, br, *, tm):
    t, d = x2.shape
    nt = t // tm
    tri = jnp.tril(jnp.ones((tm, tm), F32), -1).astype(BF16)
    return pl.pallas_call(
        _merge_kernel,
        grid=(nt,),
        in_specs=[
            pl.BlockSpec((tm, d), lambda i: (i, 0)),
            pl.BlockSpec((tm, ys.shape[1]), lambda i: (i, 0)),
            pl.BlockSpec((tm, ya.shape[1]), lambda i: (i, 0)),
            pl.BlockSpec((tm, gates.shape[1]), lambda i: (i, 0)),
            _const_spec(wus.shape), _const_spec(wua.shape), _const_spec(wout.shape),
            _const_spec(g2.shape), _const_spec(wr.shape), _const_spec(br.shape), _const_spec(tri.shape),
        ],
        out_specs=[
            pl.BlockSpec((tm, d + ROUTER_LANES), lambda i: (i, 0)),
            pl.BlockSpec((1, 1, tm), lambda i: (i, 0, 0)),
            pl.BlockSpec((1, 1, tm), lambda i: (i, 0, 0)),
            pl.BlockSpec((1, 1, ROUTER_LANES), lambda i: (i, 0, 0)),
        ],
        out_shape=[
            jax.ShapeDtypeStruct((t, d + ROUTER_LANES), F32),
            jax.ShapeDtypeStruct((nt, 1, tm), jnp.int32),
            jax.ShapeDtypeStruct((nt, 1, tm), jnp.int32),
            jax.ShapeDtypeStruct((nt, 1, ROUTER_LANES), jnp.int32),
        ],
        compiler_params=pltpu.CompilerParams(
            dimension_semantics=("arbitrary",), vmem_limit_bytes=VMEM_LIMIT),
    )(x2, ys, ya, gates, wus, wua, wout, g2, wr, br, tri)


def _permute_rows(pos, src, *, scatter):
    n, width = src.shape
    w = SC_WINDOW
    idx = jnp.pad(pos.reshape(n // w, w), ((0, 0), (0, LANES - w)))
    mesh = plsc.VectorSubcoreMesh(core_axis_name="core", subcore_axis_name="subcore")
    idx_spec = pl.BlockSpec((1, LANES), index_map=lambda i: (i, 0))
    row_spec = pl.BlockSpec((w, width), index_map=lambda i: (i, 0))

    @pl.kernel(out_type=jax.ShapeDtypeStruct(src.shape, src.dtype), mesh=mesh, scratch_types=[])
    def permute(src_hbm, idx_hbm, dst_hbm):
        if scatter:
            def body(rows_vmem, idx_vmem):
                pltpu.sync_copy(rows_vmem, dst_hbm.at[idx_vmem.at[0, pl.ds(0, w)]])
            in_specs, out_specs, args = [row_spec, idx_spec], [], (src_hbm, idx_hbm)
        else:
            def body(idx_vmem, rows_vmem):
                pltpu.sync_copy(src_hbm.at[idx_vmem.at[0, pl.ds(0, w)]], rows_vmem)
            in_specs, out_specs, args = [idx_spec], [row_spec], (idx_hbm, dst_hbm)
        pltpu.emit_pipeline(body, grid=(n // w,), in_specs=in_specs, out_specs=out_specs,
                            core_axis_name=("core", "subcore"),
                            dimension_semantics=(pltpu.PARALLEL,))(*args)

    return permute(src, idx)


def _expert(xn, w1_ref, w3_ref, w2_ref):
    h1 = jnp.dot(xn, w1_ref[0], preferred_element_type=F32)
    h3 = jnp.dot(xn, w3_ref[0], preferred_element_type=F32)
    hid = (h1 * _sigmoid(h1) * h3).astype(BF16)
    return jnp.dot(hid, w2_ref[0], preferred_element_type=F32)


def _moe_kernel(tile_ref, ea_ref, eb_ref, lo_ref, hi_ref, first_ref, last_ref,
                x_ref, w1a_ref, w3a_ref, w2a_ref, w1b_ref, w3b_ref, w2b_ref, g2_ref, gf_ref,
                y_ref, acc_ref):
    w = pl.program_id(0)
    d = y_ref.shape[1]
    lo = lo_ref[w]
    hi = hi_ref[w]

    @pl.when(w == 0)
    def _():
        acc_ref[...] = jnp.zeros_like(acc_ref)

    @pl.when(hi > lo)
    def _():
        x1 = x_ref[:, :d]
        row = lax.broadcasted_iota(jnp.int32, (x1.shape[0], 1), 0)
        inseg = (row >= lo) & (row < hi)
        ca = jnp.where(inseg, x_ref[:, d:d + 1], 0.0)
        cb = jnp.where(inseg, x_ref[:, d + 1:d + 2], 0.0)
        xn = _rms(x1, g2_ref[...]).astype(BF16)
        contrib = (ca * _expert(xn, w1a_ref, w3a_ref, w2a_ref)
                   + cb * _expert(xn, w1b_ref, w3b_ref, w2b_ref))

        acc_ref[...] = jnp.where(first_ref[w] == 1, 0.0, acc_ref[...]) + contrib

        @pl.when(last_ref[w] == 1)
        def _():
            y_ref[...] = _rms(x_ref[:, :d] + acc_ref[...], gf_ref[...])


def _moe(items, x1s, w1, w3, w2, g2, gf, *, rows):
    t, de_ = x1s.shape
    d = g2.shape[1]
    ne, _, de = w1.shape
    n_items = items[0].shape[0]

    def wa_map(w, tile, ea, eb, lo, hi, first, last):
        return (ea[w], 0, 0)

    def wb_map(w, tile, ea, eb, lo, hi, first, last):
        return (eb[w], 0, 0)

    def row_map(w, tile, ea, eb, lo, hi, first, last):
        return (tile[w], 0)

    def const_map(w, *_):
        return (0, 0)

    grid_spec = pltpu.PrefetchScalarGridSpec(
        num_scalar_prefetch=len(items),
        grid=(n_items,),
        in_specs=[
            pl.BlockSpec((rows, de_), row_map),
            pl.BlockSpec((1, d, de), wa_map), pl.BlockSpec((1, d, de), wa_map), pl.BlockSpec((1, de, d), wa_map),
            pl.BlockSpec((1, d, de), wb_map), pl.BlockSpec((1, d, de), wb_map), pl.BlockSpec((1, de, d), wb_map),
            pl.BlockSpec(g2.shape, const_map), pl.BlockSpec(gf.shape, const_map),
        ],
        out_specs=pl.BlockSpec((rows, d), row_map),
        scratch_shapes=[pltpu.VMEM((rows, d), F32)],
    )
    return pl.pallas_call(
        _moe_kernel,
        grid_spec=grid_spec,
        out_shape=jax.ShapeDtypeStruct((t, d), F32),
        compiler_params=pltpu.CompilerParams(
            dimension_semantics=("arbitrary",), vmem_limit_bytes=VMEM_LIMIT),
    )(*items, x1s, w1, w3, w2, w1, w3, w2, g2, gf)


def _pair_table():
    lo, hi = [], []
    for a in range(EXPERTS_PER_GROUP):
        for b in range(a + 1, EXPERTS_PER_GROUP):
            lo.append(a)
            hi.append(b)
    return jnp.asarray(lo, jnp.int32), jnp.asarray(hi, jnp.int32)


def _before_sum(v):
    ix = jnp.arange(v.shape[0])
    return jnp.sum(jnp.where(ix[None, :] < ix[:, None], v[None, :], 0), axis=1)


def _place(dest, vals, n):
    return jnp.sum(jnp.where(dest[None, :] == jnp.arange(n)[:, None], vals[None, :], 0), axis=1)


def _routing_tables(cls, rank, cnt, *, rows):
    nt, _, tm = cls.shape
    t = nt * tm
    lanes = cnt.shape[2]
    c = cnt[:, 0, :]
    tot = jnp.sum(c, axis=0)
    class_off = _before_sum(tot)
    tile_ix = jnp.arange(nt)
    tiles_before = jnp.sum(jnp.where((tile_ix[None, :] < tile_ix[:, None])[:, :, None], c[None], 0), axis=1)
    base = class_off[None, :] + tiles_before
    onehot = cls[:, 0, :, None] == jnp.arange(lanes)[None, None, :]
    pos = jnp.sum(jnp.where(onehot, base[:, None, :], 0), axis=2) + rank[:, 0, :]

    n_tiles = t // rows
    ta = jnp.arange(n_tiles, dtype=jnp.int32) * rows
    cb = class_off[1:N_CLASSES]
    n_items = n_tiles + N_CLASSES - 1
    idx = jnp.arange(n_items)
    at_a = jnp.arange(n_tiles) + jnp.sum(cb[None, :] <= ta[:, None], axis=1)
    at_b = jnp.arange(N_CLASSES - 1) + jnp.sum(ta[None, :] < cb[:, None], axis=1)
    starts = _place(at_a, ta, n_items) + _place(at_b, cb, n_items)
    ends = jnp.concatenate([starts[1:], jnp.full((1,), t, jnp.int32)])
    real = ends > starts
    n_real = jnp.sum(real)
    dest = jnp.where(real, _before_sum(real.astype(jnp.int32)), n_real + _before_sum(1 - real.astype(jnp.int32)))
    starts, ends = _place(dest, starts, n_items), _place(dest, ends, n_items)
    valid = idx < n_real
    last_real = jnp.maximum(n_real - 1, 0)
    starts = jnp.where(valid, starts, jnp.sum(jnp.where(idx == last_real, starts, 0)))
    tile = jnp.minimum(starts // rows, n_tiles - 1)
    klass = jnp.sum(class_off[None, :N_CLASSES] <= starts[:, None], axis=1) - 1
    pair_lo, pair_hi = _pair_table()
    grp = klass // PAIRS_PER_GROUP
    pair_hot = (klass % PAIRS_PER_GROUP)[:, None] == jnp.arange(PAIRS_PER_GROUP)[None, :]
    ea = grp * EXPERTS_PER_GROUP + jnp.sum(jnp.where(pair_hot, pair_lo[None, :], 0), axis=1)
    eb = grp * EXPERTS_PER_GROUP + jnp.sum(jnp.where(pair_hot, pair_hi[None, :], 0), axis=1)
    lo = jnp.where(valid, starts - tile * rows, 0)
    hi = jnp.where(valid, ends - tile * rows, 0)
    tile_prev = jnp.concatenate([tile[:1], tile[:-1]])
    tile_next = jnp.concatenate([tile[1:], tile[-1:]])
    first = valid & ((idx == 0) | (tile != tile_prev))
    last = valid & ((idx == last_real) | (tile != tile_next))
    items = tuple(a.astype(jnp.int32) for a in (tile, ea, eb, lo, hi, first, last))
    return pos.reshape(t).astype(jnp.int32), items


def _t5_bucket(dist):
    n = jnp.maximum(dist, 0)
    max_exact = NUM_BUCKETS // 2
    large = max_exact + (jnp.log(jnp.maximum(n, max_exact).astype(F32) / max_exact)
                         / math.log(MAX_DISTANCE / max_exact)
                         * (NUM_BUCKETS - max_exact)).astype(jnp.int32)
    return jnp.where(n < max_exact, n, jnp.minimum(large, NUM_BUCKETS - 1))


def _bias_tables(rel_bias):
    kk = jnp.arange(MOBA_BLOCK)[:, None]
    qq = jnp.arange(MOBA_BLOCK)[None, :]
    rb = rel_bias.astype(F32) * LOG2E
    d_own = qq - kk

    def lookup(dist):
        oh = jax.nn.one_hot(_t5_bucket(dist), NUM_BUCKETS, dtype=F32)
        return jnp.einsum('hn,kqn->hkq', rb, oh, precision=lax.Precision.HIGHEST)

    own = jnp.where((d_own >= 0)[None], lookup(d_own), NEG_INF)
    prv = lookup(d_own + MOBA_BLOCK)
    tab = jnp.concatenate([prv, own], axis=1)
    far = rb[:, NUM_BUCKETS - 1]
    return tab, far


def _ssm_params(lam_re, lam_im, log_step, b_re, b_im, c_re, c_im):
    g, p = lam_re.shape
    step = jnp.exp(log_step)[:, None]
    decay = jnp.exp(lam_re * step)
    a_re = decay * jnp.cos(lam_im * step)
    a_im = decay * jnp.sin(lam_im * step)
    denom = lam_re * lam_re + lam_im * lam_im
    nr, ni = a_re - 1.0, a_im
    coef_re = (nr * lam_re + ni * lam_im) / denom
    coef_im = (ni * lam_re - nr * lam_im) / denom
    bb_re = coef_re[..., None] * b_re - coef_im[..., None] * b_im
    bb_im = coef_re[..., None] * b_im + coef_im[..., None] * b_re
    eye = jnp.eye(g, dtype=F32)
    hc = b_re.shape[2]

    def in_mat(m):
        return (eye[:, None, :, None] * m.transpose(0, 2, 1)[:, :, None, :]).reshape(g * hc, g * p)

    def out_mat(m):
        return (eye[:, None, :, None] * m.transpose(0, 2, 1)[:, :, None, :]).reshape(g * p, g * hc)

    pc, ps = g * hc // SSM_PARTS, g * p // SSM_PARTS
    in_re, in_im, out_re, out_im = in_mat(bb_re), in_mat(bb_im), out_mat(c_re), out_mat(c_im)
    bb = jnp.stack([jnp.concatenate([m[h * pc:(h + 1) * pc, h * ps:(h + 1) * ps] for m in (in_re, in_im)], axis=1)
                    for h in range(SSM_PARTS)]).astype(BF16)
    cc = jnp.stack([jnp.concatenate([m[h * ps:(h + 1) * ps, h * pc:(h + 1) * pc] for m in (out_re, -out_im)], axis=0)
                    for h in range(SSM_PARTS)]).astype(BF16)
    return bb, cc, a_re.reshape(1, g * p), a_im.reshape(1, g * p)


def kernel(x, ln1_g, w_in, b_gate, ssm_lambda_re, ssm_lambda_im, ssm_log_step, ssm_b_re, ssm_b_im,
           ssm_c_re, ssm_c_im, ssm_d, w_glu, b_glu, w_up_ssm, w_up_attn, rel_bias, w_out, ln2_g,
           w_router_group, b_router_group, w_router_expert, b_router_expert, w1, w3, w2, ln_f_g):
    assert w_in.shape[0] == 1, "single-layer block"
    batch, seq, d = x.shape
    t = batch * seq
    d_ssm = w_glu.shape[1]
    d_attn = w_up_attn.shape[1]
    n_heads = d_attn // HEAD_DIM
    o1, o2, o3, o4 = d_ssm, d_ssm + d_attn, d_ssm + 2 * d_attn, d_ssm + 3 * d_attn
    x2 = x.reshape(t, d)

    wl = w_in[0]
    w_main = jnp.concatenate([wl[:, :o1], wl[:, o2:o3], wl[:, o4:]], axis=1).astype(BF16)
    wqT = (wl[:, o1:o2] * (HEAD_DIM ** -0.5 * LOG2E)).T.astype(BF16)
    wv = wl[:, o3:o4].T.reshape(n_heads, HEAD_DIM, d)
    wvT = jnp.concatenate([wv, jnp.zeros((n_heads, BF16_ROWS, d), F32)], axis=1)
    wvT = wvT.reshape(n_heads * V_ROWS, d).astype(BF16)
    vb = jnp.concatenate([jnp.zeros((n_heads, HEAD_DIM, 1), F32), jnp.ones((n_heads, BF16_ROWS, 1), F32)],
                         axis=1).reshape(n_heads * V_ROWS, 1)
    u, k3, kmean, qT, vT3, gates = _inproj(
        x2, ln1_g[0][None], w_main, wqT, wvT, vb, b_gate[0][None], batch=batch, seq=seq, tm=512)

    bb, cc, ar, ai = _ssm_params(ssm_lambda_re[0], ssm_lambda_im[0], ssm_log_step[0],
                                 ssm_b_re[0], ssm_b_im[0], ssm_c_re[0], ssm_c_im[0])
    y_ssm = _s5(u.reshape(batch, seq, d_ssm), bb, cc, ar, ai, ssm_d[0].reshape(1, d_ssm),
                w_glu[0].astype(BF16), b_glu[0][None], ts=128, chunk=1024)

    bias, far = _bias_tables(rel_bias)
    y_attn = _moba(far, qT, k3, vT3, kmean.reshape(batch, seq // MOBA_BLOCK, d_attn), bias,
                   batch=batch, seq=seq)

    wr = jnp.zeros((d, ROUTER_LANES), F32)
    wr = wr.at[:, :N_GROUPS].set(w_router_group[0])
    wr = wr.at[:, EXPERT_LANE0:EXPERT_LANE0 + N_EXPERTS].set(w_router_expert[0]).astype(BF16)
    br = jnp.zeros((1, ROUTER_LANES), F32)
    br = br.at[0, :N_GROUPS].set(b_router_group[0])
    br = br.at[0, EXPERT_LANE0:EXPERT_LANE0 + N_EXPERTS].set(b_router_expert[0])
    x1e, cls, rank, cnt = _merge(
        x2, y_ssm.reshape(t, d_ssm), y_attn.reshape(t, d_attn), gates,
        w_up_ssm[0].astype(BF16), w_up_attn[0].astype(BF16), w_out[0].astype(BF16),
        ln2_g[0][None], wr, br, tm=512)

    pos, items = _routing_tables(cls, rank, cnt, rows=MOE_ROWS)
    x1s = _permute_rows(pos, x1e, scatter=True)
    ys = _moe(items, x1s, w1[0].astype(BF16), w3[0].astype(BF16), w2[0].astype(BF16),
              ln2_g[0][None], ln_f_g[None], rows=MOE_ROWS)
    y = _permute_rows(pos, ys, scatter=False)
    return y.reshape(batch, seq, d)
```
